```python
import jax, jax.numpy as jnp
from jax import lax
import numpy as np

D_MODEL = 2048
BATCH = 16
SEQ = 256
DEPTH = 1
DEC_BATCH = 8
DEC_SEQ = 4096
PAST_LEN = 256

GRID_W = 64
HEAD_SIZE = 64
D_RWKV = D_MODEL // 2
N_HEADS_RWKV = D_RWKV // HEAD_SIZE
DECAY_LORA = 64
AAA_LORA = 64
GATE_LORA = 128
D_POOL = D_MODEL - D_RWKV
POOL_WINDOWS = (2, 4, 8, 16)
N_POOL_GROUPS = len(POOL_WINDOWS)
POOL_GROUP = D_POOL // N_POOL_GROUPS
D_FF = -(-8 * D_MODEL // (3 * 256)) * 256
RWKV_COLS = 3 * D_RWKV + 2 * DECAY_LORA + 2 * AAA_LORA + GATE_LORA
IN_COLS = RWKV_COLS + D_POOL + 2 * D_MODEL
RWKV_SPLITS = [D_RWKV, 2 * D_RWKV, 3 * D_RWKV, 3 * D_RWKV + DECAY_LORA, 3 * D_RWKV + 2 * DECAY_LORA,
               3 * D_RWKV + 2 * DECAY_LORA + AAA_LORA, 3 * D_RWKV + 2 * DECAY_LORA + 2 * AAA_LORA]
NORM_EPS = 1e-6
GN_EPS = 64e-5

kernel_name = 'bidir_rwkv7_pool_prefix_dit_step'

F32 = jnp.float32


def rms_norm(x, g):
    xf = x.astype(F32)
    y = xf * lax.rsqrt(jnp.mean(xf * xf, axis=-1, keepdims=True) + NORM_EPS)
    return (y * g.astype(F32)).astype(x.dtype)


def centred_shift(u):
    zero = jnp.zeros_like(u[:, :1])
    prev = jnp.concatenate([zero, u[:, :-1]], axis=1)
    nxt = jnp.concatenate([u[:, 1:], zero], axis=1)
    return 0.5 * (prev + nxt) - u


def window_mean(u, w, axis):
    n = u.shape[axis]
    cs = jnp.cumsum(u.astype(F32), axis=axis)
    pad = [(0, 0)] * u.ndim
    pad[axis] = (1, 0)
    cs = jnp.pad(cs, pad)
    t = jnp.arange(n)
    lo = jnp.clip(t - w // 2, 0, n)
    hi = jnp.clip(t - w // 2 + w, 0, n)
    s = jnp.take(cs, hi, axis=axis) - jnp.take(cs, lo, axis=axis)
    shape = [1] * u.ndim
    shape[axis] = n
    return s / (hi - lo).astype(F32).reshape(shape)


def wkv_step(S, inp):
    r, w, k, v, kk, a = inp
    sa = jnp.einsum('bhvk,bhk->bhv', S, -kk)
    S = S * w[:, :, None, :] + sa[..., None] * (kk * a)[:, :, None, :] + v[..., None] * k[:, :, None, :]
    return S, jnp.einsum('bhvk,bhk->bhv', S, r)


def wkv_scan(S0, r, w, k, v, kk, a, reverse):
    xs = tuple(jnp.swapaxes(u, 0, 1) for u in (r, w, k, v, kk, a))
    S, ys = lax.scan(wkv_step, S0, xs, reverse=reverse)
    return S, jnp.swapaxes(ys, 0, 1)


def rwkv7_bidir(z, S0, lp):
    B, T, _ = z.shape
    H, N = N_HEADS_RWKV, HEAD_SIZE
    zf = z.astype(F32)
    r, k, v, wd_f, wd_b, ad_f, ad_b, gd = jnp.split(zf, RWKV_SPLITS, axis=-1)

    def heads(u):
        return u.reshape(B, T, H, N)

    kk = heads(k * lp['k_k'].astype(F32))
    kk = kk / jnp.maximum(jnp.sqrt(jnp.sum(kk * kk, axis=-1, keepdims=True)), 1e-12)
    r_h = heads(r)
    v_h = heads(v)
    g = jax.nn.sigmoid(gd) @ lp['g2'].astype(F32)
    k_a = lp['k_a'].astype(F32)
    r_k = lp['r_k'].astype(F32)
    outs, bonuses, finals = [], [], []
    for d, (wd, ad, rev) in enumerate(((wd_f, ad_f, False), (wd_b, ad_b, True))):
        w_log = -jax.nn.softplus(-(lp['w0'][d].astype(F32) + jnp.tanh(wd) @ lp['w2'][d].astype(F32))) - 0.5
        decay = jnp.exp(-jnp.exp(w_log))
        a = jax.nn.sigmoid(lp['a0'][d].astype(F32) + ad @ lp['a2'][d].astype(F32))
        k_d = heads(k * (1.0 + (a - 1.0) * k_a))
        S_d, y_d = wkv_scan(S0[:, d].astype(F32), r_h, heads(decay), k_d, v_h, kk, heads(a), rev)
        outs.append(y_d)
        bonuses.append(jnp.sum(r_h * k_d * r_k, axis=-1, keepdims=True) * v_h)
        finals.append(S_d)
    y = outs[0] + outs[1]
    mu = jnp.mean(y, axis=-1, keepdims=True)
    var = jnp.mean(jnp.square(y - mu), axis=-1, keepdims=True)
    y = (y - mu) * lax.rsqrt(var + GN_EPS) * lp['lnx_g'].astype(F32).reshape(H, N) + lp['lnx_b'].astype(F32).reshape(H, N)
    y = (y + bonuses[0] + bonuses[1]).reshape(B, T, D_RWKV) * g
    return y.astype(z.dtype), jnp.stack(finals, axis=1)


def pool_mixer(z, rows, lp):
    B, T, _ = z.shape
    outs = []
    for gi, w in enumerate(POOL_WINDOWS):
        zg = z[..., gi * POOL_GROUP:(gi + 1) * POOL_GROUP]
        if rows is None:
            m = window_mean(zg, w, 1)
        else:
            grid = zg.reshape(B, rows, GRID_W, POOL_GROUP)
            m = window_mean(window_mean(grid, w, 2), w, 1).reshape(B, T, POOL_GROUP)
        d = (m - zg.astype(F32)).astype(z.dtype)
        outs.append(d @ lp['pool_w'][gi])
    return jnp.concatenate(outs, axis=-1) * lp['pool_scale']


def trunk_layer(x, mod, S0, rows, lp):
    sh1, sc1, g1, sh2, sc2, g2 = jnp.split(mod[:, None, :], 6, axis=-1)
    h = rms_norm(x, lp['norm1_g']) * (1.0 + sc1) + sh1
    proj = h @ lp['w_in']
    za = proj[..., :RWKV_COLS]
    za = za + lp['shift_mu'] * centred_shift(za)
    zb = proj[..., RWKV_COLS:RWKV_COLS + D_POOL]
    gates = jax.nn.sigmoid(proj[..., RWKV_COLS + D_POOL:])
    gate_a, gate_b = gates[..., :D_MODEL], gates[..., D_MODEL:]
    ya, S_fin = rwkv7_bidir(za, S0, lp)
    yb = pool_mixer(zb, rows, lp)
    merged = gate_a * (ya @ lp['w_up_a']) + gate_b * (yb @ lp['w_up_b'])
    x = x + g1 * (merged @ lp['w_out'])
    h2 = rms_norm(x, lp['norm2_g']) * (1.0 + sc2) + sh2
    u = h2 @ lp['ffn_w13']
    x = x + g2 * ((jax.nn.silu(u[..., :D_FF]) * u[..., D_FF:]) @ lp['ffn_w2'])
    return x, S_fin


def setup_inputs(seed: int = 0) -> dict:
    key = jax.random.key(seed)
    ks = jax.random.split(key, 32)

    def nrm(k, shape, scale):
        return jax.random.normal(k, shape, F32) * scale

    H, N = N_HEADS_RWKV, HEAD_SIZE
    return {
        'x_prompt': nrm(ks[0], (BATCH, SEQ, D_MODEL), 1.0),
        'x_sample': nrm(ks[1], (DEC_BATCH, DEC_SEQ, D_MODEL), 1.0),
        'c': nrm(ks[2], (DEC_BATCH, D_MODEL), 1.0),
        'state_rwkv': nrm(ks[3], (DEC_BATCH, DEPTH, 2, H, N, N), 0.5),
        'c_ctx': nrm(ks[4], (D_MODEL,), 1.0),
        'w_mod': nrm(ks[5], (DEPTH, D_MODEL, 6 * D_MODEL), 0.5 * D_MODEL ** -0.5),
        'b_mod': nrm(ks[6], (DEPTH, 6 * D_MODEL), 0.01),
        'norm1_g': 1.0 + nrm(ks[7], (DEPTH, D_MODEL), 0.02),
        'w_in': nrm(ks[8], (DEPTH, D_MODEL, IN_COLS), D_MODEL ** -0.5),
        'shift_mu': jax.random.uniform(ks[9], (DEPTH, RWKV_COLS), F32),
        'w0': nrm(ks[10], (DEPTH, 2, D_RWKV), 0.5),
        'w2': nrm(ks[11], (DEPTH, 2, DECAY_LORA, D_RWKV), DECAY_LORA ** -0.5),
        'a0': nrm(ks[12], (DEPTH, 2, D_RWKV), 0.5),
        'a2': nrm(ks[13], (DEPTH, 2, AAA_LORA, D_RWKV), AAA_LORA ** -0.5),
        'g2': nrm(ks[14], (DEPTH, GATE_LORA, D_RWKV), GATE_LORA ** -0.5),
        'k_k': 0.85 + nrm(ks[15], (DEPTH, D_RWKV), 0.05),
        'k_a': 1.0 + nrm(ks[16], (DEPTH, D_RWKV), 0.05),
        'r_k': nrm(ks[17], (DEPTH, H, N), 0.1),
        'lnx_g': 1.0 + nrm(ks[18], (DEPTH, D_RWKV), 0.02),
        'lnx_b': nrm(ks[19], (DEPTH, D_RWKV), 0.01),
        'w_up_a': nrm(ks[20], (DEPTH, D_RWKV, D_MODEL), D_RWKV ** -0.5),
        'pool_w': nrm(ks[21], (DEPTH, N_POOL_GROUPS, POOL_GROUP, POOL_GROUP), POOL_GROUP ** -0.5),
        'pool_scale': 1.0 + nrm(ks[22], (DEPTH, D_POOL), 0.02),
        'w_up_b': nrm(ks[23], (DEPTH, D_POOL, D_MODEL), D_POOL ** -0.5),
        'w_out': nrm(ks[24], (DEPTH, D_MODEL, D_MODEL), D_MODEL ** -0.5),
        'norm2_g': 1.0 + nrm(ks[25], (DEPTH, D_MODEL), 0.02),
        'ffn_w13': nrm(ks[26], (DEPTH, D_MODEL, 2 * D_FF), D_MODEL ** -0.5),
        'ffn_w2': nrm(ks[27], (DEPTH, D_FF, D_MODEL), D_FF ** -0.5),
        'final_g': 1.0 + nrm(ks[28], (D_MODEL,), 0.02),
    }


def reference(x_prompt, x_sample, c, state_rwkv, c_ctx, w_mod, b_mod, norm1_g, w_in, shift_mu,
              w0, w2, a0, a2, g2, k_k, k_a, r_k, lnx_g, lnx_b, w_up_a, pool_w, pool_scale,
              w_up_b, w_out, norm2_g, ffn_w13, ffn_w2, final_g):
    rows = x_sample.shape[1] // GRID_W
    xp, xs = x_prompt, x_sample
    S0_ctx = jnp.zeros((x_prompt.shape[0], 2, N_HEADS_RWKV, HEAD_SIZE, HEAD_SIZE), F32)
    new_states = []
    for l in range(DEPTH):
        lp = {
            'norm1_g': norm1_g[l], 'w_in': w_in[l], 'shift_mu': shift_mu[l],
            'w0': w0[l], 'w2': w2[l], 'a0': a0[l], 'a2': a2[l], 'g2': g2[l],
            'k_k': k_k[l], 'k_a': k_a[l], 'r_k': r_k[l], 'lnx_g': lnx_g[l], 'lnx_b': lnx_b[l],
            'w_up_a': w_up_a[l], 'pool_w': pool_w[l], 'pool_scale': pool_scale[l],
            'w_up_b': w_up_b[l], 'w_out': w_out[l], 'norm2_g': norm2_g[l],
            'ffn_w13': ffn_w13[l], 'ffn_w2': ffn_w2[l],
        }
        mod_ctx = jax.nn.silu(c_ctx)[None, :] @ w_mod[l] + b_mod[l]
        mod_lat = jax.nn.silu(c) @ w_mod[l] + b_mod[l]
        xp, S_ctx = trunk_layer(xp, mod_ctx, S0_ctx, None, lp)
        new_states.append(S_ctx)
        xs, _ = trunk_layer(xs, mod_lat, state_rwkv[:, l], rows, lp)
    y_prompt = rms_norm(xp, final_g)
    y_sample = rms_norm(xs, final_g)
    new_state_rwkv = jnp.stack(new_states, axis=1).astype(x_prompt.dtype)
    return (y_prompt, y_sample, new_state_rwkv)
```

```python
import functools

import numpy as np
import jax
import jax.numpy as jnp
from jax import lax
from jax.experimental import pallas as pl
from jax.experimental.pallas import tpu as pltpu

F32 = jnp.float32
BF16 = jnp.bfloat16

HEAD = 64
POOL_WINDOWS = (2, 4, 8, 16)
GRID_W = 64
DECAY_LORA = 64
AAA_LORA = 64
GATE_LORA = 128
NORM_EPS = 1e-6
GN_EPS = 64e-5

LANES = 128
MXU_DIM = 256
HEADS_PER_GROUP = MXU_DIM // HEAD
VMEM_LIMIT = 56 * 1024 * 1024

CHUNK = 64
SEQ_TILE = 256


def _cparams(sem):
    return pltpu.CompilerParams(dimension_semantics=sem, vmem_limit_bytes=VMEM_LIMIT)


def _dot(a, b, dims=(((1,), (0,)), ((), ())), precision=None):
    return lax.dot_general(a, b, dims, precision=precision, preferred_element_type=F32)


_NN = (((1,), (0,)), ((), ()))
_NT = (((1,), (1,)), ((), ()))
_TN = (((0,), (0,)), ((), ()))


def _split2(x):
    hi = x.astype(BF16)
    lo = (x - hi.astype(F32)).astype(BF16)
    return hi, lo


def _split3(x):
    hi = x.astype(BF16)
    r1 = x - hi.astype(F32)
    mid = r1.astype(BF16)
    lo = (r1 - mid.astype(F32)).astype(BF16)
    return hi, mid, lo


def _mm(a, b, dims=_NN, passes=1):
    if passes == 1:
        return _dot(a.astype(BF16), b.astype(BF16), dims)
    if passes == 3:
        ah, al = _split2(a)
        bh, bl = _split2(b)
        return _dot(ah, bh, dims) + (_dot(ah, bl, dims) + _dot(al, bh, dims))
    return _dot(a, b, dims, precision=lax.Precision.HIGHEST)


def _mm_exact_lhs(a_bf16, b, dims=_NN):
    b0, b1, b2 = _split3(b)
    return _dot(a_bf16, b0, dims) + (_dot(a_bf16, b1, dims) + _dot(a_bf16, b2, dims))


def _mm_exact_rhs(a, b_bf16, dims=_NN):
    a0, a1, a2 = _split3(a)
    return _dot(a0, b_bf16, dims) + (_dot(a1, b_bf16, dims) + _dot(a2, b_bf16, dims))


def _sigmoid(x):
    return 1.0 / (1.0 + jnp.exp(-x))


def _mod_kernel(c_ref, w_ref, b_ref, o_ref):
    c = c_ref[...]
    s = (c * _sigmoid(c)).astype(BF16)
    o_ref[...] = _dot(s, w_ref[...].astype(BF16)) + b_ref[...]


def _mod(cond, w_mod, b_mod):
    nb, d = cond.shape
    rows = -(-nb // 16) * 16
    cond_p = jnp.pad(cond, ((0, rows - nb), (0, 0)))
    n = w_mod.shape[1]
    tn = 1024
    out = pl.pallas_call(
        _mod_kernel,
        grid=(n // tn,),
        in_specs=[pl.BlockSpec((rows, d), lambda j: (0, 0)),
                  pl.BlockSpec((d, tn), lambda j: (0, j)),
                  pl.BlockSpec((1, tn), lambda j: (0, j))],
        out_specs=pl.BlockSpec((rows, tn), lambda j: (0, j)),
        out_shape=jax.ShapeDtypeStruct((rows, n), F32),
        compiler_params=_cparams(("arbitrary",)),
        name="mod",
    )(cond_p, w_mod, b_mod.reshape(1, n))
    return out[:nb]


def _in_kernel(x_ref, g_ref, mod_ref, w_ref, o_ref, h_ref):
    @pl.when(pl.program_id(1) == 0)
    def _():
        x = x_ref[...]
        y = x * lax.rsqrt(jnp.mean(x * x, axis=-1, keepdims=True) + NORM_EPS)
        h = (y * g_ref[...]) * (1.0 + mod_ref[0, 1:2, :]) + mod_ref[0, 0:1, :]
        h_ref[...] = h.astype(BF16)

    o_ref[...] = _dot(h_ref[...], w_ref[...])


def _in_proj(x2, norm_g, mod3, tiles_per_mod_tokens, w_bf16, tm=512, tn=512):
    m, d = x2.shape
    n = w_bf16.shape[1]
    tpm = tiles_per_mod_tokens // tm
    return pl.pallas_call(
        _in_kernel,
        grid=(m // tm, n // tn),
        in_specs=[pl.BlockSpec((tm, d), lambda i, j: (i, 0)),
                  pl.BlockSpec((1, d), lambda i, j: (0, 0)),
                  pl.BlockSpec((1, 6, d), lambda i, j: (i // tpm, 0, 0)),
                  pl.BlockSpec((d, tn), lambda i, j: (0, j))],
        out_specs=pl.BlockSpec((tm, tn), lambda i, j: (i, j)),
        out_shape=jax.ShapeDtypeStruct((m, n), F32),
        scratch_shapes=[pltpu.VMEM((tm, d), BF16)],
        compiler_params=_cparams(("parallel", "arbitrary")),
        name="in_proj",
    )(x2, norm_g.reshape(1, d), mod3, w_bf16)


def _head_sum(x, ones_bd):
    return _mm_exact_rhs(x, ones_bd)


def _prep_kernel(main_ref, prev_ref, next_ref, mu_ref, w0_ref, w2_ref, a0_ref, a2_ref, g2_ref,
                 kkw_ref, ones_ref, r_o, k_o, v_o, kk_o, lwf_o, lwb_o, af_o, ab_o, g_o, *, n_tiles, d_rwkv):
    t = pl.program_id(1)
    rows = main_ref.shape[1]
    row = lax.broadcasted_iota(jnp.int32, (rows, 1), 0)
    has_prev = t > 0
    has_next = t < n_tiles - 1

    def shifted(c0, c1):
        z = main_ref[0, :, c0:c1]
        p = jnp.where(has_prev, prev_ref[0, 7:8, c0:c1], 0.0)
        n = jnp.where(has_next, next_ref[0, 0:1, c0:c1], 0.0)
        zp = jnp.where(row == 0, p, pltpu.roll(z, 1, 0))
        zn = jnp.where(row == rows - 1, n, pltpu.roll(z, rows - 1, 0))
        return z + mu_ref[:, c0:c1] * (0.5 * (zp + zn) - z)

    dr = d_rwkv
    r_o[0] = shifted(0, dr)
    k = shifted(dr, 2 * dr)
    k_o[0] = k
    v_o[0] = shifted(2 * dr, 3 * dr)

    kkv = k * kkw_ref[...]
    ones_bd = ones_ref[...]
    for j in range(dr // MXU_DIM):
        sl = slice(j * MXU_DIM, (j + 1) * MXU_DIM)
        x = kkv[:, sl]
        ss = _head_sum(x * x, ones_bd)
        kk_o[0, :, sl] = x / jnp.maximum(jnp.sqrt(ss), 1e-12)

    c = 3 * dr
    sm = shifted(c, c + 2 * DECAY_LORA + 2 * AAA_LORA + GATE_LORA)
    hp = lax.Precision.HIGHEST
    o = 0
    for d, lw_o in enumerate((lwf_o, lwb_o)):
        wd = jnp.tanh(sm[:, o:o + DECAY_LORA])
        o += DECAY_LORA
        xw = w0_ref[d:d + 1, :] + _dot(wd, w2_ref[d], precision=hp)
        u = -xw
        softplus = jnp.maximum(u, 0.0) + jnp.log1p(jnp.exp(-jnp.abs(u)))
        lw_o[0] = -jnp.exp(-softplus - 0.5)
    for d, a_o in enumerate((af_o, ab_o)):
        ad = sm[:, o:o + AAA_LORA]
        o += AAA_LORA
        a_o[0] = _sigmoid(a0_ref[d:d + 1, :] + _dot(ad, a2_ref[d], precision=hp))
    gd = _sigmoid(sm[:, o:o + GATE_LORA])
    g_o[0] = _dot(gd, g2_ref[...], precision=hp)


def _prep(proj3, mu_p, w0, w2, a0, a2, g2, k_k, ones_bd, d_rwkv, rw_cols_p):
    b, t, _ = proj3.shape
    tp = SEQ_TILE
    n_tiles = t // tp
    sub = tp // 8
    last8 = t // 8 - 1
    out_sd = jax.ShapeDtypeStruct((b, t, d_rwkv), F32)
    o_spec = pl.BlockSpec((1, tp, d_rwkv), lambda i, j: (i, j, 0))

    def full(a):
        nd = a.ndim
        return pl.BlockSpec(a.shape, lambda i, j: (0,) * nd)

    params = (mu_p, w0, w2, a0, a2, g2, k_k.reshape(1, d_rwkv), ones_bd)
    return pl.pallas_call(
        functools.partial(_prep_kernel, n_tiles=n_tiles, d_rwkv=d_rwkv),
        grid=(b, n_tiles),
        in_specs=[pl.BlockSpec((1, tp, rw_cols_p), lambda i, j: (i, j, 0)),
                  pl.BlockSpec((1, 8, rw_cols_p), lambda i, j: (i, jnp.maximum(j * sub - 1, 0), 0)),
                  pl.BlockSpec((1, 8, rw_cols_p), lambda i, j: (i, jnp.minimum((j + 1) * sub, last8), 0)),
                  ] + [full(a) for a in params],
        out_specs=[o_spec] * 9,
        out_shape=[out_sd] * 9,
        compiler_params=_cparams(("parallel", "arbitrary")),
        name="rwkv_prep",
    )(proj3, proj3, proj3, *params)


P_INV = 6
P_AA = 6
P_STATE = 6
P_OUT = 6


def _tri_inverse(lm, eye, same16, off32, off64):
    ld = jnp.where(same16, lm, 0.0)
    p = eye - ld
    l2 = _mm(ld, ld, passes=P_INV)
    p = p + _mm(p, l2, passes=P_INV)
    l4 = _mm(l2, l2, passes=P_INV)
    p = p + _mm(p, l4, passes=P_INV)
    l8 = _mm(l4, l4, passes=P_INV)
    p = p + _mm(p, l8, passes=P_INV)
    for off in (off32, off64):
        lo = jnp.where(off, lm, 0.0)
        p = p - _mm(p, _mm(lo, p, passes=P_INV), passes=P_INV)
    return p


def _scan_kernel(r_ref, k_ref, v_ref, kk_ref, lw_ref, a_ref, ka_ref, s0_ref, y_ref, sfin_ref, s_ref,
                 *, rev, n_sub, n_blk):
    c = CHUNK
    blk = pl.program_id(2)

    @pl.when(blk == 0)
    def _():
        s_ref[...] = s0_ref[0, 0]

    ti = lax.broadcasted_iota(jnp.int32, (c, c), 0)
    si = lax.broadcasted_iota(jnp.int32, (c, c), 1)
    before = (si > ti) if rev else (si < ti)
    upto = (si >= ti) if rev else (si <= ti)
    tri = jnp.where(upto, 1.0, 0.0).astype(BF16)
    eye = jnp.where(si == ti, 1.0, 0.0)
    same16 = (si // 16) == (ti // 16)
    same32 = (si // 32) == (ti // 32)
    off32 = jnp.logical_and(same32, jnp.logical_not(same16))
    off64 = jnp.logical_not(same32)
    lane_head = lax.broadcasted_iota(jnp.int32, (1, MXU_DIM), 1) // HEAD
    bd = (lax.broadcasted_iota(jnp.int32, (MXU_DIM, MXU_DIM), 0) // HEAD) == \
         (lax.broadcasted_iota(jnp.int32, (MXU_DIM, MXU_DIM), 1) // HEAD)
    ka = ka_ref[...]
    last_row = 0 if rev else c - 1

    order = range(n_sub - 1, -1, -1) if rev else range(n_sub)
    for j in order:
        sl = slice(j * c, (j + 1) * c)
        r = r_ref[0, sl, :]
        k = k_ref[0, sl, :]
        v = v_ref[0, sl, :]
        kk = kk_ref[0, sl, :]
        lw = lw_ref[0, sl, :]
        a = a_ref[0, sl, :]

        cum = _mm_exact_lhs(tri, lw)
        e_in = jnp.exp(cum)
        e_ex = jnp.exp(cum - lw)
        e_inv = jnp.exp(-cum)
        al = kk * e_ex
        rh = r * e_in
        be = (kk * a) * e_inv
        kap = (k * (1.0 + (a - 1.0) * ka)) * e_inv
        ptot = e_in[last_row:last_row + 1, :]

        s = s_ref[...]
        z = _mm(al, s, _NT, passes=P_STATE)
        y = _mm(rh, s, _NT, passes=P_OUT)
        u = jnp.zeros((c, MXU_DIM), F32)
        for h in range(HEADS_PER_GROUP):
            mh = lane_head == h
            al_h = jnp.where(mh, al, 0.0)
            rh_h = jnp.where(mh, rh, 0.0)
            lm = jnp.where(before, _mm(al_h, be, _NT, passes=P_AA), 0.0)
            aak = jnp.where(before, _mm(al_h, kap, _NT, passes=P_AA), 0.0)
            bq = jnp.where(upto, _mm(rh_h, be, _NT, passes=P_OUT), 0.0)
            kq = jnp.where(upto, _mm(rh_h, kap, _NT, passes=P_OUT), 0.0)
            tinv = _tri_inverse(lm, eye, same16, off32, off64)
            rhs = z + _mm(aak, v, passes=P_AA)
            u_h = -_mm(tinv, rhs, passes=P_INV)
            y_h = _mm(bq, u_h, passes=P_OUT) + _mm(kq, v, passes=P_OUT)
            u = jnp.where(mh, u_h, u)
            y = y + jnp.where(mh, y_h, 0.0)
        y_ref[0, sl, :] = y
        ds = _mm(u, be, _TN, passes=P_STATE) + _mm(v, kap, _TN, passes=P_STATE)
        s_ref[...] = ptot * (s + jnp.where(bd, ds, 0.0))

    @pl.when(blk == n_blk - 1)
    def _():
        sfin_ref[0, 0] = s_ref[...]


def _scan(r, k, v, kk, lw, a, k_a, s0_bd, rev):
    b, t, d_rwkv = r.shape
    groups = d_rwkv // MXU_DIM
    tb = SEQ_TILE
    n_blk = t // tb
    n_sub = tb // CHUNK

    def tok_map(i, g, j):
        return (i, (n_blk - 1 - j) if rev else j, g)

    tok = pl.BlockSpec((1, tb, MXU_DIM), tok_map)
    st = pl.BlockSpec((1, 1, MXU_DIM, MXU_DIM), lambda i, g, j: (i, g, 0, 0))
    return pl.pallas_call(
        functools.partial(_scan_kernel, rev=rev, n_sub=n_sub, n_blk=n_blk),
        grid=(b, groups, n_blk),
        in_specs=[tok] * 6 + [pl.BlockSpec((1, MXU_DIM), lambda i, g, j: (0, g)), st],
        out_specs=[tok, st],
        out_shape=[jax.ShapeDtypeStruct((b, t, d_rwkv), F32),
                   jax.ShapeDtypeStruct((b, groups, MXU_DIM, MXU_DIM), F32)],
        scratch_shapes=[pltpu.VMEM((MXU_DIM, MXU_DIM), F32)],
        compiler_params=_cparams(("parallel", "parallel", "arbitrary")),
        name="scan_bwd" if rev else "scan_fwd",
    )(r, k, v, kk, lw, a, k_a.reshape(1, d_rwkv), s0_bd)


def _to_block_diag(s):
    b, h, n, _ = s.shape
    g = h // HEADS_PER_GROUP
    s5 = s.reshape(b, g, HEADS_PER_GROUP, n, n)
    eye = jnp.eye(HEADS_PER_GROUP, dtype=s.dtype)
    out = s5[:, :, :, :, None, :] * eye[None, None, :, None, :, None]
    return out.reshape(b, g, HEADS_PER_GROUP * n, HEADS_PER_GROUP * n)


def _from_block_diag(sbd):
    b, g, m, _ = sbd.shape
    n = m // HEADS_PER_GROUP
    s6 = sbd.reshape(b, g, HEADS_PER_GROUP, n, HEADS_PER_GROUP, n)
    idx = jnp.arange(HEADS_PER_GROUP)
    diag = s6[:, :, idx, :, idx, :]
    return jnp.moveaxis(diag, 0, 2).reshape(b, g * HEADS_PER_GROUP, n, n)


def _post_kernel(yf_ref, yb_ref, r_ref, k_ref, v_ref, af_ref, ab_ref, g_ref,
                 rk_ref, ka_ref, lg_ref, lb_ref, ones_ref, o_ref):
    ones_bd = ones_ref[...]
    inv_n = 1.0 / HEAD
    y = yf_ref[0] + yb_ref[0]
    mu = _head_sum(y, ones_bd) * inv_n
    d = y - mu
    var = _head_sum(d * d, ones_bd) * inv_n
    yn = d * lax.rsqrt(var + GN_EPS) * lg_ref[...] + lb_ref[...]
    ka = ka_ref[...]
    k = k_ref[0]
    kd_sum = k * (1.0 + (af_ref[0] - 1.0) * ka) + k * (1.0 + (ab_ref[0] - 1.0) * ka)
    bonus = _head_sum(r_ref[0] * kd_sum * rk_ref[...], ones_bd) * v_ref[0]
    o_ref[0] = ((yn + bonus) * g_ref[0]).astype(BF16)


def _post(yf, yb, r, k, v, af, ab, g, r_k, k_a, lnx_g, lnx_b, ones_bd):
    b, t, d_rwkv = yf.shape
    tp = min(512, t)
    tok = pl.BlockSpec((1, tp, MXU_DIM), lambda i, j, q: (i, j, q))
    par = pl.BlockSpec((1, MXU_DIM), lambda i, j, q: (0, q))
    return pl.pallas_call(
        _post_kernel,
        grid=(b, t // tp, d_rwkv // MXU_DIM),
        in_specs=[tok] * 8 + [par] * 4 + [pl.BlockSpec((MXU_DIM, MXU_DIM), lambda i, j, q: (0, 0))],
        out_specs=tok,
        out_shape=jax.ShapeDtypeStruct((b, t, d_rwkv), BF16),
        compiler_params=_cparams(("parallel", "parallel", "arbitrary")),
        name="rwkv_post",
    )(yf, yb, r, k, v, af, ab, g, r_k.reshape(1, d_rwkv), k_a.reshape(1, d_rwkv),
      lnx_g.reshape(1, d_rwkv), lnx_b.reshape(1, d_rwkv), ones_bd)


def _window_bounds(n, w):
    t = np.arange(n)
    lo = np.clip(t - w // 2, 0, n)
    hi = np.clip(t - w // 2 + w, 0, n)
    return lo, hi


def _window_matrix(n, w):
    lo, hi = _window_bounds(n, w)
    s = np.arange(n)[None, :]
    m = ((s >= lo[:, None]) & (s < hi[:, None])).astype(np.float32)
    return m / (hi - lo)[:, None].astype(np.float32)


def _pool_ctx_kernel(z_ref, band_ref, pw_ref, ps_ref, o_ref):
    z = z_ref[0]
    m = _dot(band_ref[0], z, precision=lax.Precision.HIGHEST)
    d = (m - z).astype(BF16)
    o_ref[0] = (_dot(d, pw_ref[0]) * ps_ref[...]).astype(BF16)


def _pool_ctx(proj3, col0, pool_w_bf16, pool_scale):
    b, t, _ = proj3.shape
    ng, pg, _ = pool_w_bf16.shape
    band = jnp.asarray(np.stack([_window_matrix(t, w) for w in POOL_WINDOWS]))
    cb = col0 // pg
    return pl.pallas_call(
        _pool_ctx_kernel,
        grid=(b, ng),
        in_specs=[pl.BlockSpec((1, t, pg), lambda i, q: (i, 0, cb + q)),
                  pl.BlockSpec((1, t, t), lambda i, q: (q, 0, 0)),
                  pl.BlockSpec((1, pg, pg), lambda i, q: (q, 0, 0)),
                  pl.BlockSpec((1, pg), lambda i, q: (0, q))],
        out_specs=pl.BlockSpec((1, t, pg), lambda i, q: (i, 0, q)),
        out_shape=jax.ShapeDtypeStruct((b, t, ng * pg), BF16),
        compiler_params=_cparams(("parallel", "arbitrary")),
        name="pool_ctx",
    )(proj3, band, pool_w_bf16, pool_scale.reshape(1, ng * pg))


def _pool_lat_kernel(z_ref, band_ref, pw_ref, ps_ref, o_ref, m1_ref, cs_ref, *, w, n_rows):
    tile = band_ref.shape[0]
    n_tok = n_rows * GRID_W
    band = band_ref[...]
    for i in range(n_tok // tile):
        sl = slice(i * tile, (i + 1) * tile)
        m1_ref[sl, :] = _dot(band, z_ref[0, sl, :], precision=lax.Precision.HIGHEST)
    cs_ref[0:GRID_W, :] = jnp.zeros((GRID_W, cs_ref.shape[1]), F32)
    for rr in range(n_rows):
        cs_ref[(rr + 1) * GRID_W:(rr + 2) * GRID_W, :] = (
            cs_ref[rr * GRID_W:(rr + 1) * GRID_W, :] + m1_ref[rr * GRID_W:(rr + 1) * GRID_W, :])
    lo, hi = _window_bounds(n_rows, w)
    for rr in range(n_rows):
        l, h = int(lo[rr]), int(hi[rr])
        m2 = (cs_ref[h * GRID_W:(h + 1) * GRID_W, :] - cs_ref[l * GRID_W:(l + 1) * GRID_W, :]) / float(h - l)
        sl = slice(rr * GRID_W, (rr + 1) * GRID_W)
        m1_ref[sl, :] = m2 - z_ref[0, sl, :]
    pw = pw_ref[...]
    ps = ps_ref[...]
    for i in range(n_tok // tile):
        sl = slice(i * tile, (i + 1) * tile)
        o_ref[0, sl, :] = (_dot(m1_ref[sl, :].astype(BF16), pw) * ps).astype(BF16)


def _pool_lat_group(proj3, col0, gi, pool_w_bf16, pool_scale):
    b, t, _ = proj3.shape
    ng, pg, _ = pool_w_bf16.shape
    w = POOL_WINDOWS[gi]
    n_rows = t // GRID_W
    tile = MXU_DIM
    band = jnp.asarray(np.kron(np.eye(tile // GRID_W, dtype=np.float32), _window_matrix(GRID_W, w)))
    cb = col0 // pg + gi
    return pl.pallas_call(
        functools.partial(_pool_lat_kernel, w=w, n_rows=n_rows),
        grid=(b,),
        in_specs=[pl.BlockSpec((1, t, pg), lambda i: (i, 0, cb)),
                  pl.BlockSpec((tile, tile), lambda i: (0, 0)),
                  pl.BlockSpec((pg, pg), lambda i: (0, 0)),
                  pl.BlockSpec((1, pg), lambda i: (0, 0))],
        out_specs=pl.BlockSpec((1, t, pg), lambda i: (i, 0, 0)),
        out_shape=jax.ShapeDtypeStruct((b, t, pg), BF16),
        scratch_shapes=[pltpu.VMEM((t, pg), F32), pltpu.VMEM((t + GRID_W, pg), F32)],
        compiler_params=_cparams(("parallel",)),
        name=f"pool_lat_w{w}",
    )(proj3, band, pool_w_bf16[gi], pool_scale.reshape(ng, pg)[gi:gi + 1])


def _up_kernel(ya_ref, yb_ref, wa_ref, wb_ref, ga_ref, gb_ref, o_ref):
    ua = _dot(ya_ref[...], wa_ref[...])
    ub = _dot(yb_ref[...], wb_ref[...])
    o_ref[...] = (_sigmoid(ga_ref[...]) * ua + _sigmoid(gb_ref[...]) * ub).astype(BF16)


def _up(ya2, yb2, wa, wb, proj2, gate_col0, tm=512, tn=512):
    m, ka = ya2.shape
    kb = yb2.shape[1]
    n = wa.shape[1]
    ca = gate_col0 // tn
    cb = (gate_col0 + n) // tn
    return pl.pallas_call(
        _up_kernel,
        grid=(m // tm, n // tn),
        in_specs=[pl.BlockSpec((tm, ka), lambda i, j: (i, 0)),
                  pl.BlockSpec((tm, kb), lambda i, j: (i, 0)),
                  pl.BlockSpec((ka, tn), lambda i, j: (0, j)),
                  pl.BlockSpec((kb, tn), lambda i, j: (0, j)),
                  pl.BlockSpec((tm, tn), lambda i, j: (i, ca + j)),
                  pl.BlockSpec((tm, tn), lambda i, j: (i, cb + j))],
        out_specs=pl.BlockSpec((tm, tn), lambda i, j: (i, j)),
        out_shape=jax.ShapeDtypeStruct((m, n), BF16),
        compiler_params=_cparams(("parallel", "arbitrary")),
        name="up_merge",
    )(ya2, yb2, wa, wb, proj2, proj2)


def _out_kernel(mg_ref, w_ref, x_ref, mod_ref, g_ref, x1_ref, h2_ref):
    acc = _dot(mg_ref[...], w_ref[...])
    x1 = x_ref[...] + mod_ref[0, 2:3, :] * acc
    x1_ref[...] = x1
    y = x1 * lax.rsqrt(jnp.mean(x1 * x1, axis=-1, keepdims=True) + NORM_EPS)
    h2 = (y * g_ref[...]) * (1.0 + mod_ref[0, 4:5, :]) + mod_ref[0, 3:4, :]
    h2_ref[...] = h2.astype(BF16)


def _out_proj(merged, w_out, x2, mod3, tiles_per_mod_tokens, norm2_g, tm=256):
    m, d = x2.shape
    tpm = tiles_per_mod_tokens // tm
    return pl.pallas_call(
        _out_kernel,
        grid=(m // tm,),
        in_specs=[pl.BlockSpec((tm, d), lambda i: (i, 0)),
                  pl.BlockSpec((d, d), lambda i: (0, 0)),
                  pl.BlockSpec((tm, d), lambda i: (i, 0)),
                  pl.BlockSpec((1, 6, d), lambda i: (i // tpm, 0, 0)),
                  pl.BlockSpec((1, d), lambda i: (0, 0))],
        out_specs=[pl.BlockSpec((tm, d), lambda i: (i, 0)),
                   pl.BlockSpec((tm, d), lambda i: (i, 0))],
        out_shape=[jax.ShapeDtypeStruct((m, d), F32), jax.ShapeDtypeStruct((m, d), BF16)],
        compiler_params=_cparams(("parallel",)),
        name="out_proj",
    )(merged, w_out, x2, mod3, norm2_g.reshape(1, d))


def _ffn1_kernel(h_ref, w1_ref, w3_ref, o_ref):
    h = h_ref[...]
    u1 = _dot(h, w1_ref[...])
    u3 = _dot(h, w3_ref[...])
    o_ref[...] = (u1 * _sigmoid(u1) * u3).astype(BF16)


def _ffn1(h2, w13, d_ff, tm=512, tn=512):
    m, d = h2.shape
    nj = d_ff // tn
    return pl.pallas_call(
        _ffn1_kernel,
        grid=(m // tm, nj),
        in_specs=[pl.BlockSpec((tm, d), lambda i, j: (i, 0)),
                  pl.BlockSpec((d, tn), lambda i, j: (0, j)),
                  pl.BlockSpec((d, tn), lambda i, j: (0, nj + j))],
        out_specs=pl.BlockSpec((tm, tn), lambda i, j: (i, j)),
        out_shape=jax.ShapeDtypeStruct((m, d_ff), BF16),
        compiler_params=_cparams(("parallel", "arbitrary")),
        name="ffn1",
    )(h2, w13, w13)


def _ffn2_kernel(a_ref, w_ref, x_ref, mod_ref, g_ref, o_ref, acc_ref, *, nk):
    kk = pl.program_id(1)

    @pl.when(kk == 0)
    def _():
        acc_ref[...] = jnp.zeros_like(acc_ref)

    acc_ref[...] += _dot(a_ref[...], w_ref[...])

    @pl.when(kk == nk - 1)
    def _():
        x2 = x_ref[...] + mod_ref[0, 5:6, :] * acc_ref[...]
        y = x2 * lax.rsqrt(jnp.mean(x2 * x2, axis=-1, keepdims=True) + NORM_EPS)
        o_ref[...] = y * g_ref[...]


def _ffn2(act, w2, x1, mod3, tiles_per_mod_tokens, final_g, tm=512, tk=512):
    m, d = x1.shape
    d_ff = act.shape[1]
    nk = d_ff // tk
    tpm = tiles_per_mod_tokens // tm
    return pl.pallas_call(
        functools.partial(_ffn2_kernel, nk=nk),
        grid=(m // tm, nk),
        in_specs=[pl.BlockSpec((tm, tk), lambda i, j: (i, j)),
                  pl.BlockSpec((tk, d), lambda i, j: (j, 0)),
                  pl.BlockSpec((tm, d), lambda i, j: (i, 0)),
                  pl.BlockSpec((1, 6, d), lambda i, j: (i // tpm, 0, 0)),
                  pl.BlockSpec((1, d), lambda i, j: (0, 0))],
        out_specs=pl.BlockSpec((tm, d), lambda i, j: (i, 0)),
        out_shape=jax.ShapeDtypeStruct((m, d), F32),
        scratch_shapes=[pltpu.VMEM((tm, d), F32)],
        compiler_params=_cparams(("parallel", "arbitrary")),
        name="ffn2",
    )(act, w2, x1, mod3, final_g.reshape(1, d))


def _trunk(x, mod, s0, latent, p):
    b, t, d = x.shape
    m = b * t
    d_rwkv = p["d_rwkv"]
    d_pool = d - d_rwkv
    heads = d_rwkv // HEAD
    x2 = x.reshape(m, d)
    nb = mod.shape[0]
    mod3 = mod.reshape(nb, 6, d)
    tokens_per_mod = t if nb == b else m

    proj = _in_proj(x2, p["norm1_g"], mod3, tokens_per_mod, p["w_in"])
    proj3 = proj.reshape(b, t, proj.shape[1])

    r, k, v, kk, lwf, lwb, af, ab, g = _prep(
        proj3, p["mu"], p["w0"], p["w2"], p["a0"], p["a2"], p["g2"], p["k_k"], p["ones_bd"],
        d_rwkv, p["rw_cols_p"])

    if s0 is None:
        s0 = jnp.zeros((b, 2, heads, HEAD, HEAD), F32)
    yf, sf = _scan(r, k, v, kk, lwf, af, p["k_a"], _to_block_diag(s0[:, 0]), rev=False)
    yb, sb = _scan(r, k, v, kk, lwb, ab, p["k_a"], _to_block_diag(s0[:, 1]), rev=True)
    s_fin = jnp.stack([_from_block_diag(sf), _from_block_diag(sb)], axis=1)
    ya = _post(yf, yb, r, k, v, af, ab, g, p["r_k"], p["k_a"], p["lnx_g"], p["lnx_b"], p["ones_bd"])

    pool_col0 = p["rw_cols_p"]
    if latent:
        yb_pool = jnp.concatenate(
            [_pool_lat_group(proj3, pool_col0, gi, p["pool_w"], p["pool_scale"])
             for gi in range(len(POOL_WINDOWS))], axis=-1)
    else:
        yb_pool = _pool_ctx(proj3, pool_col0, p["pool_w"], p["pool_scale"])

    merged = _up(ya.reshape(m, d_rwkv), yb_pool.reshape(m, d_pool), p["w_up_a"], p["w_up_b"],
                 proj, pool_col0 + d_pool)
    x1, h2 = _out_proj(merged, p["w_out"], x2, mod3, tokens_per_mod, p["norm2_g"])
    act = _ffn1(h2, p["ffn_w13"], p["d_ff"])
    y = _ffn2(act, p["ffn_w2"], x1, mod3, tokens_per_mod, p["final_g"])
    return y.reshape(b, t, d), s_fin


def kernel(x_prompt, x_sample, c, state_rwkv, c_ctx, w_mod, b_mod, norm1_g, w_in, shift_mu, w0, w2, a0, a2, g2,
           k_k, k_a, r_k, lnx_g, lnx_b, w_up_a, pool_w, pool_scale, w_up_b, w_out, norm2_g, ffn_w13, ffn_w2,
           final_g):
    depth = w_in.shape[0]
    assert depth == 1, "final norm is fused into the last layer: single-layer trunk only"
    d = x_prompt.shape[-1]
    d_rwkv = w_up_a.shape[1]
    rw_cols = shift_mu.shape[1]
    rw_cols_p = -(-rw_cols // 512) * 512
    assert x_prompt.shape[1] == SEQ_TILE and x_sample.shape[1] % SEQ_TILE == 0

    l = 0
    w_in_l = w_in[l]
    w_in_p = jnp.concatenate(
        [w_in_l[:, :rw_cols], jnp.zeros((d, rw_cols_p - rw_cols), F32), w_in_l[:, rw_cols:]], axis=1).astype(BF16)
    ones_bd = jnp.asarray(np.kron(np.eye(HEADS_PER_GROUP, dtype=np.float32),
                                  np.ones((HEAD, HEAD), np.float32))).astype(BF16)
    p = {
        "d_rwkv": d_rwkv, "rw_cols_p": rw_cols_p, "d_ff": ffn_w2.shape[1],
        "norm1_g": norm1_g[l], "w_in": w_in_p,
        "mu": jnp.pad(shift_mu[l], (0, rw_cols_p - rw_cols)).reshape(1, rw_cols_p),
        "w0": w0[l], "w2": w2[l], "a0": a0[l], "a2": a2[l], "g2": g2[l],
        "k_k": k_k[l], "k_a": k_a[l], "r_k": r_k[l].reshape(-1), "lnx_g": lnx_g[l], "lnx_b": lnx_b[l],
        "ones_bd": ones_bd,
        "w_up_a": w_up_a[l].astype(BF16), "w_up_b": w_up_b[l].astype(BF16),
        "pool_w": pool_w[l].astype(BF16), "pool_scale": pool_scale[l],
        "w_out": w_out[l].astype(BF16), "norm2_g": norm2_g[l],
        "ffn_w13": ffn_w13[l].astype(BF16), "ffn_w2": ffn_w2[l].astype(BF16), "final_g": final_g,
    }
    cond = jnp.concatenate([c_ctx[None, :], c], axis=0)
    mod = _mod(cond, w_mod[l], b_mod[l])
    y_prompt, s_ctx = _trunk(x_prompt, mod[:1], None, False, p)
    y_sample, _ = _trunk(x_sample, mod[1:], state_rwkv[:, l], True, p)
    new_state = s_ctx[:, None].astype(x_prompt.dtype)
    return (y_prompt, y_sample, new_state)
```

```python
import functools

import numpy as np
import jax
import jax.numpy as jnp
from jax import lax
from jax.experimental import pallas as pl
from jax.experimental.pallas import tpu as pltpu

F32 = jnp.float32
BF16 = jnp.bfloat16

HEAD = 64
POOL_WINDOWS = (2, 4, 8, 16)
GRID_W = 64
DECAY_LORA = 64
AAA_LORA = 64
GATE_LORA = 128
NORM_EPS = 1e-6
GN_EPS = 64e-5

LANES = 128
MXU_DIM = 256
HEADS_PER_GROUP = MXU_DIM // HEAD
VMEM_LIMIT = 56 * 1024 * 1024

CHUNK = 64
SEQ_TILE = 256


def _cparams(sem):
    return pltpu.CompilerParams(dimension_semantics=sem, vmem_limit_bytes=VMEM_LIMIT)


def _dot(a, b, dims=(((1,), (0,)), ((), ())), precision=None):
    return lax.dot_general(a, b, dims, precision=precision, preferred_element_type=F32)


_NN = (((1,), (0,)), ((), ()))
_NT = (((1,), (1,)), ((), ()))
_TN = (((0,), (0,)), ((), ()))


def _split2(x):
    hi = x.astype(BF16)
    lo = (x - hi.astype(F32)).astype(BF16)
    return hi, lo


def _split3(x):
    hi = x.astype(BF16)
    r1 = x - hi.astype(F32)
    mid = r1.astype(BF16)
    lo = (r1 - mid.astype(F32)).astype(BF16)
    return hi, mid, lo


def _mm(a, b, dims=_NN, passes=1):
    if passes == 1:
        return _dot(a.astype(BF16), b.astype(BF16), dims)
    if passes == 3:
        ah, al = _split2(a)
        bh, bl = _split2(b)
        return _dot(ah, bh, dims) + (_dot(ah, bl, dims) + _dot(al, bh, dims))
    return _dot(a, b, dims, precision=lax.Precision.HIGHEST)


def _mm_exact_lhs(a_bf16, b, dims=_NN):
    b0, b1, b2 = _split3(b)
    return _dot(a_bf16, b0, dims) + (_dot(a_bf16, b1, dims) + _dot(a_bf16, b2, dims))


def _mm_exact_rhs(a, b_bf16, dims=_NN):
    a0, a1, a2 = _split3(a)
    return _dot(a0, b_bf16, dims) + (_dot(a1, b_bf16, dims) + _dot(a2, b_bf16, dims))


def _sigmoid(x):
    return 1.0 / (1.0 + jnp.exp(-x))


def _mod_kernel(c_ref, w_ref, b_ref, o_ref):
    c = c_ref[...]
    s = (c * _sigmoid(c)).astype(BF16)
    o_ref[...] = _dot(s, w_ref[...].astype(BF16)) + b_ref[...]


def _mod(cond, w_mod, b_mod):
    nb, d = cond.shape
    rows = -(-nb // 16) * 16
    cond_p = jnp.pad(cond, ((0, rows - nb), (0, 0)))
    n = w_mod.shape[1]
    tn = 1024
    out = pl.pallas_call(
        _mod_kernel,
        grid=(n // tn,),
        in_specs=[pl.BlockSpec((rows, d), lambda j: (0, 0)),
                  pl.BlockSpec((d, tn), lambda j: (0, j)),
                  pl.BlockSpec((1, tn), lambda j: (0, j))],
        out_specs=pl.BlockSpec((rows, tn), lambda j: (0, j)),
        out_shape=jax.ShapeDtypeStruct((rows, n), F32),
        compiler_params=_cparams(("arbitrary",)),
        name="mod",
    )(cond_p, w_mod, b_mod.reshape(1, n))
    return out[:nb]


def _in_kernel(x_ref, g_ref, mod_ref, w_ref, o_ref, h_ref):
    @pl.when(pl.program_id(1) == 0)
    def _():
        x = x_ref[...]
        y = x * lax.rsqrt(jnp.mean(x * x, axis=-1, keepdims=True) + NORM_EPS)
        h = (y * g_ref[...]) * (1.0 + mod_ref[0, 1:2, :]) + mod_ref[0, 0:1, :]
        h_ref[...] = h.astype(BF16)

    o_ref[...] = _dot(h_ref[...], w_ref[...])


def _in_proj(x2, norm_g, mod3, tiles_per_mod_tokens, w_bf16, tm=512, tn=512):
    m, d = x2.shape
    n = w_bf16.shape[1]
    tpm = tiles_per_mod_tokens // tm
    return pl.pallas_call(
        _in_kernel,
        grid=(m // tm, n // tn),
        in_specs=[pl.BlockSpec((tm, d), lambda i, j: (i, 0)),
                  pl.BlockSpec((1, d), lambda i, j: (0, 0)),
                  pl.BlockSpec((1, 6, d), lambda i, j: (i // tpm, 0, 0)),
                  pl.BlockSpec((d, tn), lambda i, j: (0, j))],
        out_specs=pl.BlockSpec((tm, tn), lambda i, j: (i, j)),
        out_shape=jax.ShapeDtypeStruct((m, n), F32),
        scratch_shapes=[pltpu.VMEM((tm, d), BF16)],
        compiler_params=_cparams(("parallel", "arbitrary")),
        name="in_proj",
    )(x2, norm_g.reshape(1, d), mod3, w_bf16)


def _head_sum(x, ones_bd):
    return _mm_exact_rhs(x, ones_bd)


def _prep_kernel(main_ref, prev_ref, next_ref, mu_ref, w0_ref, w2_ref, a0_ref, a2_ref, g2_ref,
                 kkw_ref, ones_ref, r_o, k_o, v_o, kk_o, lwf_o, lwb_o, af_o, ab_o, g_o, *, n_tiles, d_rwkv):
    t = pl.program_id(1)
    rows = main_ref.shape[1]
    row = lax.broadcasted_iota(jnp.int32, (rows, 1), 0)
    has_prev = t > 0
    has_next = t < n_tiles - 1

    def shifted(c0, c1):
        z = main_ref[0, :, c0:c1]
        p = jnp.where(has_prev, prev_ref[0, 7:8, c0:c1], 0.0)
        n = jnp.where(has_next, next_ref[0, 0:1, c0:c1], 0.0)
        zp = jnp.where(row == 0, p, pltpu.roll(z, 1, 0))
        zn = jnp.where(row == rows - 1, n, pltpu.roll(z, rows - 1, 0))
        return z + mu_ref[:, c0:c1] * (0.5 * (zp + zn) - z)

    dr = d_rwkv
    r_o[0] = shifted(0, dr)
    k = shifted(dr, 2 * dr)
    k_o[0] = k
    v_o[0] = shifted(2 * dr, 3 * dr)

    kkv = k * kkw_ref[...]
    ones_bd = ones_ref[...]
    for j in range(dr // MXU_DIM):
        sl = slice(j * MXU_DIM, (j + 1) * MXU_DIM)
        x = kkv[:, sl]
        ss = _head_sum(x * x, ones_bd)
        kk_o[0, :, sl] = x / jnp.maximum(jnp.sqrt(ss), 1e-12)

    c = 3 * dr
    sm = shifted(c, c + 2 * DECAY_LORA + 2 * AAA_LORA + GATE_LORA)
    hp = lax.Precision.HIGHEST
    o = 0
    for d, lw_o in enumerate((lwf_o, lwb_o)):
        wd = jnp.tanh(sm[:, o:o + DECAY_LORA])
        o += DECAY_LORA
        xw = w0_ref[d:d + 1, :] + _dot(wd, w2_ref[d], precision=hp)
        u = -xw
        softplus = jnp.maximum(u, 0.0) + jnp.log1p(jnp.exp(-jnp.abs(u)))
        lw_o[0] = -jnp.exp(-softplus - 0.5)
    for d, a_o in enumerate((af_o, ab_o)):
        ad = sm[:, o:o + AAA_LORA]
        o += AAA_LORA
        a_o[0] = _sigmoid(a0_ref[d:d + 1, :] + _dot(ad, a2_ref[d], precision=hp))
    gd = _sigmoid(sm[:, o:o + GATE_LORA])
    g_o[0] = _dot(gd, g2_ref[...], precision=hp)


def _prep(proj3, mu_p, w0, w2, a0, a2, g2, k_k, ones_bd, d_rwkv, rw_cols_p):
    b, t, _ = proj3.shape
    tp = SEQ_TILE
    n_tiles = t // tp
    sub = tp // 8
    last8 = t // 8 - 1
    out_sd = jax.ShapeDtypeStruct((b, t, d_rwkv), F32)
    o_spec = pl.BlockSpec((1, tp, d_rwkv), lambda i, j: (i, j, 0))

    def full(a):
        nd = a.ndim
        return pl.BlockSpec(a.shape, lambda i, j: (0,) * nd)

    params = (mu_p, w0, w2, a0, a2, g2, k_k.reshape(1, d_rwkv), ones_bd)
    return pl.pallas_call(
        functools.partial(_prep_kernel, n_tiles=n_tiles, d_rwkv=d_rwkv),
        grid=(b, n_tiles),
        in_specs=[pl.BlockSpec((1, tp, rw_cols_p), lambda i, j: (i, j, 0)),
                  pl.BlockSpec((1, 8, rw_cols_p), lambda i, j: (i, jnp.maximum(j * sub - 1, 0), 0)),
                  pl.BlockSpec((1, 8, rw_cols_p), lambda i, j: (i, jnp.minimum((j + 1) * sub, last8), 0)),
                  ] + [full(a) for a in params],
        out_specs=[o_spec] * 9,
        out_shape=[out_sd] * 9,
        compiler_params=_cparams(("parallel", "arbitrary")),
        name="rwkv_prep",
    )(proj3, proj3, proj3, *params)


def _block_diag_rhs(y, bd):
    yb = y.astype(BF16)
    tiled = jnp.concatenate([yb] * HEADS_PER_GROUP, axis=0)
    return jnp.where(bd, tiled, jnp.zeros_like(tiled))


def _mmc(x, y_bd):
    return _dot(x.astype(BF16), y_bd)


def _tri_inverse_cat(lm, eye, same16, off32, off64, bd):
    ld = jnp.where(same16, lm, 0.0)
    p = eye - ld
    ld_bd = _block_diag_rhs(ld, bd)
    l2 = _mmc(ld, ld_bd)
    l2_bd = _block_diag_rhs(l2, bd)
    p = p + _mmc(p, l2_bd)
    l4 = _mmc(l2, l2_bd)
    l4_bd = _block_diag_rhs(l4, bd)
    p = p + _mmc(p, l4_bd)
    l8 = _mmc(l4, l4_bd)
    p = p + _mmc(p, _block_diag_rhs(l8, bd))
    for off in (off32, off64):
        lo = jnp.where(off, lm, 0.0)
        p = p - _mmc(p, _block_diag_rhs(_mmc(lo, _block_diag_rhs(p, bd)), bd))
    return p


def _scan_kernel(r_ref, k_ref, v_ref, kk_ref, lw_ref, a_ref, ka_ref, s0_ref, y_ref, sfin_ref, s_ref,
                 *, rev, n_sub, n_blk):
    c = CHUNK
    blk = pl.program_id(2)

    @pl.when(blk == 0)
    def _():
        s_ref[...] = s0_ref[0, 0]

    ti = lax.broadcasted_iota(jnp.int32, (c, c), 0)
    si = lax.broadcasted_iota(jnp.int32, (c, c), 1)
    upto_cc = (si >= ti) if rev else (si <= ti)
    tri = jnp.where(upto_cc, 1.0, 0.0).astype(BF16)
    tc = lax.broadcasted_iota(jnp.int32, (c, MXU_DIM), 0)
    sc = lax.broadcasted_iota(jnp.int32, (c, MXU_DIM), 1) % c
    before = (sc > tc) if rev else (sc < tc)
    upto = (sc >= tc) if rev else (sc <= tc)
    eye = jnp.where(sc == tc, 1.0, 0.0)
    same16 = (sc // 16) == (tc // 16)
    same32 = (sc // 32) == (tc // 32)
    off32 = jnp.logical_and(same32, jnp.logical_not(same16))
    off64 = jnp.logical_not(same32)
    bd = (lax.broadcasted_iota(jnp.int32, (MXU_DIM, MXU_DIM), 0) // HEAD) == \
         (lax.broadcasted_iota(jnp.int32, (MXU_DIM, MXU_DIM), 1) // HEAD)
    ka = ka_ref[...]
    last_row = 0 if rev else c - 1

    order = range(n_sub - 1, -1, -1) if rev else range(n_sub)
    for j in order:
        sl = slice(j * c, (j + 1) * c)
        r = r_ref[0, sl, :]
        k = k_ref[0, sl, :]
        v = v_ref[0, sl, :]
        kk = kk_ref[0, sl, :]
        lw = lw_ref[0, sl, :]
        a = a_ref[0, sl, :]

        cum = _mm_exact_lhs(tri, lw)
        e_in = jnp.exp(cum)
        e_ex = jnp.exp(cum - lw)
        e_inv = jnp.exp(-cum)
        be = (kk * a) * e_inv
        kap = (k * (1.0 + (a - 1.0) * ka)) * e_inv
        ar = jnp.concatenate([kk * e_ex, r * e_in], axis=0).astype(BF16)
        bk = jnp.concatenate([be, kap], axis=0).astype(BF16)
        ptot = e_in[last_row:last_row + 1, :]
        ab = _dot(ar, _block_diag_rhs(be, bd), _NT)
        ak = _dot(ar, _block_diag_rhs(kap, bd), _NT)
        lm = jnp.where(before, ab[:c], 0.0)
        bq = jnp.where(upto, ab[c:], 0.0)
        aak = jnp.where(before, ak[:c], 0.0)
        kq = jnp.where(upto, ak[c:], 0.0)
        tinv = _tri_inverse_cat(lm, eye, same16, off32, off64, bd)
        v_bd = _block_diag_rhs(v, bd)
        aak_v = _mmc(aak, v_bd)
        kq_v = _mmc(kq, v_bd)

        s = s_ref[...]
        zs = _dot(ar, s.astype(BF16), _NT)
        u = -_mmc(tinv, _block_diag_rhs(zs[:c] + aak_v, bd))
        y_ref[0, sl, :] = zs[c:] + kq_v + _mmc(bq, _block_diag_rhs(u, bd))
        uv = jnp.concatenate([u, v], axis=0).astype(BF16)
        ds = _dot(uv, bk, _TN)
        s_ref[...] = ptot * (s + jnp.where(bd, ds, 0.0))

    @pl.when(blk == n_blk - 1)
    def _():
        sfin_ref[0, 0] = s_ref[...]


def _scan(r, k, v, kk, lw, a, k_a, s0_bd, rev):
    b, t, d_rwkv = r.shape
    groups = d_rwkv // MXU_DIM
    tb = SEQ_TILE
    n_blk = t // tb
    n_sub = tb // CHUNK

    def tok_map(i, g, j):
        return (i, (n_blk - 1 - j) if rev else j, g)

    tok = pl.BlockSpec((1, tb, MXU_DIM), tok_map)
    st = pl.BlockSpec((1, 1, MXU_DIM, MXU_DIM), lambda i, g, j: (i, g, 0, 0))
    return pl.pallas_call(
        functools.partial(_scan_kernel, rev=rev, n_sub=n_sub, n_blk=n_blk),
        grid=(b, groups, n_blk),
        in_specs=[tok] * 6 + [pl.BlockSpec((1, MXU_DIM), lambda i, g, j: (0, g)), st],
        out_specs=[tok, st],
        out_shape=[jax.ShapeDtypeStruct((b, t, d_rwkv), F32),
                   jax.ShapeDtypeStruct((b, groups, MXU_DIM, MXU_DIM), F32)],
        scratch_shapes=[pltpu.VMEM((MXU_DIM, MXU_DIM), F32)],
        compiler_params=_cparams(("parallel", "parallel", "arbitrary")),
        name="scan_bwd" if rev else "scan_fwd",
    )(r, k, v, kk, lw, a, k_a.reshape(1, d_rwkv), s0_bd)


def _to_block_diag(s):
    b, h, n, _ = s.shape
    g = h // HEADS_PER_GROUP
    s5 = s.reshape(b, g, HEADS_PER_GROUP, n, n)
    eye = jnp.eye(HEADS_PER_GROUP, dtype=s.dtype)
    out = s5[:, :, :, :, None, :] * eye[None, None, :, None, :, None]
    return out.reshape(b, g, HEADS_PER_GROUP * n, HEADS_PER_GROUP * n)


def _from_block_diag(sbd):
    b, g, m, _ = sbd.shape
    n = m // HEADS_PER_GROUP
    s6 = sbd.reshape(b, g, HEADS_PER_GROUP, n, HEADS_PER_GROUP, n)
    idx = jnp.arange(HEADS_PER_GROUP)
    diag = s6[:, :, idx, :, idx, :]
    return jnp.moveaxis(diag, 0, 2).reshape(b, g * HEADS_PER_GROUP, n, n)


def _post_kernel(yf_ref, yb_ref, r_ref, k_ref, v_ref, af_ref, ab_ref, g_ref,
                 rk_ref, ka_ref, lg_ref, lb_ref, ones_ref, o_ref):
    ones_bd = ones_ref[...]
    inv_n = 1.0 / HEAD
    y = yf_ref[0] + yb_ref[0]
    mu = _head_sum(y, ones_bd) * inv_n
    d = y - mu
    var = _head_sum(d * d, ones_bd) * inv_n
    yn = d * lax.rsqrt(var + GN_EPS) * lg_ref[...] + lb_ref[...]
    ka = ka_ref[...]
    k = k_ref[0]
    kd_sum = k * (1.0 + (af_ref[0] - 1.0) * ka) + k * (1.0 + (ab_ref[0] - 1.0) * ka)
    bonus = _head_sum(r_ref[0] * kd_sum * rk_ref[...], ones_bd) * v_ref[0]
    o_ref[0] = ((yn + bonus) * g_ref[0]).astype(BF16)


def _post(yf, yb, r, k, v, af, ab, g, r_k, k_a, lnx_g, lnx_b, ones_bd):
    b, t, d_rwkv = yf.shape
    tp = min(512, t)
    tok = pl.BlockSpec((1, tp, MXU_DIM), lambda i, j, q: (i, j, q))
    par = pl.BlockSpec((1, MXU_DIM), lambda i, j, q: (0, q))
    return pl.pallas_call(
        _post_kernel,
        grid=(b, t // tp, d_rwkv // MXU_DIM),
        in_specs=[tok] * 8 + [par] * 4 + [pl.BlockSpec((MXU_DIM, MXU_DIM), lambda i, j, q: (0, 0))],
        out_specs=tok,
        out_shape=jax.ShapeDtypeStruct((b, t, d_rwkv), BF16),
        compiler_params=_cparams(("parallel", "parallel", "arbitrary")),
        name="rwkv_post",
    )(yf, yb, r, k, v, af, ab, g, r_k.reshape(1, d_rwkv), k_a.reshape(1, d_rwkv),
      lnx_g.reshape(1, d_rwkv), lnx_b.reshape(1, d_rwkv), ones_bd)


def _window_bounds(n, w):
    t = np.arange(n)
    lo = np.clip(t - w // 2, 0, n)
    hi = np.clip(t - w // 2 + w, 0, n)
    return lo, hi


def _window_matrix(n, w):
    lo, hi = _window_bounds(n, w)
    s = np.arange(n)[None, :]
    m = ((s >= lo[:, None]) & (s < hi[:, None])).astype(np.float32)
    return m / (hi - lo)[:, None].astype(np.float32)


def _pool_ctx_kernel(z_ref, band_ref, pw_ref, ps_ref, o_ref):
    z = z_ref[0]
    m = _dot(band_ref[0], z, precision=lax.Precision.HIGHEST)
    d = (m - z).astype(BF16)
    o_ref[0] = (_dot(d, pw_ref[0]) * ps_ref[...]).astype(BF16)


def _pool_ctx(proj3, col0, pool_w_bf16, pool_scale):
    b, t, _ = proj3.shape
    ng, pg, _ = pool_w_bf16.shape
    band = jnp.asarray(np.stack([_window_matrix(t, w) for w in POOL_WINDOWS]))
    cb = col0 // pg
    return pl.pallas_call(
        _pool_ctx_kernel,
        grid=(b, ng),
        in_specs=[pl.BlockSpec((1, t, pg), lambda i, q: (i, 0, cb + q)),
                  pl.BlockSpec((1, t, t), lambda i, q: (q, 0, 0)),
                  pl.BlockSpec((1, pg, pg), lambda i, q: (q, 0, 0)),
                  pl.BlockSpec((1, pg), lambda i, q: (0, q))],
        out_specs=pl.BlockSpec((1, t, pg), lambda i, q: (i, 0, q)),
        out_shape=jax.ShapeDtypeStruct((b, t, ng * pg), BF16),
        compiler_params=_cparams(("parallel", "arbitrary")),
        name="pool_ctx",
    )(proj3, band, pool_w_bf16, pool_scale.reshape(1, ng * pg))


def _pool_lat_kernel(z_ref, band_ref, pw_ref, ps_ref, o_ref, m1_ref, cs_ref, *, w, n_rows):
    tile = band_ref.shape[0]
    n_tok = n_rows * GRID_W
    band = band_ref[...]
    for i in range(n_tok // tile):
        sl = slice(i * tile, (i + 1) * tile)
        m1_ref[sl, :] = _dot(band, z_ref[0, sl, :], precision=lax.Precision.HIGHEST)
    cs_ref[0:GRID_W, :] = jnp.zeros((GRID_W, cs_ref.shape[1]), F32)
    for rr in range(n_rows):
        cs_ref[(rr + 1) * GRID_W:(rr + 2) * GRID_W, :] = (
            cs_ref[rr * GRID_W:(rr + 1) * GRID_W, :] + m1_ref[rr * GRID_W:(rr + 1) * GRID_W, :])
    lo, hi = _window_bounds(n_rows, w)
    for rr in range(n_rows):
        l, h = int(lo[rr]), int(hi[rr])
        m2 = (cs_ref[h * GRID_W:(h + 1) * GRID_W, :] - cs_ref[l * GRID_W:(l + 1) * GRID_W, :]) / float(h - l)
        sl = slice(rr * GRID_W, (rr + 1) * GRID_W)
        m1_ref[sl, :] = m2 - z_ref[0, sl, :]
    pw = pw_ref[...]
    ps = ps_ref[...]
    for i in range(n_tok // tile):
        sl = slice(i * tile, (i + 1) * tile)
        o_ref[0, sl, :] = (_dot(m1_ref[sl, :].astype(BF16), pw) * ps).astype(BF16)


def _pool_lat_group(proj3, col0, gi, pool_w_bf16, pool_scale):
    b, t, _ = proj3.shape
    ng, pg, _ = pool_w_bf16.shape
    w = POOL_WINDOWS[gi]
    n_rows = t // GRID_W
    tile = MXU_DIM
    band = jnp.asarray(np.kron(np.eye(tile // GRID_W, dtype=np.float32), _window_matrix(GRID_W, w)))
    cb = col0 // pg + gi
    return pl.pallas_call(
        functools.partial(_pool_lat_kernel, w=w, n_rows=n_rows),
        grid=(b,),
        in_specs=[pl.BlockSpec((1, t, pg), lambda i: (i, 0, cb)),
                  pl.BlockSpec((tile, tile), lambda i: (0, 0)),
                  pl.BlockSpec((pg, pg), lambda i: (0, 0)),
                  pl.BlockSpec((1, pg), lambda i: (0, 0))],
        out_specs=pl.BlockSpec((1, t, pg), lambda i: (i, 0, 0)),
        out_shape=jax.ShapeDtypeStruct((b, t, pg), BF16),
        scratch_shapes=[pltpu.VMEM((t, pg), F32), pltpu.VMEM((t + GRID_W, pg), F32)],
        compiler_params=_cparams(("parallel",)),
        name=f"pool_lat_w{w}",
    )(proj3, band, pool_w_bf16[gi], pool_scale.reshape(ng, pg)[gi:gi + 1])


def _up_kernel(ya_ref, yb_ref, wa_ref, wb_ref, ga_ref, gb_ref, o_ref):
    ua = _dot(ya_ref[...], wa_ref[...])
    ub = _dot(yb_ref[...], wb_ref[...])
    o_ref[...] = (_sigmoid(ga_ref[...]) * ua + _sigmoid(gb_ref[...]) * ub).astype(BF16)


def _up(ya2, yb2, wa, wb, proj2, gate_col0, tm=512, tn=512):
    m, ka = ya2.shape
    kb = yb2.shape[1]
    n = wa.shape[1]
    ca = gate_col0 // tn
    cb = (gate_col0 + n) // tn
    return pl.pallas_call(
        _up_kernel,
        grid=(m // tm, n // tn),
        in_specs=[pl.BlockSpec((tm, ka), lambda i, j: (i, 0)),
                  pl.BlockSpec((tm, kb), lambda i, j: (i, 0)),
                  pl.BlockSpec((ka, tn), lambda i, j: (0, j)),
                  pl.BlockSpec((kb, tn), lambda i, j: (0, j)),
                  pl.BlockSpec((tm, tn), lambda i, j: (i, ca + j)),
                  pl.BlockSpec((tm, tn), lambda i, j: (i, cb + j))],
        out_specs=pl.BlockSpec((tm, tn), lambda i, j: (i, j)),
        out_shape=jax.ShapeDtypeStruct((m, n), BF16),
        compiler_params=_cparams(("parallel", "arbitrary")),
        name="up_merge",
    )(ya2, yb2, wa, wb, proj2, proj2)


def _out_kernel(mg_ref, w_ref, x_ref, mod_ref, g_ref, x1_ref, h2_ref):
    acc = _dot(mg_ref[...], w_ref[...])
    x1 = x_ref[...] + mod_ref[0, 2:3, :] * acc
    x1_ref[...] = x1
    y = x1 * lax.rsqrt(jnp.mean(x1 * x1, axis=-1, keepdims=True) + NORM_EPS)
    h2 = (y * g_ref[...]) * (1.0 + mod_ref[0, 4:5, :]) + mod_ref[0, 3:4, :]
    h2_ref[...] = h2.astype(BF16)


def _out_proj(merged, w_out, x2, mod3, tiles_per_mod_tokens, norm2_g, tm=256):
    m, d = x2.shape
    tpm = tiles_per_mod_tokens // tm
    return pl.pallas_call(
        _out_kernel,
        grid=(m // tm,),
        in_specs=[pl.BlockSpec((tm, d), lambda i: (i, 0)),
                  pl.BlockSpec((d, d), lambda i: (0, 0)),
                  pl.BlockSpec((tm, d), lambda i: (i, 0)),
                  pl.BlockSpec((1, 6, d), lambda i: (i // tpm, 0, 0)),
                  pl.BlockSpec((1, d), lambda i: (0, 0))],
        out_specs=[pl.BlockSpec((tm, d), lambda i: (i, 0)),
                   pl.BlockSpec((tm, d), lambda i: (i, 0))],
        out_shape=[jax.ShapeDtypeStruct((m, d), F32), jax.ShapeDtypeStruct((m, d), BF16)],
        compiler_params=_cparams(("parallel",)),
        name="out_proj",
    )(merged, w_out, x2, mod3, norm2_g.reshape(1, d))


def _ffn1_kernel(h_ref, w1_ref, w3_ref, o_ref):
    h = h_ref[...]
    u1 = _dot(h, w1_ref[...])
    u3 = _dot(h, w3_ref[...])
    o_ref[...] = (u1 * _sigmoid(u1) * u3).astype(BF16)


def _ffn1(h2, w13, d_ff, tm=512, tn=512):
    m, d = h2.shape
    nj = d_ff // tn
    return pl.pallas_call(
        _ffn1_kernel,
        grid=(m // tm, nj),
        in_specs=[pl.BlockSpec((tm, d), lambda i, j: (i, 0)),
                  pl.BlockSpec((d, tn), lambda i, j: (0, j)),
                  pl.BlockSpec((d, tn), lambda i, j: (0, nj + j))],
        out_specs=pl.BlockSpec((tm, tn), lambda i, j: (i, j)),
        out_shape=jax.ShapeDtypeStruct((m, d_ff), BF16),
        compiler_params=_cparams(("parallel", "arbitrary")),
        name="ffn1",
    )(h2, w13, w13)


def _ffn2_kernel(a_ref, w_ref, x_ref, mod_ref, g_ref, o_ref, acc_ref, *, nk):
    kk = pl.program_id(1)

    @pl.when(kk == 0)
    def _():
        acc_ref[...] = jnp.zeros_like(acc_ref)

    acc_ref[...] += _dot(a_ref[...], w_ref[...])

    @pl.when(kk == nk - 1)
    def _():
        x2 = x_ref[...] + mod_ref[0, 5:6, :] * acc_ref[...]
        y = x2 * lax.rsqrt(jnp.mean(x2 * x2, axis=-1, keepdims=True) + NORM_EPS)
        o_ref[...] = y * g_ref[...]


def _ffn2(act, w2, x1, mod3, tiles_per_mod_tokens, final_g, tm=512, tk=512):
    m, d = x1.shape
    d_ff = act.shape[1]
    nk = d_ff // tk
    tpm = tiles_per_mod_tokens // tm
    return pl.pallas_call(
        functools.partial(_ffn2_kernel, nk=nk),
        grid=(m // tm, nk),
        in_specs=[pl.BlockSpec((tm, tk), lambda i, j: (i, j)),
                  pl.BlockSpec((tk, d), lambda i, j: (j, 0)),
                  pl.BlockSpec((tm, d), lambda i, j: (i, 0)),
                  pl.BlockSpec((1, 6, d), lambda i, j: (i // tpm, 0, 0)),
                  pl.BlockSpec((1, d), lambda i, j: (0, 0))],
        out_specs=pl.BlockSpec((tm, d), lambda i, j: (i, 0)),
        out_shape=jax.ShapeDtypeStruct((m, d), F32),
        scratch_shapes=[pltpu.VMEM((tm, d), F32)],
        compiler_params=_cparams(("parallel", "arbitrary")),
        name="ffn2",
    )(act, w2, x1, mod3, final_g.reshape(1, d))


def _trunk(x, mod, s0, latent, p):
    b, t, d = x.shape
    m = b * t
    d_rwkv = p["d_rwkv"]
    d_pool = d - d_rwkv
    heads = d_rwkv // HEAD
    x2 = x.reshape(m, d)
    nb = mod.shape[0]
    mod3 = mod.reshape(nb, 6, d)
    tokens_per_mod = t if nb == b else m

    proj = _in_proj(x2, p["norm1_g"], mod3, tokens_per_mod, p["w_in"])
    proj3 = proj.reshape(b, t, proj.shape[1])

    r, k, v, kk, lwf, lwb, af, ab, g = _prep(
        proj3, p["mu"], p["w0"], p["w2"], p["a0"], p["a2"], p["g2"], p["k_k"], p["ones_bd"],
        d_rwkv, p["rw_cols_p"])

    if s0 is None:
        s0 = jnp.zeros((b, 2, heads, HEAD, HEAD), F32)
    yf, sf = _scan(r, k, v, kk, lwf, af, p["k_a"], _to_block_diag(s0[:, 0]), rev=False)
    yb, sb = _scan(r, k, v, kk, lwb, ab, p["k_a"], _to_block_diag(s0[:, 1]), rev=True)
    s_fin = jnp.stack([_from_block_diag(sf), _from_block_diag(sb)], axis=1)
    ya = _post(yf, yb, r, k, v, af, ab, g, p["r_k"], p["k_a"], p["lnx_g"], p["lnx_b"], p["ones_bd"])

    pool_col0 = p["rw_cols_p"]
    if latent:
        yb_pool = jnp.concatenate(
            [_pool_lat_group(proj3, pool_col0, gi, p["pool_w"], p["pool_scale"])
             for gi in range(len(POOL_WINDOWS))], axis=-1)
    else:
        yb_pool = _pool_ctx(proj3, pool_col0, p["pool_w"], p["pool_scale"])

    merged = _up(ya.reshape(m, d_rwkv), yb_pool.reshape(m, d_pool), p["w_up_a"], p["w_up_b"],
                 proj, pool_col0 + d_pool)
    x1, h2 = _out_proj(merged, p["w_out"], x2, mod3, tokens_per_mod, p["norm2_g"])
    act = _ffn1(h2, p["ffn_w13"], p["d_ff"])
    y = _ffn2(act, p["ffn_w2"], x1, mod3, tokens_per_mod, p["final_g"])
    return y.reshape(b, t, d), s_fin


def kernel(x_prompt, x_sample, c, state_rwkv, c_ctx, w_mod, b_mod, norm1_g, w_in, shift_mu, w0, w2, a0, a2, g2,
           k_k, k_a, r_k, lnx_g, lnx_b, w_up_a, pool_w, pool_scale, w_up_b, w_out, norm2_g, ffn_w13, ffn_w2,
           final_g):
    depth = w_in.shape[0]
    assert depth == 1, "final norm is fused into the last layer: single-layer trunk only"
    d = x_prompt.shape[-1]
    d_rwkv = w_up_a.shape[1]
    rw_cols = shift_mu.shape[1]
    rw_cols_p = -(-rw_cols // 512) * 512
    assert x_prompt.shape[1] == SEQ_TILE and x_sample.shape[1] % SEQ_TILE == 0

    l = 0
    w_in_l = w_in[l]
    w_in_p = jnp.concatenate(
        [w_in_l[:, :rw_cols], jnp.zeros((d, rw_cols_p - rw_cols), F32), w_in_l[:, rw_cols:]], axis=1).astype(BF16)
    ones_bd = jnp.asarray(np.kron(np.eye(HEADS_PER_GROUP, dtype=np.float32),
                                  np.ones((HEAD, HEAD), np.float32))).astype(BF16)
    p = {
        "d_rwkv": d_rwkv, "rw_cols_p": rw_cols_p, "d_ff": ffn_w2.shape[1],
        "norm1_g": norm1_g[l], "w_in": w_in_p,
        "mu": jnp.pad(shift_mu[l], (0, rw_cols_p - rw_cols)).reshape(1, rw_cols_p),
        "w0": w0[l], "w2": w2[l], "a0": a0[l], "a2": a2[l], "g2": g2[l],
        "k_k": k_k[l], "k_a": k_a[l], "r_k": r_k[l].reshape(-1), "lnx_g": lnx_g[l], "lnx_b": lnx_b[l],
        "ones_bd": ones_bd,
        "w_up_a": w_up_a[l].astype(BF16), "w_up_b": w_up_b[l].astype(BF16),
        "pool_w": pool_w[l].astype(BF16), "pool_scale": pool_scale[l],
        "w_out": w_out[l].astype(BF16), "norm2_g": norm2_g[l],
        "ffn_w13": ffn_w13[l].astype(BF16), "ffn_w2": ffn_w2[l].astype(BF16), "final_g": final_g,
    }
    cond = jnp.concatenate([c_ctx[None, :], c], axis=0)
    mod = _mod(cond, w_mod[l], b_mod[l])
    y_prompt, s_ctx = _trunk(x_prompt, mod[:1], None, False, p)
    y_sample, _ = _trunk(x_sample, mod[1:], state_rwkv[:, l], True, p)
    new_state = s_ctx[:, None].astype(x_prompt.dtype)
    return (y_prompt, y_sample, new_state)
```

```python
import functools

import numpy as np
import jax
import jax.numpy as jnp
from jax import lax
from jax.experimental import pallas as pl
from jax.experimental.pallas import tpu as pltpu

F32 = jnp.float32
BF16 = jnp.bfloat16

HEAD = 64
POOL_WINDOWS = (2, 4, 8, 16)
GRID_W = 64
DECAY_LORA = 64
AAA_LORA = 64
GATE_LORA = 128
NORM_EPS = 1e-6
GN_EPS = 64e-5

LANES = 128
MXU_DIM = 256
HEADS_PER_GROUP = MXU_DIM // HEAD
VMEM_LIMIT = 56 * 1024 * 1024

CHUNK = 64
SEQ_TILE = 256


def _cparams(sem):
    return pltpu.CompilerParams(dimension_semantics=sem, vmem_limit_bytes=VMEM_LIMIT)


def _dot(a, b, dims=(((1,), (0,)), ((), ())), precision=None):
    return lax.dot_general(a, b, dims, precision=precision, preferred_element_type=F32)


_NN = (((1,), (0,)), ((), ()))
_NT = (((1,), (1,)), ((), ()))
_TN = (((0,), (0,)), ((), ()))


def _split2(x):
    hi = x.astype(BF16)
    lo = (x - hi.astype(F32)).astype(BF16)
    return hi, lo


def _split3(x):
    hi = x.astype(BF16)
    r1 = x - hi.astype(F32)
    mid = r1.astype(BF16)
    lo = (r1 - mid.astype(F32)).astype(BF16)
    return hi, mid, lo


def _mm(a, b, dims=_NN, passes=1):
    if passes == 1:
        return _dot(a.astype(BF16), b.astype(BF16), dims)
    if passes == 3:
        ah, al = _split2(a)
        bh, bl = _split2(b)
        return _dot(ah, bh, dims) + (_dot(ah, bl, dims) + _dot(al, bh, dims))
    return _dot(a, b, dims, precision=lax.Precision.HIGHEST)


def _mm_exact_lhs(a_bf16, b, dims=_NN):
    b0, b1, b2 = _split3(b)
    return _dot(a_bf16, b0, dims) + (_dot(a_bf16, b1, dims) + _dot(a_bf16, b2, dims))


def _mm_exact_rhs(a, b_bf16, dims=_NN):
    a0, a1, a2 = _split3(a)
    return _dot(a0, b_bf16, dims) + (_dot(a1, b_bf16, dims) + _dot(a2, b_bf16, dims))


def _sigmoid(x):
    return 1.0 / (1.0 + jnp.exp(-x))


def _mod_kernel(c_ref, w_ref, b_ref, o_ref):
    c = c_ref[...]
    s = (c * _sigmoid(c)).astype(BF16)
    o_ref[...] = _dot(s, w_ref[...].astype(BF16)) + b_ref[...]


def _mod(cond, w_mod, b_mod):
    nb, d = cond.shape
    rows = -(-nb // 16) * 16
    cond_p = jnp.pad(cond, ((0, rows - nb), (0, 0)))
    n = w_mod.shape[1]
    tn = 1024
    out = pl.pallas_call(
        _mod_kernel,
        grid=(n // tn,),
        in_specs=[pl.BlockSpec((rows, d), lambda j: (0, 0)),
                  pl.BlockSpec((d, tn), lambda j: (0, j)),
                  pl.BlockSpec((1, tn), lambda j: (0, j))],
        out_specs=pl.BlockSpec((rows, tn), lambda j: (0, j)),
        out_shape=jax.ShapeDtypeStruct((rows, n), F32),
        compiler_params=_cparams(("arbitrary",)),
        name="mod",
    )(cond_p, w_mod, b_mod.reshape(1, n))
    return out[:nb]


def _in_kernel(x_ref, g_ref, mod_ref, w_ref, o_ref, h_ref):
    @pl.when(pl.program_id(1) == 0)
    def _():
        x = x_ref[...]
        y = x * lax.rsqrt(jnp.mean(x * x, axis=-1, keepdims=True) + NORM_EPS)
        h = (y * g_ref[...]) * (1.0 + mod_ref[0, 1:2, :]) + mod_ref[0, 0:1, :]
        h_ref[...] = h.astype(BF16)

    o_ref[...] = _dot(h_ref[...], w_ref[...])


def _in_proj(x2, norm_g, mod3, tiles_per_mod_tokens, w_bf16, tm=512, tn=512):
    m, d = x2.shape
    n = w_bf16.shape[1]
    tpm = tiles_per_mod_tokens // tm
    return pl.pallas_call(
        _in_kernel,
        grid=(m // tm, n // tn),
        in_specs=[pl.BlockSpec((tm, d), lambda i, j: (i, 0)),
                  pl.BlockSpec((1, d), lambda i, j: (0, 0)),
                  pl.BlockSpec((1, 6, d), lambda i, j: (i // tpm, 0, 0)),
                  pl.BlockSpec((d, tn), lambda i, j: (0, j))],
        out_specs=pl.BlockSpec((tm, tn), lambda i, j: (i, j)),
        out_shape=jax.ShapeDtypeStruct((m, n), F32),
        scratch_shapes=[pltpu.VMEM((tm, d), BF16)],
        compiler_params=_cparams(("parallel", "arbitrary")),
        name="in_proj",
    )(x2, norm_g.reshape(1, d), mod3, w_bf16)


def _head_sum(x, ones_bd):
    return _mm_exact_rhs(x, ones_bd)


def _prep_kernel(main_ref, prev_ref, next_ref, mu_ref, w0_ref, w2_ref, a0_ref, a2_ref, g2_ref,
                 kkw_ref, ones_ref, r_o, k_o, v_o, kk_o, lwf_o, lwb_o, af_o, ab_o, g_o, *, n_tiles, d_rwkv):
    t = pl.program_id(1)
    rows = main_ref.shape[1]
    row = lax.broadcasted_iota(jnp.int32, (rows, 1), 0)
    has_prev = t > 0
    has_next = t < n_tiles - 1

    def shifted(c0, c1):
        z = main_ref[0, :, c0:c1]
        p = jnp.where(has_prev, prev_ref[0, 7:8, c0:c1], 0.0)
        n = jnp.where(has_next, next_ref[0, 0:1, c0:c1], 0.0)
        zp = jnp.where(row == 0, p, pltpu.roll(z, 1, 0))
        zn = jnp.where(row == rows - 1, n, pltpu.roll(z, rows - 1, 0))
        return z + mu_ref[:, c0:c1] * (0.5 * (zp + zn) - z)

    dr = d_rwkv
    r_o[0] = shifted(0, dr)
    k = shifted(dr, 2 * dr)
    k_o[0] = k
    v_o[0] = shifted(2 * dr, 3 * dr)

    kkv = k * kkw_ref[...]
    ones_bd = ones_ref[...]
    for j in range(dr // MXU_DIM):
        sl = slice(j * MXU_DIM, (j + 1) * MXU_DIM)
        x = kkv[:, sl]
        ss = _head_sum(x * x, ones_bd)
        kk_o[0, :, sl] = x / jnp.maximum(jnp.sqrt(ss), 1e-12)

    c = 3 * dr
    sm = shifted(c, c + 2 * DECAY_LORA + 2 * AAA_LORA + GATE_LORA)
    hp = lax.Precision.HIGHEST
    o = 0
    for d, lw_o in enumerate((lwf_o, lwb_o)):
        wd = jnp.tanh(sm[:, o:o + DECAY_LORA])
        o += DECAY_LORA
        xw = w0_ref[d:d + 1, :] + _dot(wd, w2_ref[d], precision=hp)
        u = -xw
        softplus = jnp.maximum(u, 0.0) + jnp.log1p(jnp.exp(-jnp.abs(u)))
        lw_o[0] = -jnp.exp(-softplus - 0.5)
    for d, a_o in enumerate((af_o, ab_o)):
        ad = sm[:, o:o + AAA_LORA]
        o += AAA_LORA
        a_o[0] = _sigmoid(a0_ref[d:d + 1, :] + _dot(ad, a2_ref[d], precision=hp))
    gd = _sigmoid(sm[:, o:o + GATE_LORA])
    g_o[0] = _dot(gd, g2_ref[...], precision=hp)


def _prep(proj3, mu_p, w0, w2, a0, a2, g2, k_k, ones_bd, d_rwkv, rw_cols_p):
    b, t, _ = proj3.shape
    tp = SEQ_TILE
    n_tiles = t // tp
    sub = tp // 8
    last8 = t // 8 - 1
    out_sd = jax.ShapeDtypeStruct((b, t, d_rwkv), F32)
    o_spec = pl.BlockSpec((1, tp, d_rwkv), lambda i, j: (i, j, 0))

    def full(a):
        nd = a.ndim
        return pl.BlockSpec(a.shape, lambda i, j: (0,) * nd)

    params = (mu_p, w0, w2, a0, a2, g2, k_k.reshape(1, d_rwkv), ones_bd)
    return pl.pallas_call(
        functools.partial(_prep_kernel, n_tiles=n_tiles, d_rwkv=d_rwkv),
        grid=(b, n_tiles),
        in_specs=[pl.BlockSpec((1, tp, rw_cols_p), lambda i, j: (i, j, 0)),
                  pl.BlockSpec((1, 8, rw_cols_p), lambda i, j: (i, jnp.maximum(j * sub - 1, 0), 0)),
                  pl.BlockSpec((1, 8, rw_cols_p), lambda i, j: (i, jnp.minimum((j + 1) * sub, last8), 0)),
                  ] + [full(a) for a in params],
        out_specs=[o_spec] * 9,
        out_shape=[out_sd] * 9,
        compiler_params=_cparams(("parallel", "arbitrary")),
        name="rwkv_prep",
    )(proj3, proj3, proj3, *params)


def _block_diag_rhs(y, bd):
    yb = y.astype(BF16)
    tiled = jnp.concatenate([yb] * HEADS_PER_GROUP, axis=0)
    return jnp.where(bd, tiled, jnp.zeros_like(tiled))


def _mmc(x, y_bd):
    return _dot(x.astype(BF16), y_bd)


def _scan_masks(rev):
    c = CHUNK
    ti = lax.broadcasted_iota(jnp.int32, (c, c), 0)
    si = lax.broadcasted_iota(jnp.int32, (c, c), 1)
    tri = jnp.where((si >= ti) if rev else (si <= ti), 1.0, 0.0).astype(BF16)
    tc = lax.broadcasted_iota(jnp.int32, (c, MXU_DIM), 0)
    sc = lax.broadcasted_iota(jnp.int32, (c, MXU_DIM), 1) % c
    before = (sc > tc) if rev else (sc < tc)
    upto = (sc >= tc) if rev else (sc <= tc)
    return tri, before, upto


def _scan_chunk_local(out, refs, sl, dir_masks, blk_masks, ka, last_row, bd):
    r_ref, k_ref, v_ref, kk_ref, lw_ref, a_ref = refs
    tri, before, upto = dir_masks
    eye, same16, off32, off64 = blk_masks
    c = CHUNK
    kk = kk_ref[0, sl, :]
    lw = lw_ref[0, sl, :]
    a = a_ref[0, sl, :]
    lw0, lw1, lw2 = _split3(lw)
    cum = _dot(tri, lw0) + (_dot(tri, lw1) + _dot(tri, lw2))
    yield
    e_in = jnp.exp(cum)
    e_inv = jnp.exp(-cum)
    al = kk * jnp.exp(cum - lw)
    rh = r_ref[0, sl, :] * e_in
    be = (kk * a) * e_inv
    kap = (k_ref[0, sl, :] * (1.0 + (a - 1.0) * ka)) * e_inv
    ptot = e_in[last_row:last_row + 1, :]
    ar = jnp.concatenate([al, rh], axis=0).astype(BF16)
    ab = _dot(ar, _block_diag_rhs(be, bd), _NT)
    ak = _dot(ar, _block_diag_rhs(kap, bd), _NT)
    yield
    lm = jnp.where(before, ab[:c], 0.0)
    aak = jnp.where(before, ak[:c], 0.0)
    kq = jnp.where(upto, ak[c:], 0.0)
    out["bq"] = jnp.where(upto, ab[c:], 0.0).astype(BF16)
    ld = jnp.where(same16, lm, 0.0)
    p = eye - ld
    ld_bd = _block_diag_rhs(ld, bd)
    l2 = _mmc(ld, ld_bd)
    yield
    l2_bd = _block_diag_rhs(l2, bd)
    p2 = _mmc(p, l2_bd)
    l4 = _mmc(l2, l2_bd)
    yield
    p = p + p2
    l4_bd = _block_diag_rhs(l4, bd)
    p4 = _mmc(p, l4_bd)
    l8 = _mmc(l4, l4_bd)
    yield
    p = p + p4
    p8 = _mmc(p, _block_diag_rhs(l8, bd))
    v = v_ref[0, sl, :]
    v_bd = _block_diag_rhs(v, bd)
    aak_v = _mmc(aak, v_bd)
    out["kq_v"] = _mmc(kq, v_bd)
    yield
    p = p + p8
    for off in (off32, off64):
        lo = jnp.where(off, lm, 0.0)
        t1 = _mmc(lo, _block_diag_rhs(p, bd))
        yield
        t2 = _mmc(p, _block_diag_rhs(t1, bd))
        yield
        p = p - t2
    wt = _mmc(p, _block_diag_rhs(al, bd))
    out["ut"] = _mmc(p, _block_diag_rhs(aak_v, bd))
    yield
    out["wr"] = jnp.concatenate([wt, rh], axis=0).astype(BF16)
    out["v"] = v.astype(BF16)
    out["bkT"] = jnp.transpose(jnp.concatenate([be, kap], axis=0)).astype(BF16)
    pc = jnp.transpose(jnp.broadcast_to(ptot, (8, MXU_DIM)))
    out["pcol"] = jnp.broadcast_to(pc[:, 0:1], (MXU_DIM, MXU_DIM))


def _scan_chain(items, m_scr, y_ref, bd):
    c = CHUNK
    for loc, sl in items:
        m = m_scr[...]
        zs = _dot(loc["wr"], m.astype(BF16))
        yield
        u = -(zs[:c] + loc["ut"])
        uv = jnp.concatenate([u.astype(BF16), loc["v"]], axis=0)
        dm = _dot(loc["bkT"], uv)
        yu = _mmc(loc["bq"], _block_diag_rhs(u, bd))
        yield
        y_ref[0, sl, :] = zs[c:] + loc["kq_v"] + yu
        m_scr[...] = loc["pcol"] * (m + jnp.where(bd, dm, 0.0))


def _run_lockstep(gens):
    gens = list(gens)
    while gens:
        alive = []
        for g in gens:
            try:
                next(g)
                alive.append(g)
            except StopIteration:
                pass
        gens = alive


def _scan_kernel(rf_ref, kf_ref, vf_ref, kkf_ref, lwf_ref, af_ref, rb_ref, kb_ref, vb_ref, kkb_ref, lwb_ref,
                 ab_ref, ka_ref, m0f_ref, m0b_ref, yf_ref, yb_ref, mff_ref, mfb_ref, mf_scr, mb_scr,
                 *, n_sub, n_blk):
    c = CHUNK
    blk = pl.program_id(2)

    @pl.when(blk == 0)
    def _():
        mf_scr[...] = m0f_ref[0, 0]
        mb_scr[...] = m0b_ref[0, 0]

    tc = lax.broadcasted_iota(jnp.int32, (c, MXU_DIM), 0)
    sc = lax.broadcasted_iota(jnp.int32, (c, MXU_DIM), 1) % c
    same16 = (sc // 16) == (tc // 16)
    same32 = (sc // 32) == (tc // 32)
    blk_masks = (jnp.where(sc == tc, 1.0, 0.0), same16,
                 jnp.logical_and(same32, jnp.logical_not(same16)), jnp.logical_not(same32))
    bd = (lax.broadcasted_iota(jnp.int32, (MXU_DIM, MXU_DIM), 0) // HEAD) == \
         (lax.broadcasted_iota(jnp.int32, (MXU_DIM, MXU_DIM), 1) // HEAD)
    ka = ka_ref[...]

    refs_f = (rf_ref, kf_ref, vf_ref, kkf_ref, lwf_ref, af_ref)
    refs_b = (rb_ref, kb_ref, vb_ref, kkb_ref, lwb_ref, ab_ref)
    masks_f = _scan_masks(False)
    masks_b = _scan_masks(True)
    slices = [slice(j * c, (j + 1) * c) for j in range(n_sub)]
    items_f = [({}, sl) for sl in slices]
    items_b = [({}, sl) for sl in reversed(slices)]
    gens = []
    for (of, sf), (ob, sb) in zip(items_f, items_b):
        gens.append(_scan_chunk_local(of, refs_f, sf, masks_f, blk_masks, ka, c - 1, bd))
        gens.append(_scan_chunk_local(ob, refs_b, sb, masks_b, blk_masks, ka, 0, bd))
    _run_lockstep(gens)
    _run_lockstep([_scan_chain(items_f, mf_scr, yf_ref, bd), _scan_chain(items_b, mb_scr, yb_ref, bd)])

    @pl.when(blk == n_blk - 1)
    def _():
        mff_ref[0, 0] = mf_scr[...]
        mfb_ref[0, 0] = mb_scr[...]


def _scan(r, k, v, kk, lwf, af, lwb, ab, k_a, m0f, m0b):
    b, t, d_rwkv = r.shape
    groups = d_rwkv // MXU_DIM
    tb = SEQ_TILE
    n_blk = t // tb
    n_sub = tb // CHUNK
    tok_f = pl.BlockSpec((1, tb, MXU_DIM), lambda i, g, j: (i, j, g))
    tok_b = pl.BlockSpec((1, tb, MXU_DIM), lambda i, g, j: (i, n_blk - 1 - j, g))
    st = pl.BlockSpec((1, 1, MXU_DIM, MXU_DIM), lambda i, g, j: (i, g, 0, 0))
    y_sd = jax.ShapeDtypeStruct((b, t, d_rwkv), F32)
    m_sd = jax.ShapeDtypeStruct((b, groups, MXU_DIM, MXU_DIM), F32)
    return pl.pallas_call(
        functools.partial(_scan_kernel, n_sub=n_sub, n_blk=n_blk),
        grid=(b, groups, n_blk),
        in_specs=[tok_f] * 6 + [tok_b] * 6 + [pl.BlockSpec((1, MXU_DIM), lambda i, g, j: (0, g)), st, st],
        out_specs=[tok_f, tok_b, st, st],
        out_shape=[y_sd, y_sd, m_sd, m_sd],
        scratch_shapes=[pltpu.VMEM((MXU_DIM, MXU_DIM), F32), pltpu.VMEM((MXU_DIM, MXU_DIM), F32)],
        compiler_params=_cparams(("parallel", "parallel", "arbitrary")),
        name="scan",
    )(r, k, v, kk, lwf, af, r, k, v, kk, lwb, ab, k_a.reshape(1, d_rwkv), m0f, m0b)


def _to_block_diag(s):
    b, h, n, _ = s.shape
    g = h // HEADS_PER_GROUP
    s5 = s.reshape(b, g, HEADS_PER_GROUP, n, n)
    eye = jnp.eye(HEADS_PER_GROUP, dtype=s.dtype)
    out = s5[:, :, :, :, None, :] * eye[None, None, :, None, :, None]
    return out.reshape(b, g, HEADS_PER_GROUP * n, HEADS_PER_GROUP * n)


def _from_block_diag(sbd):
    b, g, m, _ = sbd.shape
    n = m // HEADS_PER_GROUP
    s6 = sbd.reshape(b, g, HEADS_PER_GROUP, n, HEADS_PER_GROUP, n)
    idx = jnp.arange(HEADS_PER_GROUP)
    diag = s6[:, :, idx, :, idx, :]
    return jnp.moveaxis(diag, 0, 2).reshape(b, g * HEADS_PER_GROUP, n, n)


def _post_kernel(yf_ref, yb_ref, r_ref, k_ref, v_ref, af_ref, ab_ref, g_ref,
                 rk_ref, ka_ref, lg_ref, lb_ref, ones_ref, o_ref):
    ones_bd = ones_ref[...]
    inv_n = 1.0 / HEAD
    y = yf_ref[0] + yb_ref[0]
    mu = _head_sum(y, ones_bd) * inv_n
    d = y - mu
    var = _head_sum(d * d, ones_bd) * inv_n
    yn = d * lax.rsqrt(var + GN_EPS) * lg_ref[...] + lb_ref[...]
    ka = ka_ref[...]
    k = k_ref[0]
    kd_sum = k * (1.0 + (af_ref[0] - 1.0) * ka) + k * (1.0 + (ab_ref[0] - 1.0) * ka)
    bonus = _head_sum(r_ref[0] * kd_sum * rk_ref[...], ones_bd) * v_ref[0]
    o_ref[0] = ((yn + bonus) * g_ref[0]).astype(BF16)


def _post(yf, yb, r, k, v, af, ab, g, r_k, k_a, lnx_g, lnx_b, ones_bd):
    b, t, d_rwkv = yf.shape
    tp = min(512, t)
    tok = pl.BlockSpec((1, tp, MXU_DIM), lambda i, j, q: (i, j, q))
    par = pl.BlockSpec((1, MXU_DIM), lambda i, j, q: (0, q))
    return pl.pallas_call(
        _post_kernel,
        grid=(b, t // tp, d_rwkv // MXU_DIM),
        in_specs=[tok] * 8 + [par] * 4 + [pl.BlockSpec((MXU_DIM, MXU_DIM), lambda i, j, q: (0, 0))],
        out_specs=tok,
        out_shape=jax.ShapeDtypeStruct((b, t, d_rwkv), BF16),
        compiler_params=_cparams(("parallel", "parallel", "arbitrary")),
        name="rwkv_post",
    )(yf, yb, r, k, v, af, ab, g, r_k.reshape(1, d_rwkv), k_a.reshape(1, d_rwkv),
      lnx_g.reshape(1, d_rwkv), lnx_b.reshape(1, d_rwkv), ones_bd)


def _window_bounds(n, w):
    t = np.arange(n)
    lo = np.clip(t - w // 2, 0, n)
    hi = np.clip(t - w // 2 + w, 0, n)
    return lo, hi


def _window_matrix(n, w):
    lo, hi = _window_bounds(n, w)
    s = np.arange(n)[None, :]
    m = ((s >= lo[:, None]) & (s < hi[:, None])).astype(np.float32)
    return m / (hi - lo)[:, None].astype(np.float32)


def _pool_ctx_kernel(z_ref, band_ref, pw_ref, ps_ref, o_ref):
    z = z_ref[0]
    m = _dot(band_ref[0], z, precision=lax.Precision.HIGHEST)
    d = (m - z).astype(BF16)
    o_ref[0] = (_dot(d, pw_ref[0]) * ps_ref[...]).astype(BF16)


def _pool_ctx(proj3, col0, pool_w_bf16, pool_scale):
    b, t, _ = proj3.shape
    ng, pg, _ = pool_w_bf16.shape
    band = jnp.asarray(np.stack([_window_matrix(t, w) for w in POOL_WINDOWS]))
    cb = col0 // pg
    return pl.pallas_call(
        _pool_ctx_kernel,
        grid=(b, ng),
        in_specs=[pl.BlockSpec((1, t, pg), lambda i, q: (i, 0, cb + q)),
                  pl.BlockSpec((1, t, t), lambda i, q: (q, 0, 0)),
                  pl.BlockSpec((1, pg, pg), lambda i, q: (q, 0, 0)),
                  pl.BlockSpec((1, pg), lambda i, q: (0, q))],
        out_specs=pl.BlockSpec((1, t, pg), lambda i, q: (i, 0, q)),
        out_shape=jax.ShapeDtypeStruct((b, t, ng * pg), BF16),
        compiler_params=_cparams(("parallel", "arbitrary")),
        name="pool_ctx",
    )(proj3, band, pool_w_bf16, pool_scale.reshape(1, ng * pg))


def _pool_lat_kernel(z_ref, band_ref, pw_ref, ps_ref, o_ref, m1_ref, cs_ref, *, w, n_rows):
    tile = band_ref.shape[0]
    n_tok = n_rows * GRID_W
    band = band_ref[...]
    for i in range(n_tok // tile):
        sl = slice(i * tile, (i + 1) * tile)
        m1_ref[sl, :] = _dot(band, z_ref[0, sl, :], precision=lax.Precision.HIGHEST)
    cs_ref[0:GRID_W, :] = jnp.zeros((GRID_W, cs_ref.shape[1]), F32)
    for rr in range(n_rows):
        cs_ref[(rr + 1) * GRID_W:(rr + 2) * GRID_W, :] = (
            cs_ref[rr * GRID_W:(rr + 1) * GRID_W, :] + m1_ref[rr * GRID_W:(rr + 1) * GRID_W, :])
    lo, hi = _window_bounds(n_rows, w)
    for rr in range(n_rows):
        l, h = int(lo[rr]), int(hi[rr])
        m2 = (cs_ref[h * GRID_W:(h + 1) * GRID_W, :] - cs_ref[l * GRID_W:(l + 1) * GRID_W, :]) / float(h - l)
        sl = slice(rr * GRID_W, (rr + 1) * GRID_W)
        m1_ref[sl, :] = m2 - z_ref[0, sl, :]
    pw = pw_ref[...]
    ps = ps_ref[...]
    for i in range(n_tok // tile):
        sl = slice(i * tile, (i + 1) * tile)
        o_ref[0, sl, :] = (_dot(m1_ref[sl, :].astype(BF16), pw) * ps).astype(BF16)


def _pool_lat_group(proj3, col0, gi, pool_w_bf16, pool_scale):
    b, t, _ = proj3.shape
    ng, pg, _ = pool_w_bf16.shape
    w = POOL_WINDOWS[gi]
    n_rows = t // GRID_W
    tile = MXU_DIM
    band = jnp.asarray(np.kron(np.eye(tile // GRID_W, dtype=np.float32), _window_matrix(GRID_W, w)))
    cb = col0 // pg + gi
    return pl.pallas_call(
        functools.partial(_pool_lat_kernel, w=w, n_rows=n_rows),
        grid=(b,),
        in_specs=[pl.BlockSpec((1, t, pg), lambda i: (i, 0, cb)),
                  pl.BlockSpec((tile, tile), lambda i: (0, 0)),
                  pl.BlockSpec((pg, pg), lambda i: (0, 0)),
                  pl.BlockSpec((1, pg), lambda i: (0, 0))],
        out_specs=pl.BlockSpec((1, t, pg), lambda i: (i, 0, 0)),
        out_shape=jax.ShapeDtypeStruct((b, t, pg), BF16),
        scratch_shapes=[pltpu.VMEM((t, pg), F32), pltpu.VMEM((t + GRID_W, pg), F32)],
        compiler_params=_cparams(("parallel",)),
        name=f"pool_lat_w{w}",
    )(proj3, band, pool_w_bf16[gi], pool_scale.reshape(ng, pg)[gi:gi + 1])


def _up_kernel(ya_ref, yb_ref, wa_ref, wb_ref, ga_ref, gb_ref, o_ref):
    ua = _dot(ya_ref[...], wa_ref[...])
    ub = _dot(yb_ref[...], wb_ref[...])
    o_ref[...] = (_sigmoid(ga_ref[...]) * ua + _sigmoid(gb_ref[...]) * ub).astype(BF16)


def _up(ya2, yb2, wa, wb, proj2, gate_col0, tm=512, tn=512):
    m, ka = ya2.shape
    kb = yb2.shape[1]
    n = wa.shape[1]
    ca = gate_col0 // tn
    cb = (gate_col0 + n) // tn
    return pl.pallas_call(
        _up_kernel,
        grid=(m // tm, n // tn),
        in_specs=[pl.BlockSpec((tm, ka), lambda i, j: (i, 0)),
                  pl.BlockSpec((tm, kb), lambda i, j: (i, 0)),
                  pl.BlockSpec((ka, tn), lambda i, j: (0, j)),
                  pl.BlockSpec((kb, tn), lambda i, j: (0, j)),
                  pl.BlockSpec((tm, tn), lambda i, j: (i, ca + j)),
                  pl.BlockSpec((tm, tn), lambda i, j: (i, cb + j))],
        out_specs=pl.BlockSpec((tm, tn), lambda i, j: (i, j)),
        out_shape=jax.ShapeDtypeStruct((m, n), BF16),
        compiler_params=_cparams(("parallel", "arbitrary")),
        name="up_merge",
    )(ya2, yb2, wa, wb, proj2, proj2)


def _out_kernel(mg_ref, w_ref, x_ref, mod_ref, g_ref, x1_ref, h2_ref):
    acc = _dot(mg_ref[...], w_ref[...])
    x1 = x_ref[...] + mod_ref[0, 2:3, :] * acc
    x1_ref[...] = x1
    y = x1 * lax.rsqrt(jnp.mean(x1 * x1, axis=-1, keepdims=True) + NORM_EPS)
    h2 = (y * g_ref[...]) * (1.0 + mod_ref[0, 4:5, :]) + mod_ref[0, 3:4, :]
    h2_ref[...] = h2.astype(BF16)


def _out_proj(merged, w_out, x2, mod3, tiles_per_mod_tokens, norm2_g, tm=256):
    m, d = x2.shape
    tpm = tiles_per_mod_tokens // tm
    return pl.pallas_call(
        _out_kernel,
        grid=(m // tm,),
        in_specs=[pl.BlockSpec((tm, d), lambda i: (i, 0)),
                  pl.BlockSpec((d, d), lambda i: (0, 0)),
                  pl.BlockSpec((tm, d), lambda i: (i, 0)),
                  pl.BlockSpec((1, 6, d), lambda i: (i // tpm, 0, 0)),
                  pl.BlockSpec((1, d), lambda i: (0, 0))],
        out_specs=[pl.BlockSpec((tm, d), lambda i: (i, 0)),
                   pl.BlockSpec((tm, d), lambda i: (i, 0))],
        out_shape=[jax.ShapeDtypeStruct((m, d), F32), jax.ShapeDtypeStruct((m, d), BF16)],
        compiler_params=_cparams(("parallel",)),
        name="out_proj",
    )(merged, w_out, x2, mod3, norm2_g.reshape(1, d))


def _ffn1_kernel(h_ref, w1_ref, w3_ref, o_ref):
    h = h_ref[...]
    u1 = _dot(h, w1_ref[...])
    u3 = _dot(h, w3_ref[...])
    o_ref[...] = (u1 * _sigmoid(u1) * u3).astype(BF16)


def _ffn1(h2, w13, d_ff, tm=512, tn=512):
    m, d = h2.shape
    nj = d_ff // tn
    return pl.pallas_call(
        _ffn1_kernel,
        grid=(m // tm, nj),
        in_specs=[pl.BlockSpec((tm, d), lambda i, j: (i, 0)),
                  pl.BlockSpec((d, tn), lambda i, j: (0, j)),
                  pl.BlockSpec((d, tn), lambda i, j: (0, nj + j))],
        out_specs=pl.BlockSpec((tm, tn), lambda i, j: (i, j)),
        out_shape=jax.ShapeDtypeStruct((m, d_ff), BF16),
        compiler_params=_cparams(("parallel", "arbitrary")),
        name="ffn1",
    )(h2, w13, w13)


def _ffn2_kernel(a_ref, w_ref, x_ref, mod_ref, g_ref, o_ref, acc_ref, *, nk):
    kk = pl.program_id(1)

    @pl.when(kk == 0)
    def _():
        acc_ref[...] = jnp.zeros_like(acc_ref)

    acc_ref[...] += _dot(a_ref[...], w_ref[...])

    @pl.when(kk == nk - 1)
    def _():
        x2 = x_ref[...] + mod_ref[0, 5:6, :] * acc_ref[...]
        y = x2 * lax.rsqrt(jnp.mean(x2 * x2, axis=-1, keepdims=True) + NORM_EPS)
        o_ref[...] = y * g_ref[...]


def _ffn2(act, w2, x1, mod3, tiles_per_mod_tokens, final_g, tm=512, tk=512):
    m, d = x1.shape
    d_ff = act.shape[1]
    nk = d_ff // tk
    tpm = tiles_per_mod_tokens // tm
    return pl.pallas_call(
        functools.partial(_ffn2_kernel, nk=nk),
        grid=(m // tm, nk),
        in_specs=[pl.BlockSpec((tm, tk), lambda i, j: (i, j)),
                  pl.BlockSpec((tk, d), lambda i, j: (j, 0)),
                  pl.BlockSpec((tm, d), lambda i, j: (i, 0)),
                  pl.BlockSpec((1, 6, d), lambda i, j: (i // tpm, 0, 0)),
                  pl.BlockSpec((1, d), lambda i, j: (0, 0))],
        out_specs=pl.BlockSpec((tm, d), lambda i, j: (i, 0)),
        out_shape=jax.ShapeDtypeStruct((m, d), F32),
        scratch_shapes=[pltpu.VMEM((tm, d), F32)],
        compiler_params=_cparams(("parallel", "arbitrary")),
        name="ffn2",
    )(act, w2, x1, mod3, final_g.reshape(1, d))


def _trunk(x, mod, s0, latent, p):
    b, t, d = x.shape
    m = b * t
    d_rwkv = p["d_rwkv"]
    d_pool = d - d_rwkv
    heads = d_rwkv // HEAD
    x2 = x.reshape(m, d)
    nb = mod.shape[0]
    mod3 = mod.reshape(nb, 6, d)
    tokens_per_mod = t if nb == b else m

    proj = _in_proj(x2, p["norm1_g"], mod3, tokens_per_mod, p["w_in"])
    proj3 = proj.reshape(b, t, proj.shape[1])

    r, k, v, kk, lwf, lwb, af, ab, g = _prep(
        proj3, p["mu"], p["w0"], p["w2"], p["a0"], p["a2"], p["g2"], p["k_k"], p["ones_bd"],
        d_rwkv, p["rw_cols_p"])

    if s0 is None:
        s0 = jnp.zeros((b, 2, heads, HEAD, HEAD), F32)
    m0 = jnp.swapaxes(s0, -1, -2)
    yf, yb, mf, mb = _scan(r, k, v, kk, lwf, af, lwb, ab, p["k_a"],
                           _to_block_diag(m0[:, 0]), _to_block_diag(m0[:, 1]))
    s_fin = jnp.swapaxes(jnp.stack([_from_block_diag(mf), _from_block_diag(mb)], axis=1), -1, -2)
    ya = _post(yf, yb, r, k, v, af, ab, g, p["r_k"], p["k_a"], p["lnx_g"], p["lnx_b"], p["ones_bd"])

    pool_col0 = p["rw_cols_p"]
    if latent:
        yb_pool = jnp.concatenate(
            [_pool_lat_group(proj3, pool_col0, gi, p["pool_w"], p["pool_scale"])
             for gi in range(len(POOL_WINDOWS))], axis=-1)
    else:
        yb_pool = _pool_ctx(proj3, pool_col0, p["pool_w"], p["pool_scale"])

    merged = _up(ya.reshape(m, d_rwkv), yb_pool.reshape(m, d_pool), p["w_up_a"], p["w_up_b"],
                 proj, pool_col0 + d_pool)
    x1, h2 = _out_proj(merged, p["w_out"], x2, mod3, tokens_per_mod, p["norm2_g"])
    act = _ffn1(h2, p["ffn_w13"], p["d_ff"])
    y = _ffn2(act, p["ffn_w2"], x1, mod3, tokens_per_mod, p["final_g"])
    return y.reshape(b, t, d), s_fin


def kernel(x_prompt, x_sample, c, state_rwkv, c_ctx, w_mod, b_mod, norm1_g, w_in, shift_mu, w0, w2, a0, a2, g2,
           k_k, k_a, r_k, lnx_g, lnx_b, w_up_a, pool_w, pool_scale, w_up_b, w_out, norm2_g, ffn_w13, ffn_w2,
           final_g):
    depth = w_in.shape[0]
    assert depth == 1, "final norm is fused into the last layer: single-layer trunk only"
    d = x_prompt.shape[-1]
    d_rwkv = w_up_a.shape[1]
    rw_cols = shift_mu.shape[1]
    rw_cols_p = -(-rw_cols // 512) * 512
    assert x_prompt.shape[1] == SEQ_TILE and x_sample.shape[1] % SEQ_TILE == 0

    l = 0
    w_in_l = w_in[l]
    w_in_p = jnp.concatenate(
        [w_in_l[:, :rw_cols], jnp.zeros((d, rw_cols_p - rw_cols), F32), w_in_l[:, rw_cols:]], axis=1).astype(BF16)
    ones_bd = jnp.asarray(np.kron(np.eye(HEADS_PER_GROUP, dtype=np.float32),
                                  np.ones((HEAD, HEAD), np.float32))).astype(BF16)
    p = {
        "d_rwkv": d_rwkv, "rw_cols_p": rw_cols_p, "d_ff": ffn_w2.shape[1],
        "norm1_g": norm1_g[l], "w_in": w_in_p,
        "mu": jnp.pad(shift_mu[l], (0, rw_cols_p - rw_cols)).reshape(1, rw_cols_p),
        "w0": w0[l], "w2": w2[l], "a0": a0[l], "a2": a2[l], "g2": g2[l],
        "k_k": k_k[l], "k_a": k_a[l], "r_k": r_k[l].reshape(-1), "lnx_g": lnx_g[l], "lnx_b": lnx_b[l],
        "ones_bd": ones_bd,
        "w_up_a": w_up_a[l].astype(BF16), "w_up_b": w_up_b[l].astype(BF16),
        "pool_w": pool_w[l].astype(BF16), "pool_scale": pool_scale[l],
        "w_out": w_out[l].astype(BF16), "norm2_g": norm2_g[l],
        "ffn_w13": ffn_w13[l].astype(BF16), "ffn_w2": ffn_w2[l].astype(BF16), "final_g": final_g,
    }
    cond = jnp.concatenate([c_ctx[None, :], c], axis=0)
    mod = _mod(cond, w_mod[l], b_mod[l])
    y_prompt, s_ctx = _trunk(x_prompt, mod[:1], None, False, p)
    y_sample, _ = _trunk(x_sample, mod[1:], state_rwkv[:, l], True, p)
    new_state = s_ctx[:, None].astype(x_prompt.dtype)
    return (y_prompt, y_sample, new_state)
```

```python
import functools

import numpy as np
import jax
import jax.numpy as jnp
from jax import lax
from jax.experimental import pallas as pl
from jax.experimental.pallas import tpu as pltpu

F32 = jnp.float32
BF16 = jnp.bfloat16

HEAD = 64
POOL_WINDOWS = (2, 4, 8, 16)
GRID_W = 64
DECAY_LORA = 64
AAA_LORA = 64
GATE_LORA = 128
NORM_EPS = 1e-6
GN_EPS = 64e-5

LANES = 128
MXU_DIM = 256
HEADS_PER_GROUP = MXU_DIM // HEAD
VMEM_LIMIT = 56 * 1024 * 1024

CHUNK = 64
SEQ_TILE = 256


def _cparams(sem):
    return pltpu.CompilerParams(dimension_semantics=sem, vmem_limit_bytes=VMEM_LIMIT)


def _dot(a, b, dims=(((1,), (0,)), ((), ())), precision=None):
    return lax.dot_general(a, b, dims, precision=precision, preferred_element_type=F32)


_NN = (((1,), (0,)), ((), ()))
_NT = (((1,), (1,)), ((), ()))
_TN = (((0,), (0,)), ((), ()))


def _split2(x):
    hi = x.astype(BF16)
    lo = (x - hi.astype(F32)).astype(BF16)
    return hi, lo


def _split3(x):
    hi = x.astype(BF16)
    r1 = x - hi.astype(F32)
    mid = r1.astype(BF16)
    lo = (r1 - mid.astype(F32)).astype(BF16)
    return hi, mid, lo


def _mm(a, b, dims=_NN, passes=1):
    if passes == 1:
        return _dot(a.astype(BF16), b.astype(BF16), dims)
    if passes == 3:
        ah, al = _split2(a)
        bh, bl = _split2(b)
        return _dot(ah, bh, dims) + (_dot(ah, bl, dims) + _dot(al, bh, dims))
    return _dot(a, b, dims, precision=lax.Precision.HIGHEST)


def _mm_exact_rhs(a, b_bf16):
    a0, a1 = _split2(a)
    return _dot(a0, b_bf16) + _dot(a1, b_bf16)


def _sigmoid(x):
    return 1.0 / (1.0 + jnp.exp(-x))


def _mod_kernel(c_ref, w_ref, b_ref, o_ref):
    c = c_ref[...]
    s = (c * _sigmoid(c)).astype(BF16)
    o_ref[...] = _dot(s, w_ref[...].astype(BF16)) + b_ref[...]


def _mod(cond, w_mod, b_mod):
    nb, d = cond.shape
    rows = -(-nb // 16) * 16
    cond_p = jnp.pad(cond, ((0, rows - nb), (0, 0)))
    n = w_mod.shape[1]
    tn = 1024
    out = pl.pallas_call(
        _mod_kernel,
        grid=(n // tn,),
        in_specs=[pl.BlockSpec((rows, d), lambda j: (0, 0)),
                  pl.BlockSpec((d, tn), lambda j: (0, j)),
                  pl.BlockSpec((1, tn), lambda j: (0, j))],
        out_specs=pl.BlockSpec((rows, tn), lambda j: (0, j)),
        out_shape=jax.ShapeDtypeStruct((rows, n), F32),
        compiler_params=_cparams(("arbitrary",)),
        name="mod",
    )(cond_p, w_mod, b_mod.reshape(1, n))
    return out[:nb]


def _in_kernel(x_ref, g_ref, mod_ref, w_ref, o_ref, h_ref):
    @pl.when(pl.program_id(1) == 0)
    def _():
        x = x_ref[...]
        y = x * lax.rsqrt(jnp.mean(x * x, axis=-1, keepdims=True) + NORM_EPS)
        h = (y * g_ref[...]) * (1.0 + mod_ref[0, 1:2, :]) + mod_ref[0, 0:1, :]
        h_ref[...] = h.astype(BF16)

    o_ref[...] = _dot(h_ref[...], w_ref[...])


def _in_proj(x2, norm_g, mod3, tiles_per_mod_tokens, w_bf16, tm=1024, tn=512):
    m, d = x2.shape
    tm = min(tm, m)
    n = w_bf16.shape[1]
    tpm = tiles_per_mod_tokens // tm
    return pl.pallas_call(
        _in_kernel,
        grid=(m // tm, n // tn),
        in_specs=[pl.BlockSpec((tm, d), lambda i, j: (i, 0)),
                  pl.BlockSpec((1, d), lambda i, j: (0, 0)),
                  pl.BlockSpec((1, 6, d), lambda i, j: (i // tpm, 0, 0)),
                  pl.BlockSpec((d, tn), lambda i, j: (0, j))],
        out_specs=pl.BlockSpec((tm, tn), lambda i, j: (i, j)),
        out_shape=jax.ShapeDtypeStruct((m, n), F32),
        scratch_shapes=[pltpu.VMEM((tm, d), BF16)],
        compiler_params=_cparams(("parallel", "arbitrary")),
        name="in_proj",
    )(x2, norm_g.reshape(1, d), mod3, w_bf16)


P_LORA_DECAY = 3
P_LORA = 1


def _head_sum(x, ones_bd):
    return _mm_exact_rhs(x, ones_bd)


def _prep_kernel(main_ref, prev_ref, next_ref, mu_ref, w0_ref, w2_ref, a0_ref, a2_ref, g2_ref,
                 kkw_ref, ones_ref, r_o, k_o, v_o, kk_o, lwf_o, lwb_o, af_o, ab_o, g_o, *, n_tiles, d_rwkv):
    t = pl.program_id(1)
    rows = main_ref.shape[1]
    row = lax.broadcasted_iota(jnp.int32, (rows, 1), 0)
    has_prev = t > 0
    has_next = t < n_tiles - 1

    def shifted(c0, c1):
        z = main_ref[0, :, c0:c1]
        p = jnp.where(has_prev, prev_ref[0, 7:8, c0:c1], 0.0)
        n = jnp.where(has_next, next_ref[0, 0:1, c0:c1], 0.0)
        zp = jnp.where(row == 0, p, pltpu.roll(z, 1, 0))
        zn = jnp.where(row == rows - 1, n, pltpu.roll(z, rows - 1, 0))
        return z + mu_ref[:, c0:c1] * (0.5 * (zp + zn) - z)

    dr = d_rwkv
    r_o[0] = shifted(0, dr)
    k = shifted(dr, 2 * dr)
    k_o[0] = k
    v_o[0] = shifted(2 * dr, 3 * dr)

    kkv = k * kkw_ref[...]
    ones_bd = ones_ref[...]
    for j in range(dr // MXU_DIM):
        sl = slice(j * MXU_DIM, (j + 1) * MXU_DIM)
        x = kkv[:, sl]
        ss = _head_sum(x * x, ones_bd)
        kk_o[0, :, sl] = x / jnp.maximum(jnp.sqrt(ss), 1e-12)

    c = 3 * dr
    sm = shifted(c, c + 2 * DECAY_LORA + 2 * AAA_LORA + GATE_LORA)
    o = 0
    for d, lw_o in enumerate((lwf_o, lwb_o)):
        wd = jnp.tanh(sm[:, o:o + DECAY_LORA])
        o += DECAY_LORA
        xw = w0_ref[d:d + 1, :] + _mm(wd, w2_ref[d], passes=P_LORA_DECAY)
        u = -xw
        softplus = jnp.maximum(u, 0.0) + jnp.log1p(jnp.exp(-jnp.abs(u)))
        lw_o[0] = -jnp.exp(-softplus - 0.5)
    for d, a_o in enumerate((af_o, ab_o)):
        ad = sm[:, o:o + AAA_LORA]
        o += AAA_LORA
        a_o[0] = _sigmoid(a0_ref[d:d + 1, :] + _mm(ad, a2_ref[d], passes=P_LORA))
    gd = _sigmoid(sm[:, o:o + GATE_LORA])
    g_o[0] = _mm(gd, g2_ref[...], passes=P_LORA)


def _prep(proj3, mu_p, w0, w2, a0, a2, g2, k_k, ones_bd, d_rwkv, rw_cols_p):
    b, t, _ = proj3.shape
    tp = SEQ_TILE
    n_tiles = t // tp
    sub = tp // 8
    last8 = t // 8 - 1
    out_sd = jax.ShapeDtypeStruct((b, t, d_rwkv), F32)
    o_spec = pl.BlockSpec((1, tp, d_rwkv), lambda i, j: (i, j, 0))

    def full(a):
        nd = a.ndim
        return pl.BlockSpec(a.shape, lambda i, j: (0,) * nd)

    params = (mu_p, w0, w2, a0, a2, g2, k_k.reshape(1, d_rwkv), ones_bd)
    return pl.pallas_call(
        functools.partial(_prep_kernel, n_tiles=n_tiles, d_rwkv=d_rwkv),
        grid=(b, n_tiles),
        in_specs=[pl.BlockSpec((1, tp, rw_cols_p), lambda i, j: (i, j, 0)),
                  pl.BlockSpec((1, 8, rw_cols_p), lambda i, j: (i, jnp.maximum(j * sub - 1, 0), 0)),
                  pl.BlockSpec((1, 8, rw_cols_p), lambda i, j: (i, jnp.minimum((j + 1) * sub, last8), 0)),
                  ] + [full(a) for a in params],
        out_specs=[o_spec] * 9,
        out_shape=[out_sd] * 9,
        compiler_params=_cparams(("parallel", "arbitrary")),
        name="rwkv_prep",
    )(proj3, proj3, proj3, *params)


def _block_diag_rhs(y, bd):
    yb = y.astype(BF16)
    return jnp.concatenate([yb] * HEADS_PER_GROUP, axis=0) * bd


def _mmc(x, y_bd):
    return _dot(x.astype(BF16), y_bd)


def _scan_masks(rev):
    c = CHUNK
    ti = lax.broadcasted_iota(jnp.int32, (c, c), 0)
    si = lax.broadcasted_iota(jnp.int32, (c, c), 1)
    tri = jnp.where((si >= ti) if rev else (si <= ti), 1.0, 0.0).astype(BF16)
    tc = lax.broadcasted_iota(jnp.int32, (c, MXU_DIM), 0)
    sc = lax.broadcasted_iota(jnp.int32, (c, MXU_DIM), 1) % c
    before = (sc > tc) if rev else (sc < tc)
    upto = (sc >= tc) if rev else (sc <= tc)
    return tri, before, upto


def _scan_chunk_local(out, refs, sl, dir_masks, blk_masks, ka, last_row, bd):
    r_ref, k_ref, v_ref, kk_ref, lw_ref, a_ref = refs
    tri, before, upto = dir_masks
    eye, same16, off32, off64 = blk_masks
    c = CHUNK
    kk = kk_ref[0, sl, :]
    lw = lw_ref[0, sl, :]
    a = a_ref[0, sl, :]
    lw0, lw1, lw2 = _split3(lw)
    cum = _dot(tri, lw0) + (_dot(tri, lw1) + _dot(tri, lw2))
    yield
    e_in = jnp.exp(cum)
    e_inv = jnp.exp(-cum)
    al = kk * jnp.exp(cum - lw)
    rh = r_ref[0, sl, :] * e_in
    be = (kk * a) * e_inv
    kap = (k_ref[0, sl, :] * (1.0 + (a - 1.0) * ka)) * e_inv
    ptot = e_in[last_row:last_row + 1, :]
    ar = jnp.concatenate([al, rh], axis=0).astype(BF16)
    ab = _dot(ar, _block_diag_rhs(be, bd), _NT)
    ak = _dot(ar, _block_diag_rhs(kap, bd), _NT)
    yield
    lm = jnp.where(before, ab[:c], 0.0)
    aak = jnp.where(before, ak[:c], 0.0)
    kq = jnp.where(upto, ak[c:], 0.0)
    out["bq"] = jnp.where(upto, ab[c:], 0.0).astype(BF16)
    ld = jnp.where(same16, lm, 0.0)
    p = eye - ld
    ld_bd = _block_diag_rhs(ld, bd)
    l2 = _mmc(ld, ld_bd)
    yield
    pl2 = _mmc(jnp.concatenate([p, l2], axis=0), _block_diag_rhs(l2, bd))
    yield
    p = p + pl2[:c]
    l4 = pl2[c:]
    pl4 = _mmc(jnp.concatenate([p, l4], axis=0), _block_diag_rhs(l4, bd))
    yield
    p = p + pl4[:c]
    p8 = _mmc(p, _block_diag_rhs(pl4[c:], bd))
    v = v_ref[0, sl, :]
    akv = _mmc(jnp.concatenate([aak, kq], axis=0), _block_diag_rhs(v, bd))
    aak_v = akv[:c]
    out["kq_v"] = akv[c:]
    yield
    p = p + p8
    for off in (off32, off64):
        lo = jnp.where(off, lm, 0.0)
        t1 = _mmc(lo, _block_diag_rhs(p, bd))
        yield
        t2 = _mmc(p, _block_diag_rhs(t1, bd))
        yield
        p = p - t2
    wt = _mmc(p, _block_diag_rhs(al, bd))
    out["ut"] = _mmc(p, _block_diag_rhs(aak_v, bd))
    yield
    out["wr"] = jnp.concatenate([wt, rh], axis=0).astype(BF16)
    out["v"] = v.astype(BF16)
    out["bkT"] = jnp.transpose(jnp.concatenate([be, kap], axis=0)).astype(BF16)
    pc = jnp.transpose(jnp.broadcast_to(ptot, (8, MXU_DIM)))
    out["pcol"] = jnp.broadcast_to(pc[:, 0:1], (MXU_DIM, MXU_DIM))


def _scan_chain(items, m_scr, y_ref, bd, bd_mask):
    c = CHUNK
    for loc, sl in items:
        m = m_scr[...]
        zs = _dot(loc["wr"], m.astype(BF16))
        yield
        u = -(zs[:c] + loc["ut"])
        uv = jnp.concatenate([u.astype(BF16), loc["v"]], axis=0)
        dm = _dot(loc["bkT"], uv)
        yu = _mmc(loc["bq"], _block_diag_rhs(u, bd))
        yield
        y_ref[0, sl, :] = zs[c:] + loc["kq_v"] + yu
        m_scr[...] = loc["pcol"] * (m + jnp.where(bd_mask, dm, 0.0))


def _run_lockstep(gens):
    gens = list(gens)
    while gens:
        alive = []
        for g in gens:
            try:
                next(g)
                alive.append(g)
            except StopIteration:
                pass
        gens = alive


def _scan_kernel(rf_ref, kf_ref, vf_ref, kkf_ref, lwf_ref, af_ref, rb_ref, kb_ref, vb_ref, kkb_ref, lwb_ref,
                 ab_ref, ka_ref, m0f_ref, m0b_ref, yf_ref, yb_ref, mff_ref, mfb_ref, mf_scr, mb_scr,
                 *, n_sub, n_blk):
    c = CHUNK
    blk = pl.program_id(2)

    @pl.when(blk == 0)
    def _():
        mf_scr[...] = m0f_ref[0, 0]
        mb_scr[...] = m0b_ref[0, 0]

    tc = lax.broadcasted_iota(jnp.int32, (c, MXU_DIM), 0)
    sc = lax.broadcasted_iota(jnp.int32, (c, MXU_DIM), 1) % c
    same16 = (sc // 16) == (tc // 16)
    same32 = (sc // 32) == (tc // 32)
    blk_masks = (jnp.where(sc == tc, 1.0, 0.0), same16,
                 jnp.logical_and(same32, jnp.logical_not(same16)), jnp.logical_not(same32))
    bd_mask = (lax.broadcasted_iota(jnp.int32, (MXU_DIM, MXU_DIM), 0) // HEAD) == \
              (lax.broadcasted_iota(jnp.int32, (MXU_DIM, MXU_DIM), 1) // HEAD)
    bd = jnp.where(bd_mask, 1.0, 0.0).astype(BF16)
    ka = ka_ref[...]

    refs_f = (rf_ref, kf_ref, vf_ref, kkf_ref, lwf_ref, af_ref)
    refs_b = (rb_ref, kb_ref, vb_ref, kkb_ref, lwb_ref, ab_ref)
    masks_f = _scan_masks(False)
    masks_b = _scan_masks(True)
    slices = [slice(j * c, (j + 1) * c) for j in range(n_sub)]
    items_f = [({}, sl) for sl in slices]
    items_b = [({}, sl) for sl in reversed(slices)]
    gens = []
    for (of, sf), (ob, sb) in zip(items_f, items_b):
        gens.append(_scan_chunk_local(of, refs_f, sf, masks_f, blk_masks, ka, c - 1, bd))
        gens.append(_scan_chunk_local(ob, refs_b, sb, masks_b, blk_masks, ka, 0, bd))
    _run_lockstep(gens)
    _run_lockstep([_scan_chain(items_f, mf_scr, yf_ref, bd, bd_mask),
                   _scan_chain(items_b, mb_scr, yb_ref, bd, bd_mask)])

    @pl.when(blk == n_blk - 1)
    def _():
        mff_ref[0, 0] = mf_scr[...]
        mfb_ref[0, 0] = mb_scr[...]


def _scan(r, k, v, kk, lwf, af, lwb, ab, k_a, m0f, m0b):
    b, t, d_rwkv = r.shape
    groups = d_rwkv // MXU_DIM
    tb = SEQ_TILE
    n_blk = t // tb
    n_sub = tb // CHUNK
    tok_f = pl.BlockSpec((1, tb, MXU_DIM), lambda i, g, j: (i, j, g))
    tok_b = pl.BlockSpec((1, tb, MXU_DIM), lambda i, g, j: (i, n_blk - 1 - j, g))
    st = pl.BlockSpec((1, 1, MXU_DIM, MXU_DIM), lambda i, g, j: (i, g, 0, 0))
    y_sd = jax.ShapeDtypeStruct((b, t, d_rwkv), F32)
    m_sd = jax.ShapeDtypeStruct((b, groups, MXU_DIM, MXU_DIM), F32)
    return pl.pallas_call(
        functools.partial(_scan_kernel, n_sub=n_sub, n_blk=n_blk),
        grid=(b, groups, n_blk),
        in_specs=[tok_f] * 6 + [tok_b] * 6 + [pl.BlockSpec((1, MXU_DIM), lambda i, g, j: (0, g)), st, st],
        out_specs=[tok_f, tok_b, st, st],
        out_shape=[y_sd, y_sd, m_sd, m_sd],
        scratch_shapes=[pltpu.VMEM((MXU_DIM, MXU_DIM), F32), pltpu.VMEM((MXU_DIM, MXU_DIM), F32)],
        compiler_params=_cparams(("parallel", "parallel", "arbitrary")),
        name="scan",
    )(r, k, v, kk, lwf, af, r, k, v, kk, lwb, ab, k_a.reshape(1, d_rwkv), m0f, m0b)


def _to_block_diag(s):
    b, h, n, _ = s.shape
    g = h // HEADS_PER_GROUP
    s5 = s.reshape(b, g, HEADS_PER_GROUP, n, n)
    eye = jnp.eye(HEADS_PER_GROUP, dtype=s.dtype)
    out = s5[:, :, :, :, None, :] * eye[None, None, :, None, :, None]
    return out.reshape(b, g, HEADS_PER_GROUP * n, HEADS_PER_GROUP * n)


def _from_block_diag(sbd):
    b, g, m, _ = sbd.shape
    n = m // HEADS_PER_GROUP
    s6 = sbd.reshape(b, g, HEADS_PER_GROUP, n, HEADS_PER_GROUP, n)
    idx = jnp.arange(HEADS_PER_GROUP)
    diag = s6[:, :, idx, :, idx, :]
    return jnp.moveaxis(diag, 0, 2).reshape(b, g * HEADS_PER_GROUP, n, n)


def _post_kernel(yf_ref, yb_ref, r_ref, k_ref, v_ref, af_ref, ab_ref, g_ref,
                 rk_ref, ka_ref, lg_ref, lb_ref, ones_ref, o_ref):
    ones_bd = ones_ref[...]
    inv_n = 1.0 / HEAD
    y = yf_ref[0] + yb_ref[0]
    mu = _head_sum(y, ones_bd) * inv_n
    d = y - mu
    var = _head_sum(d * d, ones_bd) * inv_n
    yn = d * lax.rsqrt(var + GN_EPS) * lg_ref[...] + lb_ref[...]
    ka = ka_ref[...]
    k = k_ref[0]
    kd_sum = k * (1.0 + (af_ref[0] - 1.0) * ka) + k * (1.0 + (ab_ref[0] - 1.0) * ka)
    bonus = _head_sum(r_ref[0] * kd_sum * rk_ref[...], ones_bd) * v_ref[0]
    o_ref[0] = ((yn + bonus) * g_ref[0]).astype(BF16)


def _post(yf, yb, r, k, v, af, ab, g, r_k, k_a, lnx_g, lnx_b, ones_bd):
    b, t, d_rwkv = yf.shape
    tp = min(512, t)
    tok = pl.BlockSpec((1, tp, MXU_DIM), lambda i, j, q: (i, j, q))
    par = pl.BlockSpec((1, MXU_DIM), lambda i, j, q: (0, q))
    return pl.pallas_call(
        _post_kernel,
        grid=(b, t // tp, d_rwkv // MXU_DIM),
        in_specs=[tok] * 8 + [par] * 4 + [pl.BlockSpec((MXU_DIM, MXU_DIM), lambda i, j, q: (0, 0))],
        out_specs=tok,
        out_shape=jax.ShapeDtypeStruct((b, t, d_rwkv), BF16),
        compiler_params=_cparams(("parallel", "parallel", "arbitrary")),
        name="rwkv_post",
    )(yf, yb, r, k, v, af, ab, g, r_k.reshape(1, d_rwkv), k_a.reshape(1, d_rwkv),
      lnx_g.reshape(1, d_rwkv), lnx_b.reshape(1, d_rwkv), ones_bd)


def _window_bounds(n, w):
    t = np.arange(n)
    lo = np.clip(t - w // 2, 0, n)
    hi = np.clip(t - w // 2 + w, 0, n)
    return lo, hi


def _window_matrix(n, w):
    lo, hi = _window_bounds(n, w)
    s = np.arange(n)[None, :]
    m = ((s >= lo[:, None]) & (s < hi[:, None])).astype(np.float32)
    return m / (hi - lo)[:, None].astype(np.float32)


def _pool_ctx_kernel(z_ref, band_ref, pw_ref, ps_ref, o_ref):
    z = z_ref[0]
    m = _dot(band_ref[0], z, precision=lax.Precision.HIGHEST)
    d = (m - z).astype(BF16)
    o_ref[0] = (_dot(d, pw_ref[0]) * ps_ref[...]).astype(BF16)


def _pool_ctx(proj3, col0, pool_w_bf16, pool_scale):
    b, t, _ = proj3.shape
    ng, pg, _ = pool_w_bf16.shape
    band = jnp.asarray(np.stack([_window_matrix(t, w) for w in POOL_WINDOWS]))
    cb = col0 // pg
    return pl.pallas_call(
        _pool_ctx_kernel,
        grid=(b, ng),
        in_specs=[pl.BlockSpec((1, t, pg), lambda i, q: (i, 0, cb + q)),
                  pl.BlockSpec((1, t, t), lambda i, q: (q, 0, 0)),
                  pl.BlockSpec((1, pg, pg), lambda i, q: (q, 0, 0)),
                  pl.BlockSpec((1, pg), lambda i, q: (0, q))],
        out_specs=pl.BlockSpec((1, t, pg), lambda i, q: (i, 0, q)),
        out_shape=jax.ShapeDtypeStruct((b, t, ng * pg), BF16),
        compiler_params=_cparams(("parallel", "arbitrary")),
        name="pool_ctx",
    )(proj3, band, pool_w_bf16, pool_scale.reshape(1, ng * pg))


def _pool_lat_kernel(z_ref, band_ref, pw_ref, ps_ref, o_ref, m1_ref, cs_ref, *, w, n_rows):
    tile = band_ref.shape[0]
    n_tok = n_rows * GRID_W
    band = band_ref[...]
    for i in range(n_tok // tile):
        sl = slice(i * tile, (i + 1) * tile)
        m1_ref[sl, :] = _dot(band, z_ref[0, sl, :], precision=lax.Precision.HIGHEST)
    cs_ref[0:GRID_W, :] = jnp.zeros((GRID_W, cs_ref.shape[1]), F32)
    for rr in range(n_rows):
        cs_ref[(rr + 1) * GRID_W:(rr + 2) * GRID_W, :] = (
            cs_ref[rr * GRID_W:(rr + 1) * GRID_W, :] + m1_ref[rr * GRID_W:(rr + 1) * GRID_W, :])
    lo, hi = _window_bounds(n_rows, w)
    for rr in range(n_rows):
        l, h = int(lo[rr]), int(hi[rr])
        m2 = (cs_ref[h * GRID_W:(h + 1) * GRID_W, :] - cs_ref[l * GRID_W:(l + 1) * GRID_W, :]) / float(h - l)
        sl = slice(rr * GRID_W, (rr + 1) * GRID_W)
        m1_ref[sl, :] = m2 - z_ref[0, sl, :]
    pw = pw_ref[...]
    ps = ps_ref[...]
    for i in range(n_tok // tile):
        sl = slice(i * tile, (i + 1) * tile)
        o_ref[0, sl, :] = (_dot(m1_ref[sl, :].astype(BF16), pw) * ps).astype(BF16)


def _pool_lat_group(proj3, col0, gi, pool_w_bf16, pool_scale):
    b, t, _ = proj3.shape
    ng, pg, _ = pool_w_bf16.shape
    w = POOL_WINDOWS[gi]
    n_rows = t // GRID_W
    tile = MXU_DIM
    band = jnp.asarray(np.kron(np.eye(tile // GRID_W, dtype=np.float32), _window_matrix(GRID_W, w)))
    cb = col0 // pg + gi
    return pl.pallas_call(
        functools.partial(_pool_lat_kernel, w=w, n_rows=n_rows),
        grid=(b,),
        in_specs=[pl.BlockSpec((1, t, pg), lambda i: (i, 0, cb)),
                  pl.BlockSpec((tile, tile), lambda i: (0, 0)),
                  pl.BlockSpec((pg, pg), lambda i: (0, 0)),
                  pl.BlockSpec((1, pg), lambda i: (0, 0))],
        out_specs=pl.BlockSpec((1, t, pg), lambda i: (i, 0, 0)),
        out_shape=jax.ShapeDtypeStruct((b, t, pg), BF16),
        scratch_shapes=[pltpu.VMEM((t, pg), F32), pltpu.VMEM((t + GRID_W, pg), F32)],
        compiler_params=_cparams(("parallel",)),
        name=f"pool_lat_w{w}",
    )(proj3, band, pool_w_bf16[gi], pool_scale.reshape(ng, pg)[gi:gi + 1])


def _up_kernel(ya_ref, yb_ref, wa_ref, wb_ref, ga_ref, gb_ref, o_ref):
    ua = _dot(ya_ref[...], wa_ref[...])
    ub = _dot(yb_ref[...], wb_ref[...])
    o_ref[...] = (_sigmoid(ga_ref[...]) * ua + _sigmoid(gb_ref[...]) * ub).astype(BF16)


def _up(ya2, yb2, wa, wb, proj2, gate_col0, tm=1024, tn=512):
    m, ka = ya2.shape
    tm = min(tm, m)
    kb = yb2.shape[1]
    n = wa.shape[1]
    ca = gate_col0 // tn
    cb = (gate_col0 + n) // tn
    return pl.pallas_call(
        _up_kernel,
        grid=(m // tm, n // tn),
        in_specs=[pl.BlockSpec((tm, ka), lambda i, j: (i, 0)),
                  pl.BlockSpec((tm, kb), lambda i, j: (i, 0)),
                  pl.BlockSpec((ka, tn), lambda i, j: (0, j)),
                  pl.BlockSpec((kb, tn), lambda i, j: (0, j)),
                  pl.BlockSpec((tm, tn), lambda i, j: (i, ca + j)),
                  pl.BlockSpec((tm, tn), lambda i, j: (i, cb + j))],
        out_specs=pl.BlockSpec((tm, tn), lambda i, j: (i, j)),
        out_shape=jax.ShapeDtypeStruct((m, n), BF16),
        compiler_params=_cparams(("parallel", "arbitrary")),
        name="up_merge",
    )(ya2, yb2, wa, wb, proj2, proj2)


def _out_kernel(mg_ref, w_ref, x_ref, mod_ref, g_ref, x1_ref, h2_ref):
    acc = _dot(mg_ref[...], w_ref[...])
    x1 = x_ref[...] + mod_ref[0, 2:3, :] * acc
    x1_ref[...] = x1
    y = x1 * lax.rsqrt(jnp.mean(x1 * x1, axis=-1, keepdims=True) + NORM_EPS)
    h2 = (y * g_ref[...]) * (1.0 + mod_ref[0, 4:5, :]) + mod_ref[0, 3:4, :]
    h2_ref[...] = h2.astype(BF16)


def _out_proj(merged, w_out, x2, mod3, tiles_per_mod_tokens, norm2_g, tm=256):
    m, d = x2.shape
    tm = min(tm, m)
    tpm = tiles_per_mod_tokens // tm
    return pl.pallas_call(
        _out_kernel,
        grid=(m // tm,),
        in_specs=[pl.BlockSpec((tm, d), lambda i: (i, 0)),
                  pl.BlockSpec((d, d), lambda i: (0, 0)),
                  pl.BlockSpec((tm, d), lambda i: (i, 0)),
                  pl.BlockSpec((1, 6, d), lambda i: (i // tpm, 0, 0)),
                  pl.BlockSpec((1, d), lambda i: (0, 0))],
        out_specs=[pl.BlockSpec((tm, d), lambda i: (i, 0)),
                   pl.BlockSpec((tm, d), lambda i: (i, 0))],
        out_shape=[jax.ShapeDtypeStruct((m, d), F32), jax.ShapeDtypeStruct((m, d), BF16)],
        compiler_params=_cparams(("parallel",)),
        name="out_proj",
    )(merged, w_out, x2, mod3, norm2_g.reshape(1, d))


def _ffn1_kernel(h_ref, w1_ref, w3_ref, o_ref):
    h = h_ref[...]
    u1 = _dot(h, w1_ref[...])
    u3 = _dot(h, w3_ref[...])
    o_ref[...] = (u1 * _sigmoid(u1) * u3).astype(BF16)


def _ffn1(h2, w13, d_ff, tm=1024, tn=512):
    m, d = h2.shape
    tm = min(tm, m)
    nj = d_ff // tn
    return pl.pallas_call(
        _ffn1_kernel,
        grid=(m // tm, nj),
        in_specs=[pl.BlockSpec((tm, d), lambda i, j: (i, 0)),
                  pl.BlockSpec((d, tn), lambda i, j: (0, j)),
                  pl.BlockSpec((d, tn), lambda i, j: (0, nj + j))],
        out_specs=pl.BlockSpec((tm, tn), lambda i, j: (i, j)),
        out_shape=jax.ShapeDtypeStruct((m, d_ff), BF16),
        compiler_params=_cparams(("parallel", "arbitrary")),
        name="ffn1",
    )(h2, w13, w13)


def _ffn2_kernel(a_ref, w_ref, x_ref, mod_ref, g_ref, o_ref, acc_ref, *, nk):
    kk = pl.program_id(1)

    @pl.when(kk == 0)
    def _():
        acc_ref[...] = jnp.zeros_like(acc_ref)

    acc_ref[...] += _dot(a_ref[...], w_ref[...])

    @pl.when(kk == nk - 1)
    def _():
        x2 = x_ref[...] + mod_ref[0, 5:6, :] * acc_ref[...]
        y = x2 * lax.rsqrt(jnp.mean(x2 * x2, axis=-1, keepdims=True) + NORM_EPS)
        o_ref[...] = y * g_ref[...]


def _ffn2(act, w2, x1, mod3, tiles_per_mod_tokens, final_g, tm=512, tk=1408):
    m, d = x1.shape
    tm = min(tm, m)
    d_ff = act.shape[1]
    nk = d_ff // tk
    tpm = tiles_per_mod_tokens // tm
    return pl.pallas_call(
        functools.partial(_ffn2_kernel, nk=nk),
        grid=(m // tm, nk),
        in_specs=[pl.BlockSpec((tm, tk), lambda i, j: (i, j)),
                  pl.BlockSpec((tk, d), lambda i, j: (j, 0)),
                  pl.BlockSpec((tm, d), lambda i, j: (i, 0)),
                  pl.BlockSpec((1, 6, d), lambda i, j: (i // tpm, 0, 0)),
                  pl.BlockSpec((1, d), lambda i, j: (0, 0))],
        out_specs=pl.BlockSpec((tm, d), lambda i, j: (i, 0)),
        out_shape=jax.ShapeDtypeStruct((m, d), F32),
        scratch_shapes=[pltpu.VMEM((tm, d), F32)],
        compiler_params=_cparams(("parallel", "arbitrary")),
        name="ffn2",
    )(act, w2, x1, mod3, final_g.reshape(1, d))


def _trunk(x, mod, s0, latent, p):
    b, t, d = x.shape
    m = b * t
    d_rwkv = p["d_rwkv"]
    d_pool = d - d_rwkv
    heads = d_rwkv // HEAD
    x2 = x.reshape(m, d)
    nb = mod.shape[0]
    mod3 = mod.reshape(nb, 6, d)
    tokens_per_mod = t if nb == b else m

    proj = _in_proj(x2, p["norm1_g"], mod3, tokens_per_mod, p["w_in"])
    proj3 = proj.reshape(b, t, proj.shape[1])

    r, k, v, kk, lwf, lwb, af, ab, g = _prep(
        proj3, p["mu"], p["w0"], p["w2"], p["a0"], p["a2"], p["g2"], p["k_k"], p["ones_bd"],
        d_rwkv, p["rw_cols_p"])

    if s0 is None:
        s0 = jnp.zeros((b, 2, heads, HEAD, HEAD), F32)
    m0 = jnp.swapaxes(s0, -1, -2)
    yf, yb, mf, mb = _scan(r, k, v, kk, lwf, af, lwb, ab, p["k_a"],
                           _to_block_diag(m0[:, 0]), _to_block_diag(m0[:, 1]))
    s_fin = jnp.swapaxes(jnp.stack([_from_block_diag(mf), _from_block_diag(mb)], axis=1), -1, -2)
    ya = _post(yf, yb, r, k, v, af, ab, g, p["r_k"], p["k_a"], p["lnx_g"], p["lnx_b"], p["ones_bd"])

    pool_col0 = p["rw_cols_p"]
    if latent:
        yb_pool = jnp.concatenate(
            [_pool_lat_group(proj3, pool_col0, gi, p["pool_w"], p["pool_scale"])
             for gi in range(len(POOL_WINDOWS))], axis=-1)
    else:
        yb_pool = _pool_ctx(proj3, pool_col0, p["pool_w"], p["pool_scale"])

    merged = _up(ya.reshape(m, d_rwkv), yb_pool.reshape(m, d_pool), p["w_up_a"], p["w_up_b"],
                 proj, pool_col0 + d_pool)
    x1, h2 = _out_proj(merged, p["w_out"], x2, mod3, tokens_per_mod, p["norm2_g"])
    act = _ffn1(h2, p["ffn_w13"], p["d_ff"])
    y = _ffn2(act, p["ffn_w2"], x1, mod3, tokens_per_mod, p["final_g"])
    return y.reshape(b, t, d), s_fin


def kernel(x_prompt, x_sample, c, state_rwkv, c_ctx, w_mod, b_mod, norm1_g, w_in, shift_mu, w0, w2, a0, a2, g2,
           k_k, k_a, r_k, lnx_g, lnx_b, w_up_a, pool_w, pool_scale, w_up_b, w_out, norm2_g, ffn_w13, ffn_w2,
           final_g):
    depth = w_in.shape[0]
    assert depth == 1, "final norm is fused into the last layer: single-layer trunk only"
    d = x_prompt.shape[-1]
    d_rwkv = w_up_a.shape[1]
    rw_cols = shift_mu.shape[1]
    rw_cols_p = -(-rw_cols // 512) * 512
    assert x_prompt.shape[1] == SEQ_TILE and x_sample.shape[1] % SEQ_TILE == 0

    l = 0
    w_in_l = w_in[l]
    w_in_p = jnp.concatenate(
        [w_in_l[:, :rw_cols], jnp.zeros((d, rw_cols_p - rw_cols), F32), w_in_l[:, rw_cols:]], axis=1).astype(BF16)
    ones_bd = jnp.asarray(np.kron(np.eye(HEADS_PER_GROUP, dtype=np.float32),
                                  np.ones((HEAD, HEAD), np.float32))).astype(BF16)
    p = {
        "d_rwkv": d_rwkv, "rw_cols_p": rw_cols_p, "d_ff": ffn_w2.shape[1],
        "norm1_g": norm1_g[l], "w_in": w_in_p,
        "mu": jnp.pad(shift_mu[l], (0, rw_cols_p - rw_cols)).reshape(1, rw_cols_p),
        "w0": w0[l], "w2": w2[l], "a0": a0[l], "a2": a2[l], "g2": g2[l],
        "k_k": k_k[l], "k_a": k_a[l], "r_k": r_k[l].reshape(-1), "lnx_g": lnx_g[l], "lnx_b": lnx_b[l],
        "ones_bd": ones_bd,
        "w_up_a": w_up_a[l].astype(BF16), "w_up_b": w_up_b[l].astype(BF16),
        "pool_w": pool_w[l].astype(BF16), "pool_scale": pool_scale[l],
        "w_out": w_out[l].astype(BF16), "norm2_g": norm2_g[l],
        "ffn_w13": ffn_w13[l].astype(BF16), "ffn_w2": ffn_w2[l].astype(BF16), "final_g": final_g,
    }
    cond = jnp.concatenate([c_ctx[None, :], c], axis=0)
    mod = _mod(cond, w_mod[l], b_mod[l])
    y_prompt, s_ctx = _trunk(x_prompt, mod[:1], None, False, p)
    y_sample, _ = _trunk(x_sample, mod[1:], state_rwkv[:, l], True, p)
    new_state = s_ctx[:, None].astype(x_prompt.dtype)
    return (y_prompt, y_sample, new_state)
```

```python
import functools

import numpy as np
import jax
import jax.numpy as jnp
from jax import lax
from jax.experimental import pallas as pl
from jax.experimental.pallas import tpu as pltpu

F32 = jnp.float32
BF16 = jnp.bfloat16

HEAD = 64
POOL_WINDOWS = (2, 4, 8, 16)
GRID_W = 64
DECAY_LORA = 64
AAA_LORA = 64
GATE_LORA = 128
NORM_EPS = 1e-6
GN_EPS = 64e-5

LANES = 128
MXU_DIM = 256
HEADS_PER_GROUP = MXU_DIM // HEAD
VMEM_LIMIT = 56 * 1024 * 1024

CHUNK = 64
SEQ_TILE = 256
SCAN_BLOCK = 1024
SCAN_WAVE = 4


def _cparams(sem):
    return pltpu.CompilerParams(dimension_semantics=sem, vmem_limit_bytes=VMEM_LIMIT)


def _dot(a, b, dims=(((1,), (0,)), ((), ())), precision=None):
    return lax.dot_general(a, b, dims, precision=precision, preferred_element_type=F32)


_NN = (((1,), (0,)), ((), ()))
_NT = (((1,), (1,)), ((), ()))
_TN = (((0,), (0,)), ((), ()))


def _split2(x):
    hi = x.astype(BF16)
    lo = (x - hi.astype(F32)).astype(BF16)
    return hi, lo


def _split3(x):
    hi = x.astype(BF16)
    r1 = x - hi.astype(F32)
    mid = r1.astype(BF16)
    lo = (r1 - mid.astype(F32)).astype(BF16)
    return hi, mid, lo


def _mm(a, b, dims=_NN, passes=1):
    if passes == 1:
        return _dot(a.astype(BF16), b.astype(BF16), dims)
    if passes == 3:
        ah, al = _split2(a)
        bh, bl = _split2(b)
        return _dot(ah, bh, dims) + (_dot(ah, bl, dims) + _dot(al, bh, dims))
    return _dot(a, b, dims, precision=lax.Precision.HIGHEST)


def _mm_exact_rhs(a, b_bf16):
    a0, a1 = _split2(a)
    return _dot(a0, b_bf16) + _dot(a1, b_bf16)


def _sigmoid(x):
    return 1.0 / (1.0 + jnp.exp(-x))


def _mod_kernel(c_ref, w_ref, b_ref, o_ref):
    c = c_ref[...]
    s = (c * _sigmoid(c)).astype(BF16)
    o_ref[...] = _dot(s, w_ref[...].astype(BF16)) + b_ref[...]


def _mod(cond, w_mod, b_mod):
    nb, d = cond.shape
    rows = -(-nb // 16) * 16
    cond_p = jnp.pad(cond, ((0, rows - nb), (0, 0)))
    n = w_mod.shape[1]
    tn = 1024
    out = pl.pallas_call(
        _mod_kernel,
        grid=(n // tn,),
        in_specs=[pl.BlockSpec((rows, d), lambda j: (0, 0)),
                  pl.BlockSpec((d, tn), lambda j: (0, j)),
                  pl.BlockSpec((1, tn), lambda j: (0, j))],
        out_specs=pl.BlockSpec((rows, tn), lambda j: (0, j)),
        out_shape=jax.ShapeDtypeStruct((rows, n), F32),
        compiler_params=_cparams(("arbitrary",)),
        name="mod",
    )(cond_p, w_mod, b_mod.reshape(1, n))
    return out[:nb]


def _in_kernel(x_ref, g_ref, mod_ref, w_ref, om_ref, og_ref, h_ref, *, nj_main):
    j = pl.program_id(1)

    @pl.when(j == 0)
    def _():
        x = x_ref[...]
        y = x * lax.rsqrt(jnp.mean(x * x, axis=-1, keepdims=True) + NORM_EPS)
        h = (y * g_ref[...]) * (1.0 + mod_ref[0, 1:2, :]) + mod_ref[0, 0:1, :]
        h_ref[...] = h.astype(BF16)

    @pl.when(j < nj_main)
    def _():
        om_ref[...] = _dot(h_ref[...], w_ref[...])

    @pl.when(j >= nj_main)
    def _():
        og_ref[...] = _dot(h_ref[...], w_ref[...])


def _in_proj(x2, norm_g, mod3, tiles_per_mod_tokens, w_bf16, n_main, tm=1024, tn=512):
    m, d = x2.shape
    tm = min(tm, m)
    n = w_bf16.shape[1]
    tpm = tiles_per_mod_tokens // tm
    nj_main = n_main // tn
    return pl.pallas_call(
        functools.partial(_in_kernel, nj_main=nj_main),
        grid=(m // tm, n // tn),
        in_specs=[pl.BlockSpec((tm, d), lambda i, j: (i, 0)),
                  pl.BlockSpec((1, d), lambda i, j: (0, 0)),
                  pl.BlockSpec((1, 6, d), lambda i, j: (i // tpm, 0, 0)),
                  pl.BlockSpec((d, tn), lambda i, j: (0, j))],
        out_specs=[pl.BlockSpec((tm, tn), lambda i, j: (i, jnp.minimum(j, nj_main - 1))),
                   pl.BlockSpec((tm, tn), lambda i, j: (i, jnp.maximum(j - nj_main, 0)))],
        out_shape=[jax.ShapeDtypeStruct((m, n_main), F32), jax.ShapeDtypeStruct((m, n - n_main), F32)],
        scratch_shapes=[pltpu.VMEM((tm, d), BF16)],
        compiler_params=_cparams(("parallel", "arbitrary")),
        name="in_proj",
    )(x2, norm_g.reshape(1, d), mod3, w_bf16)


P_LORA_DECAY = 3
P_LORA = 1


def _head_sum(x, ones_bd):
    return _mm_exact_rhs(x, ones_bd)


def _prep_kernel(main_ref, prev_ref, next_ref, mu_ref, w0_ref, w2_ref, a0_ref, a2_ref, g2_ref,
                 kkw_ref, ones_ref, r_o, k_o, v_o, kk_o, lwf_o, lwb_o, af_o, ab_o, g_o, *, n_tiles, d_rwkv):
    t = pl.program_id(1)
    rows = main_ref.shape[1]
    row = lax.broadcasted_iota(jnp.int32, (rows, 1), 0)
    has_prev = t > 0
    has_next = t < n_tiles - 1

    def shifted(c0, c1):
        z = main_ref[0, :, c0:c1]
        p = jnp.where(has_prev, prev_ref[0, 7:8, c0:c1], 0.0)
        n = jnp.where(has_next, next_ref[0, 0:1, c0:c1], 0.0)
        zp = jnp.where(row == 0, p, pltpu.roll(z, 1, 0))
        zn = jnp.where(row == rows - 1, n, pltpu.roll(z, rows - 1, 0))
        return z + mu_ref[:, c0:c1] * (0.5 * (zp + zn) - z)

    dr = d_rwkv
    r_o[0] = shifted(0, dr)
    k = shifted(dr, 2 * dr)
    k_o[0] = k
    v_o[0] = shifted(2 * dr, 3 * dr)

    kkv = k * kkw_ref[...]
    ones_bd = ones_ref[...]
    for j in range(dr // MXU_DIM):
        sl = slice(j * MXU_DIM, (j + 1) * MXU_DIM)
        x = kkv[:, sl]
        ss = _head_sum(x * x, ones_bd)
        kk_o[0, :, sl] = x / jnp.maximum(jnp.sqrt(ss), 1e-12)

    c = 3 * dr
    sm = shifted(c, c + 2 * DECAY_LORA + 2 * AAA_LORA + GATE_LORA)
    o = 0
    for d, lw_o in enumerate((lwf_o, lwb_o)):
        wd = jnp.tanh(sm[:, o:o + DECAY_LORA])
        o += DECAY_LORA
        xw = w0_ref[d:d + 1, :] + _mm(wd, w2_ref[d], passes=P_LORA_DECAY)
        u = -xw
        softplus = jnp.maximum(u, 0.0) + jnp.log1p(jnp.exp(-jnp.abs(u)))
        lw_o[0] = -jnp.exp(-softplus - 0.5)
    for d, a_o in enumerate((af_o, ab_o)):
        ad = sm[:, o:o + AAA_LORA]
        o += AAA_LORA
        a_o[0] = _sigmoid(a0_ref[d:d + 1, :] + _mm(ad, a2_ref[d], passes=P_LORA))
    gd = _sigmoid(sm[:, o:o + GATE_LORA])
    g_o[0] = _mm(gd, g2_ref[...], passes=P_LORA)


def _prep(proj3, mu_p, w0, w2, a0, a2, g2, k_k, ones_bd, d_rwkv, rw_cols_p):
    b, t, _ = proj3.shape
    tp = SEQ_TILE
    n_tiles = t // tp
    sub = tp // 8
    last8 = t // 8 - 1
    out_sd = jax.ShapeDtypeStruct((b, t, d_rwkv), F32)
    o_spec = pl.BlockSpec((1, tp, d_rwkv), lambda i, j: (i, j, 0))

    def full(a):
        nd = a.ndim
        return pl.BlockSpec(a.shape, lambda i, j: (0,) * nd)

    params = (mu_p, w0, w2, a0, a2, g2, k_k.reshape(1, d_rwkv), ones_bd)
    return pl.pallas_call(
        functools.partial(_prep_kernel, n_tiles=n_tiles, d_rwkv=d_rwkv),
        grid=(b, n_tiles),
        in_specs=[pl.BlockSpec((1, tp, rw_cols_p), lambda i, j: (i, j, 0)),
                  pl.BlockSpec((1, 8, rw_cols_p), lambda i, j: (i, jnp.maximum(j * sub - 1, 0), 0)),
                  pl.BlockSpec((1, 8, rw_cols_p), lambda i, j: (i, jnp.minimum((j + 1) * sub, last8), 0)),
                  ] + [full(a) for a in params],
        out_specs=[o_spec] * 9,
        out_shape=[out_sd] * 9,
        compiler_params=_cparams(("parallel", "arbitrary")),
        name="rwkv_prep",
    )(proj3, proj3, proj3, *params)


def _block_diag_rhs(y, bd):
    yb = y.astype(BF16)
    return jnp.concatenate([yb] * HEADS_PER_GROUP, axis=0) * bd


def _mmc(x, y_bd):
    return _dot(x.astype(BF16), y_bd)


def _scan_masks(rev):
    c = CHUNK
    ti = lax.broadcasted_iota(jnp.int32, (c, c), 0)
    si = lax.broadcasted_iota(jnp.int32, (c, c), 1)
    tri = jnp.where((si >= ti) if rev else (si <= ti), 1.0, 0.0).astype(BF16)
    tc = lax.broadcasted_iota(jnp.int32, (c, MXU_DIM), 0)
    sc = lax.broadcasted_iota(jnp.int32, (c, MXU_DIM), 1) % c
    before = (sc > tc) if rev else (sc < tc)
    upto = (sc >= tc) if rev else (sc <= tc)
    return tri, before, upto


def _scan_chunk_local(out, refs, sl, dir_masks, blk_masks, ka, last_row, bd):
    r_ref, k_ref, v_ref, kk_ref, lw_ref, a_ref = refs
    tri, before, upto = dir_masks
    eye, same16, off32, off64 = blk_masks
    c = CHUNK
    kk = kk_ref[0, sl, :]
    lw = lw_ref[0, sl, :]
    a = a_ref[0, sl, :]
    lw0, lw1, lw2 = _split3(lw)
    cum = _dot(tri, lw0) + (_dot(tri, lw1) + _dot(tri, lw2))
    yield
    e_in = jnp.exp(cum)
    e_inv = jnp.exp(-cum)
    al = kk * jnp.exp(cum - lw)
    rh = r_ref[0, sl, :] * e_in
    be = (kk * a) * e_inv
    kap = (k_ref[0, sl, :] * (1.0 + (a - 1.0) * ka)) * e_inv
    ptot = e_in[last_row:last_row + 1, :]
    ar = jnp.concatenate([al, rh], axis=0).astype(BF16)
    ab = _dot(ar, _block_diag_rhs(be, bd), _NT)
    ak = _dot(ar, _block_diag_rhs(kap, bd), _NT)
    yield
    lm = jnp.where(before, ab[:c], 0.0)
    aak = jnp.where(before, ak[:c], 0.0)
    kq = jnp.where(upto, ak[c:], 0.0)
    out["bq"] = jnp.where(upto, ab[c:], 0.0).astype(BF16)
    ld = jnp.where(same16, lm, 0.0)
    p = eye - ld
    ld_bd = _block_diag_rhs(ld, bd)
    l2 = _mmc(ld, ld_bd)
    yield
    pl2 = _mmc(jnp.concatenate([p, l2], axis=0), _block_diag_rhs(l2, bd))
    yield
    p = p + pl2[:c]
    l4 = pl2[c:]
    pl4 = _mmc(jnp.concatenate([p, l4], axis=0), _block_diag_rhs(l4, bd))
    yield
    p = p + pl4[:c]
    p8 = _mmc(p, _block_diag_rhs(pl4[c:], bd))
    v = v_ref[0, sl, :]
    akv = _mmc(jnp.concatenate([aak, kq], axis=0), _block_diag_rhs(v, bd))
    aak_v = akv[:c]
    out["kq_v"] = akv[c:]
    yield
    p = p + p8
    for off in (off32, off64):
        lo = jnp.where(off, lm, 0.0)
        t1 = _mmc(lo, _block_diag_rhs(p, bd))
        yield
        t2 = _mmc(p, _block_diag_rhs(t1, bd))
        yield
        p = p - t2
    wt = _mmc(p, _block_diag_rhs(al, bd))
    out["ut"] = _mmc(p, _block_diag_rhs(aak_v, bd))
    yield
    out["wr"] = jnp.concatenate([wt, rh], axis=0).astype(BF16)
    out["v"] = v.astype(BF16)
    out["bkT"] = jnp.transpose(jnp.concatenate([be, kap], axis=0)).astype(BF16)
    pc = jnp.transpose(jnp.broadcast_to(ptot, (8, MXU_DIM)))
    out["pcol"] = jnp.broadcast_to(pc[:, 0:1], (MXU_DIM, MXU_DIM))


def _scan_chain(items, m_scr, y_ref, bd, bd_mask):
    c = CHUNK
    for loc, sl in items:
        m = m_scr[...]
        zs = _dot(loc["wr"], m.astype(BF16))
        yield
        u = -(zs[:c] + loc["ut"])
        uv = jnp.concatenate([u.astype(BF16), loc["v"]], axis=0)
        dm = _dot(loc["bkT"], uv)
        yu = _mmc(loc["bq"], _block_diag_rhs(u, bd))
        yield
        y_ref[0, sl, :] = zs[c:] + loc["kq_v"] + yu
        m_scr[...] = loc["pcol"] * (m + jnp.where(bd_mask, dm, 0.0))


def _run_lockstep(gens):
    gens = list(gens)
    while gens:
        alive = []
        for g in gens:
            try:
                next(g)
                alive.append(g)
            except StopIteration:
                pass
        gens = alive


def _scan_kernel(rf_ref, kf_ref, vf_ref, kkf_ref, lwf_ref, af_ref, rb_ref, kb_ref, vb_ref, kkb_ref, lwb_ref,
                 ab_ref, ka_ref, m0f_ref, m0b_ref, yf_ref, yb_ref, mff_ref, mfb_ref, mf_scr, mb_scr,
                 *, n_sub, n_blk):
    c = CHUNK
    blk = pl.program_id(2)

    @pl.when(blk == 0)
    def _():
        mf_scr[...] = m0f_ref[0, 0]
        mb_scr[...] = m0b_ref[0, 0]

    tc = lax.broadcasted_iota(jnp.int32, (c, MXU_DIM), 0)
    sc = lax.broadcasted_iota(jnp.int32, (c, MXU_DIM), 1) % c
    same16 = (sc // 16) == (tc // 16)
    same32 = (sc // 32) == (tc // 32)
    blk_masks = (jnp.where(sc == tc, 1.0, 0.0), same16,
                 jnp.logical_and(same32, jnp.logical_not(same16)), jnp.logical_not(same32))
    bd_mask = (lax.broadcasted_iota(jnp.int32, (MXU_DIM, MXU_DIM), 0) // HEAD) == \
              (lax.broadcasted_iota(jnp.int32, (MXU_DIM, MXU_DIM), 1) // HEAD)
    bd = jnp.where(bd_mask, 1.0, 0.0).astype(BF16)
    ka = ka_ref[...]

    refs_f = (rf_ref, kf_ref, vf_ref, kkf_ref, lwf_ref, af_ref)
    refs_b = (rb_ref, kb_ref, vb_ref, kkb_ref, lwb_ref, ab_ref)
    masks_f = _scan_masks(False)
    masks_b = _scan_masks(True)
    slices = [slice(j * c, (j + 1) * c) for j in range(n_sub)]
    items_f = [({}, sl) for sl in slices]
    items_b = [({}, sl) for sl in reversed(slices)]
    chains = []
    for w0 in range(0, n_sub, SCAN_WAVE):
        wave_f = items_f[w0:w0 + SCAN_WAVE]
        wave_b = items_b[w0:w0 + SCAN_WAVE]
        gens = []
        for (of, sf), (ob, sb) in zip(wave_f, wave_b):
            gens.append(_scan_chunk_local(of, refs_f, sf, masks_f, blk_masks, ka, c - 1, bd))
            gens.append(_scan_chunk_local(ob, refs_b, sb, masks_b, blk_masks, ka, 0, bd))
        _run_lockstep(gens + chains)
        chains = [_scan_chain(wave_f, mf_scr, yf_ref, bd, bd_mask),
                  _scan_chain(wave_b, mb_scr, yb_ref, bd, bd_mask)]
    _run_lockstep(chains)

    @pl.when(blk == n_blk - 1)
    def _():
        mff_ref[0, 0] = mf_scr[...]
        mfb_ref[0, 0] = mb_scr[...]


def _scan(r, k, v, kk, lwf, af, lwb, ab, k_a, m0f, m0b):
    b, t, d_rwkv = r.shape
    groups = d_rwkv // MXU_DIM
    tb = min(SCAN_BLOCK, t)
    n_blk = t // tb
    n_sub = tb // CHUNK
    tok_f =pl.BlockSpec((1, tb, MXU_DIM), lambda i, g, j: (i, j, g))
    tok_b = pl.BlockSpec((1, tb, MXU_DIM), lambda i, g, j: (i, n_blk - 1 - j, g))
    st = pl.BlockSpec((1, 1, MXU_DIM, MXU_DIM), lambda i, g, j: (i, g, 0, 0))
    y_sd = jax.ShapeDtypeStruct((b, t, d_rwkv), F32)
    m_sd = jax.ShapeDtypeStruct((b, groups, MXU_DIM, MXU_DIM), F32)
    return pl.pallas_call(
        functools.partial(_scan_kernel, n_sub=n_sub, n_blk=n_blk),
        grid=(b, groups, n_blk),
        in_specs=[tok_f] * 6 + [tok_b] * 6 + [pl.BlockSpec((1, MXU_DIM), lambda i, g, j: (0, g)), st, st],
        out_specs=[tok_f, tok_b, st, st],
        out_shape=[y_sd, y_sd, m_sd, m_sd],
        scratch_shapes=[pltpu.VMEM((MXU_DIM, MXU_DIM), F32), pltpu.VMEM((MXU_DIM, MXU_DIM), F32)],
        compiler_params=_cparams(("parallel", "parallel", "arbitrary")),
        name="scan",
    )(r, k, v, kk, lwf, af, r, k, v, kk, lwb, ab, k_a.reshape(1, d_rwkv), m0f, m0b)


def _to_block_diag(s):
    b, h, n, _ = s.shape
    g = h // HEADS_PER_GROUP
    s5 = s.reshape(b, g, HEADS_PER_GROUP, n, n)
    eye = jnp.eye(HEADS_PER_GROUP, dtype=s.dtype)
    out = s5[:, :, :, :, None, :] * eye[None, None, :, None, :, None]
    return out.reshape(b, g, HEADS_PER_GROUP * n, HEADS_PER_GROUP * n)


def _from_block_diag(sbd):
    b, g, m, _ = sbd.shape
    n = m // HEADS_PER_GROUP
    s6 = sbd.reshape(b, g, HEADS_PER_GROUP, n, HEADS_PER_GROUP, n)
    idx = jnp.arange(HEADS_PER_GROUP)
    diag = s6[:, :, idx, :, idx, :]
    return jnp.moveaxis(diag, 0, 2).reshape(b, g * HEADS_PER_GROUP, n, n)


def _post_kernel(yf_ref, yb_ref, r_ref, k_ref, v_ref, af_ref, ab_ref, g_ref,
                 rk_ref, ka_ref, lg_ref, lb_ref, ones_ref, o_ref):
    ones_bd = ones_ref[...]
    inv_n = 1.0 / HEAD
    y = yf_ref[0] + yb_ref[0]
    mu = _head_sum(y, ones_bd) * inv_n
    d = y - mu
    var = _head_sum(d * d, ones_bd) * inv_n
    yn = d * lax.rsqrt(var + GN_EPS) * lg_ref[...] + lb_ref[...]
    ka = ka_ref[...]
    k = k_ref[0]
    kd_sum = k * (1.0 + (af_ref[0] - 1.0) * ka) + k * (1.0 + (ab_ref[0] - 1.0) * ka)
    bonus = _head_sum(r_ref[0] * kd_sum * rk_ref[...], ones_bd) * v_ref[0]
    o_ref[0] = ((yn + bonus) * g_ref[0]).astype(BF16)


def _post(yf, yb, r, k, v, af, ab, g, r_k, k_a, lnx_g, lnx_b, ones_bd):
    b, t, d_rwkv = yf.shape
    tp = min(512, t)
    tok = pl.BlockSpec((1, tp, MXU_DIM), lambda i, j, q: (i, j, q))
    par = pl.BlockSpec((1, MXU_DIM), lambda i, j, q: (0, q))
    return pl.pallas_call(
        _post_kernel,
        grid=(b, t // tp, d_rwkv // MXU_DIM),
        in_specs=[tok] * 8 + [par] * 4 + [pl.BlockSpec((MXU_DIM, MXU_DIM), lambda i, j, q: (0, 0))],
        out_specs=tok,
        out_shape=jax.ShapeDtypeStruct((b, t, d_rwkv), BF16),
        compiler_params=_cparams(("parallel", "parallel", "arbitrary")),
        name="rwkv_post",
    )(yf, yb, r, k, v, af, ab, g, r_k.reshape(1, d_rwkv), k_a.reshape(1, d_rwkv),
      lnx_g.reshape(1, d_rwkv), lnx_b.reshape(1, d_rwkv), ones_bd)


def _window_bounds(n, w):
    t = np.arange(n)
    lo = np.clip(t - w // 2, 0, n)
    hi = np.clip(t - w // 2 + w, 0, n)
    return lo, hi


def _window_matrix(n, w):
    lo, hi = _window_bounds(n, w)
    s = np.arange(n)[None, :]
    m = ((s >= lo[:, None]) & (s < hi[:, None])).astype(np.float32)
    return m / (hi - lo)[:, None].astype(np.float32)


def _pool_ctx_kernel(z_ref, band_ref, pw_ref, ps_ref, o_ref):
    z = z_ref[0]
    m = _dot(band_ref[0], z, precision=lax.Precision.HIGHEST)
    d = (m - z).astype(BF16)
    o_ref[0] = (_dot(d, pw_ref[0]) * ps_ref[...]).astype(BF16)


def _pool_ctx(proj3, col0, pool_w_bf16, pool_scale):
    b, t, _ = proj3.shape
    ng, pg, _ = pool_w_bf16.shape
    band = jnp.asarray(np.stack([_window_matrix(t, w) for w in POOL_WINDOWS]))
    cb = col0 // pg
    return pl.pallas_call(
        _pool_ctx_kernel,
        grid=(b, ng),
        in_specs=[pl.BlockSpec((1, t, pg), lambda i, q: (i, 0, cb + q)),
                  pl.BlockSpec((1, t, t), lambda i, q: (q, 0, 0)),
                  pl.BlockSpec((1, pg, pg), lambda i, q: (q, 0, 0)),
                  pl.BlockSpec((1, pg), lambda i, q: (0, q))],
        out_specs=pl.BlockSpec((1, t, pg), lambda i, q: (i, 0, q)),
        out_shape=jax.ShapeDtypeStruct((b, t, ng * pg), BF16),
        compiler_params=_cparams(("parallel", "arbitrary")),
        name="pool_ctx",
    )(proj3, band, pool_w_bf16, pool_scale.reshape(1, ng * pg))


def _pool_lat_kernel(z_ref, band_ref, pw_ref, ps_ref, o_ref, m1_ref, cs_ref, *, w, n_rows):
    tile = band_ref.shape[0]
    n_tok = n_rows * GRID_W
    band = band_ref[...]
    for i in range(n_tok // tile):
        sl = slice(i * tile, (i + 1) * tile)
        m1_ref[sl, :] = _dot(band, z_ref[0, sl, :], precision=lax.Precision.HIGHEST)
    cs_ref[0:GRID_W, :] = jnp.zeros((GRID_W, cs_ref.shape[1]), F32)
    for rr in range(n_rows):
        cs_ref[(rr + 1) * GRID_W:(rr + 2) * GRID_W, :] = (
            cs_ref[rr * GRID_W:(rr + 1) * GRID_W, :] + m1_ref[rr * GRID_W:(rr + 1) * GRID_W, :])
    lo, hi = _window_bounds(n_rows, w)
    for rr in range(n_rows):
        l, h = int(lo[rr]), int(hi[rr])
        m2 = (cs_ref[h * GRID_W:(h + 1) * GRID_W, :] - cs_ref[l * GRID_W:(l + 1) * GRID_W, :]) / float(h - l)
        sl = slice(rr * GRID_W, (rr + 1) * GRID_W)
        m1_ref[sl, :] = m2 - z_ref[0, sl, :]
    pw = pw_ref[...]
    ps = ps_ref[...]
    for i in range(n_tok // tile):
        sl = slice(i * tile, (i + 1) * tile)
        o_ref[0, sl, :] = (_dot(m1_ref[sl, :].astype(BF16), pw) * ps).astype(BF16)


def _pool_lat_group(proj3, col0, gi, pool_w_bf16, pool_scale):
    b, t, _ = proj3.shape
    ng, pg, _ = pool_w_bf16.shape
    w = POOL_WINDOWS[gi]
    n_rows = t // GRID_W
    tile = MXU_DIM
    band = jnp.asarray(np.kron(np.eye(tile // GRID_W, dtype=np.float32), _window_matrix(GRID_W, w)))
    cb = col0 // pg + gi
    return pl.pallas_call(
        functools.partial(_pool_lat_kernel, w=w, n_rows=n_rows),
        grid=(b,),
        in_specs=[pl.BlockSpec((1, t, pg), lambda i: (i, 0, cb)),
                  pl.BlockSpec((tile, tile), lambda i: (0, 0)),
                  pl.BlockSpec((pg, pg), lambda i: (0, 0)),
                  pl.BlockSpec((1, pg), lambda i: (0, 0))],
        out_specs=pl.BlockSpec((1, t, pg), lambda i: (i, 0, 0)),
        out_shape=jax.ShapeDtypeStruct((b, t, pg), BF16),
        scratch_shapes=[pltpu.VMEM((t, pg), F32), pltpu.VMEM((t + GRID_W, pg), F32)],
        compiler_params=_cparams(("parallel",)),
        name=f"pool_lat_w{w}",
    )(proj3, band, pool_w_bf16[gi], pool_scale.reshape(ng, pg)[gi:gi + 1])


ROW_SPLIT = 2


def _resident(shape):
    nd = len(shape)
    return pl.BlockSpec(shape, lambda i: (0,) * nd, pipeline_mode=pl.Buffered(1))


def _up_out_kernel(ya_ref, yb_ref, ga_ref, gb_ref, x_ref, mod_ref, g_ref, wa_ref, wb_ref, wo_ref,
                   x1_ref, h2_ref):
    rows = x_ref.shape[0] // ROW_SPLIT
    for part in range(ROW_SPLIT):
        sl = slice(part * rows, (part + 1) * rows)
        ua = _dot(ya_ref[sl, :], wa_ref[...])
        ub = _dot(yb_ref[sl, :], wb_ref[...])
        merged = (_sigmoid(ga_ref[sl, :]) * ua + _sigmoid(gb_ref[sl, :]) * ub).astype(BF16)
        x1 = x_ref[sl, :] + mod_ref[0, 2:3, :] * _dot(merged, wo_ref[...])
        x1_ref[sl, :] = x1
        y = x1 * lax.rsqrt(jnp.mean(x1 * x1, axis=-1, keepdims=True) + NORM_EPS)
        h2 = (y * g_ref[...]) * (1.0 + mod_ref[0, 4:5, :]) + mod_ref[0, 3:4, :]
        h2_ref[sl, :] = h2.astype(BF16)


def _up_out(ya2, yb2, gates, wa, wb, w_out, x2, mod3, tiles_per_mod_tokens, norm2_g, tm=256):
    m, d = x2.shape
    tm = min(tm, m)
    ka, kb = ya2.shape[1], yb2.shape[1]
    tpm = tiles_per_mod_tokens // tm
    row = lambda width, col: pl.BlockSpec((tm, width), lambda i: (i, col))
    return pl.pallas_call(
        _up_out_kernel,
        grid=(m // tm,),
        in_specs=[row(ka, 0), row(kb, 0), row(d, 0), row(d, 1), row(d, 0),
                  pl.BlockSpec((1, 6, d), lambda i: (i // tpm, 0, 0)),
                  _resident((1, d)), _resident((ka, d)), _resident((kb, d)), _resident((d, d))],
        out_specs=[row(d, 0), row(d, 0)],
        out_shape=[jax.ShapeDtypeStruct((m, d), F32), jax.ShapeDtypeStruct((m, d), BF16)],
        compiler_params=_cparams(("parallel",)),
        name="up_out",
    )(ya2, yb2, gates, gates, x2, mod3, norm2_g.reshape(1, d), wa, wb, w_out)


def _ffn1_kernel(h_ref, w1_ref, w3_ref, o_ref):
    h = h_ref[...]
    u1 = _dot(h, w1_ref[...])
    u3 = _dot(h, w3_ref[...])
    o_ref[...] = (u1 * _sigmoid(u1) * u3).astype(BF16)


def _ffn1(h2, w13, d_ff, tm=1024, tn=512):
    m, d = h2.shape
    tm = min(tm, m)
    nj = d_ff // tn
    return pl.pallas_call(
        _ffn1_kernel,
        grid=(m // tm, nj),
        in_specs=[pl.BlockSpec((tm, d), lambda i, j: (i, 0)),
                  pl.BlockSpec((d, tn), lambda i, j: (0, j)),
                  pl.BlockSpec((d, tn), lambda i, j: (0, nj + j))],
        out_specs=pl.BlockSpec((tm, tn), lambda i, j: (i, j)),
        out_shape=jax.ShapeDtypeStruct((m, d_ff), BF16),
        compiler_params=_cparams(("parallel", "arbitrary")),
        name="ffn1",
    )(h2, w13, w13)


def _ffn2_kernel(a_ref, x_ref, mod_ref, g_ref, w_ref, o_ref):
    rows = x_ref.shape[0] // ROW_SPLIT
    for part in range(ROW_SPLIT):
        sl = slice(part * rows, (part + 1) * rows)
        x2 = x_ref[sl, :] + mod_ref[0, 5:6, :] * _dot(a_ref[sl, :], w_ref[...])
        y = x2 * lax.rsqrt(jnp.mean(x2 * x2, axis=-1, keepdims=True) + NORM_EPS)
        o_ref[sl, :] = y * g_ref[...]


def _ffn2(act, w2, x1, mod3, tiles_per_mod_tokens, final_g, tm=256):
    m, d = x1.shape
    tm = min(tm, m)
    d_ff = act.shape[1]
    tpm = tiles_per_mod_tokens // tm
    return pl.pallas_call(
        _ffn2_kernel,
        grid=(m // tm,),
        in_specs=[pl.BlockSpec((tm, d_ff), lambda i: (i, 0)),
                  pl.BlockSpec((tm, d), lambda i: (i, 0)),
                  pl.BlockSpec((1, 6, d), lambda i: (i // tpm, 0, 0)),
                  _resident((1, d)), _resident((d_ff, d))],
        out_specs=pl.BlockSpec((tm, d), lambda i: (i, 0)),
        out_shape=jax.ShapeDtypeStruct((m, d), F32),
        compiler_params=_cparams(("parallel",)),
        name="ffn2",
    )(act, x1, mod3, final_g.reshape(1, d), w2)


def _trunk(x, mod, s0, latent, p):
    b, t, d = x.shape
    m = b * t
    d_rwkv = p["d_rwkv"]
    d_pool = d - d_rwkv
    heads = d_rwkv // HEAD
    x2 = x.reshape(m, d)
    nb = mod.shape[0]
    mod3 = mod.reshape(nb, 6, d)
    tokens_per_mod = t if nb == b else m

    proj, gates = _in_proj(x2, p["norm1_g"], mod3, tokens_per_mod, p["w_in"], p["rw_cols_p"] + d_pool)
    proj3 = proj.reshape(b, t, proj.shape[1])

    r, k, v, kk, lwf, lwb, af, ab, g = _prep(
        proj3, p["mu"], p["w0"], p["w2"], p["a0"], p["a2"], p["g2"], p["k_k"], p["ones_bd"],
        d_rwkv, p["rw_cols_p"])

    if s0 is None:
        s0 = jnp.zeros((b, 2, heads, HEAD, HEAD), F32)
    m0 = jnp.swapaxes(s0, -1, -2)
    yf, yb, mf, mb = _scan(r, k, v, kk, lwf, af, lwb, ab, p["k_a"],
                           _to_block_diag(m0[:, 0]), _to_block_diag(m0[:, 1]))
    s_fin = jnp.swapaxes(jnp.stack([_from_block_diag(mf), _from_block_diag(mb)], axis=1), -1, -2)
    ya = _post(yf, yb, r, k, v, af, ab, g, p["r_k"], p["k_a"], p["lnx_g"], p["lnx_b"], p["ones_bd"])

    pool_col0 = p["rw_cols_p"]
    if latent:
        yb_pool = jnp.concatenate(
            [_pool_lat_group(proj3, pool_col0, gi, p["pool_w"], p["pool_scale"])
             for gi in range(len(POOL_WINDOWS))], axis=-1)
    else:
        yb_pool = _pool_ctx(proj3, pool_col0, p["pool_w"], p["pool_scale"])

    x1, h2 = _up_out(ya.reshape(m, d_rwkv), yb_pool.reshape(m, d_pool), gates, p["w_up_a"], p["w_up_b"],
                     p["w_out"], x2, mod3, tokens_per_mod, p["norm2_g"])
    act = _ffn1(h2, p["ffn_w13"], p["d_ff"])
    y = _ffn2(act, p["ffn_w2"], x1, mod3, tokens_per_mod, p["final_g"])
    return y.reshape(b, t, d), s_fin


def kernel(x_prompt, x_sample, c, state_rwkv, c_ctx, w_mod, b_mod, norm1_g, w_in, shift_mu, w0, w2, a0, a2, g2,
           k_k, k_a, r_k, lnx_g, lnx_b, w_up_a, pool_w, pool_scale, w_up_b, w_out, norm2_g, ffn_w13, ffn_w2,
           final_g):
    depth = w_in.shape[0]
    assert depth == 1, "final norm is fused into the last layer: single-layer trunk only"
    d = x_prompt.shape[-1]
    d_rwkv = w_up_a.shape[1]
    rw_cols = shift_mu.shape[1]
    rw_cols_p = -(-rw_cols // 512) * 512
    assert x_prompt.shape[1] == SEQ_TILE and x_sample.shape[1] % SEQ_TILE == 0

    l = 0
    w_in_l = w_in[l]
    w_in_p = jnp.concatenate(
        [w_in_l[:, :rw_cols], jnp.zeros((d, rw_cols_p - rw_cols), F32), w_in_l[:, rw_cols:]], axis=1).astype(BF16)
    ones_bd = jnp.asarray(np.kron(np.eye(HEADS_PER_GROUP, dtype=np.float32),
                                  np.ones((HEAD, HEAD), np.float32))).astype(BF16)
    p = {
        "d_rwkv": d_rwkv, "rw_cols_p": rw_cols_p, "d_ff": ffn_w2.shape[1],
        "norm1_g": norm1_g[l], "w_in": w_in_p,
        "mu": jnp.pad(shift_mu[l], (0, rw_cols_p - rw_cols)).reshape(1, rw_cols_p),
        "w0": w0[l], "w2": w2[l], "a0": a0[l], "a2": a2[l], "g2": g2[l],
        "k_k": k_k[l], "k_a": k_a[l], "r_k": r_k[l].reshape(-1), "lnx_g": lnx_g[l], "lnx_b": lnx_b[l],
        "ones_bd": ones_bd,
        "w_up_a": w_up_a[l].astype(BF16), "w_up_b": w_up_b[l].astype(BF16),
        "pool_w": pool_w[l].astype(BF16), "pool_scale": pool_scale[l],
        "w_out": w_out[l].astype(BF16), "norm2_g": norm2_g[l],
        "ffn_w13": ffn_w13[l].astype(BF16), "ffn_w2": ffn_w2[l].astype(BF16), "final_g": final_g,
    }
    cond = jnp.concatenate([c_ctx[None, :], c], axis=0)
    mod = _mod(cond, w_mod[l], b_mod[l])
    y_prompt, s_ctx = _trunk(x_prompt, mod[:1], None, False, p)
    y_sample, _ = _trunk(x_sample, mod[1:], state_rwkv[:, l], True, p)
    new_state = s_ctx[:, None].astype(x_prompt.dtype)
    return (y_prompt, y_sample, new_state)
```

```python
import functools

import numpy as np
import jax
import jax.numpy as jnp
from jax import lax
from jax.experimental import pallas as pl
from jax.experimental.pallas import tpu as pltpu

F32 = jnp.float32
BF16 = jnp.bfloat16

HEAD = 64
POOL_WINDOWS = (2, 4, 8, 16)
GRID_W = 64
DECAY_LORA = 64
AAA_LORA = 64
GATE_LORA = 128
NORM_EPS = 1e-6
GN_EPS = 64e-5

LANES = 128
MXU_DIM = 256
HEADS_PER_GROUP = MXU_DIM // HEAD
VMEM_LIMIT = 56 * 1024 * 1024

CHUNK = 64
SEQ_TILE = 256
SCAN_BLOCK = 1024
SCAN_WAVE = 4


def _cparams(sem):
    return pltpu.CompilerParams(dimension_semantics=sem, vmem_limit_bytes=VMEM_LIMIT)


def _dot(a, b, dims=(((1,), (0,)), ((), ())), precision=None):
    return lax.dot_general(a, b, dims, precision=precision, preferred_element_type=F32)


_NN = (((1,), (0,)), ((), ()))
_NT = (((1,), (1,)), ((), ()))
_TN = (((0,), (0,)), ((), ()))


def _split2(x):
    hi = x.astype(BF16)
    lo = (x - hi.astype(F32)).astype(BF16)
    return hi, lo


def _split3(x):
    hi = x.astype(BF16)
    r1 = x - hi.astype(F32)
    mid = r1.astype(BF16)
    lo = (r1 - mid.astype(F32)).astype(BF16)
    return hi, mid, lo


def _mm(a, b, dims=_NN, passes=1):
    if passes == 1:
        return _dot(a.astype(BF16), b.astype(BF16), dims)
    if passes == 3:
        ah, al = _split2(a)
        bh, bl = _split2(b)
        return _dot(ah, bh, dims) + (_dot(ah, bl, dims) + _dot(al, bh, dims))
    return _dot(a, b, dims, precision=lax.Precision.HIGHEST)


def _mm_exact_rhs(a, b_bf16):
    a0, a1 = _split2(a)
    return _dot(a0, b_bf16) + _dot(a1, b_bf16)


def _sigmoid(x):
    return 1.0 / (1.0 + jnp.exp(-x))


def _mod_kernel(c_ref, w_ref, b_ref, o_ref):
    c = c_ref[...]
    s = (c * _sigmoid(c)).astype(BF16)
    o_ref[...] = _dot(s, w_ref[...].astype(BF16)) + b_ref[...]


def _mod(cond, w_mod, b_mod):
    nb, d = cond.shape
    rows = -(-nb // 16) * 16
    cond_p = jnp.pad(cond, ((0, rows - nb), (0, 0)))
    n = w_mod.shape[1]
    tn = 1024
    out = pl.pallas_call(
        _mod_kernel,
        grid=(n // tn,),
        in_specs=[pl.BlockSpec((rows, d), lambda j: (0, 0)),
                  pl.BlockSpec((d, tn), lambda j: (0, j)),
                  pl.BlockSpec((1, tn), lambda j: (0, j))],
        out_specs=pl.BlockSpec((rows, tn), lambda j: (0, j)),
        out_shape=jax.ShapeDtypeStruct((rows, n), F32),
        compiler_params=_cparams(("arbitrary",)),
        name="mod",
    )(cond_p, w_mod, b_mod.reshape(1, n))
    return out[:nb]


def _in_kernel(x_ref, g_ref, mod_ref, w_ref, om_ref, og_ref, h_ref, *, nj_main):
    j = pl.program_id(1)

    @pl.when(j == 0)
    def _():
        x = x_ref[...]
        y = x * lax.rsqrt(jnp.mean(x * x, axis=-1, keepdims=True) + NORM_EPS)
        h = (y * g_ref[...]) * (1.0 + mod_ref[0, 1:2, :]) + mod_ref[0, 0:1, :]
        h_ref[...] = h.astype(BF16)

    @pl.when(j < nj_main)
    def _():
        om_ref[...] = _dot(h_ref[...], w_ref[...])

    @pl.when(j >= nj_main)
    def _():
        og_ref[...] = _dot(h_ref[...], w_ref[...])


def _in_proj(x2, norm_g, mod3, tiles_per_mod_tokens, w_bf16, n_main, tm=1024, tn=512):
    m, d = x2.shape
    tm = min(tm, m)
    n = w_bf16.shape[1]
    tpm = tiles_per_mod_tokens // tm
    nj_main = n_main // tn
    return pl.pallas_call(
        functools.partial(_in_kernel, nj_main=nj_main),
        grid=(m // tm, n // tn),
        in_specs=[pl.BlockSpec((tm, d), lambda i, j: (i, 0)),
                  pl.BlockSpec((1, d), lambda i, j: (0, 0)),
                  pl.BlockSpec((1, 6, d), lambda i, j: (i // tpm, 0, 0)),
                  pl.BlockSpec((d, tn), lambda i, j: (0, j))],
        out_specs=[pl.BlockSpec((tm, tn), lambda i, j: (i, jnp.minimum(j, nj_main - 1))),
                   pl.BlockSpec((tm, tn), lambda i, j: (i, jnp.maximum(j - nj_main, 0)))],
        out_shape=[jax.ShapeDtypeStruct((m, n_main), F32), jax.ShapeDtypeStruct((m, n - n_main), F32)],
        scratch_shapes=[pltpu.VMEM((tm, d), BF16)],
        compiler_params=_cparams(("parallel", "arbitrary")),
        name="in_proj",
    )(x2, norm_g.reshape(1, d), mod3, w_bf16)


P_LORA_DECAY = 3
P_LORA = 1


def _head_sum(x, ones_bd):
    return _mm_exact_rhs(x, ones_bd)


def _prep_kernel(main_ref, prev_ref, next_ref, mu_ref, w0_ref, w2_ref, a0_ref, a2_ref, g2_ref,
                 kkw_ref, ka_ref, rk_ref, ones_ref, r_o, k_o, v_o, kk_o, lwf_o, lwb_o, af_o, ab_o, g_o, bg_o,
                 *, n_tiles, d_rwkv):
    t = pl.program_id(1)
    rows = main_ref.shape[1]
    row8 = lax.broadcasted_iota(jnp.int32, (8, 1), 0)
    has_prev = t > 0
    has_next = t < n_tiles - 1

    def shifted(c0, c1):
        z = main_ref[0, :, c0:c1]
        p = jnp.where(has_prev, prev_ref[0, 7:8, c0:c1], 0.0)
        n = jnp.where(has_next, next_ref[0, 0:1, c0:c1], 0.0)
        zp = pltpu.roll(z, 1, 0)
        zn = pltpu.roll(z, rows - 1, 0)
        zp = jnp.concatenate([jnp.where(row8 == 0, p, zp[:8]), zp[8:]], axis=0)
        zn = jnp.concatenate([zn[:rows - 8], jnp.where(row8 == 7, n, zn[rows - 8:])], axis=0)
        mu = mu_ref[:, c0:c1]
        return (1.0 - mu) * z + (0.5 * mu) * (zp + zn)

    dr = d_rwkv
    r_o[0] = shifted(0, dr)
    k = shifted(dr, 2 * dr)
    k_o[0] = k
    v = shifted(2 * dr, 3 * dr)
    v_o[0] = v.astype(BF16)

    kkv = k * kkw_ref[...]
    ones_bd = ones_ref[...]
    for j in range(dr // MXU_DIM):
        sl = slice(j * MXU_DIM, (j + 1) * MXU_DIM)
        x = kkv[:, sl]
        ss = _head_sum(x * x, ones_bd)
        kk_o[0, :, sl] = x * lax.rsqrt(jnp.maximum(ss, 1e-24))

    c = 3 * dr
    sm = shifted(c, c + 2 * DECAY_LORA + 2 * AAA_LORA + GATE_LORA)
    o = 0
    for d, lw_o in enumerate((lwf_o, lwb_o)):
        wd = jnp.tanh(sm[:, o:o + DECAY_LORA])
        o += DECAY_LORA
        xw = w0_ref[d:d + 1, :] + _mm(wd, w2_ref[d], passes=P_LORA_DECAY)
        u = -xw
        softplus = jnp.maximum(u, 0.0) + jnp.log1p(jnp.exp(-jnp.abs(u)))
        lw_o[0] = -jnp.exp(-softplus - 0.5)
    for d, a_o in enumerate((af_o, ab_o)):
        ad = sm[:, o:o + AAA_LORA]
        o += AAA_LORA
        a_o[0] = _sigmoid(a0_ref[d:d + 1, :] + _mm(ad, a2_ref[d], passes=P_LORA))
    gd = _sigmoid(sm[:, o:o + GATE_LORA])
    g_o[0] = _mm(gd, g2_ref[...], passes=P_LORA)

    for j in range(dr // MXU_DIM):
        sl = slice(j * MXU_DIM, (j + 1) * MXU_DIM)
        ka = ka_ref[:, sl]
        kj = k_o[0, :, sl]
        kd_sum = kj * (1.0 + (af_o[0, :, sl] - 1.0) * ka) + kj * (1.0 + (ab_o[0, :, sl] - 1.0) * ka)
        bonus = _head_sum(r_o[0, :, sl] * kd_sum * rk_ref[:, sl], ones_bd) * v[:, sl]
        bg_o[0, :, sl] = bonus * g_o[0, :, sl]


def _prep(proj3, mu_p, w0, w2, a0, a2, g2, k_k, k_a, r_k, ones_bd, d_rwkv, rw_cols_p):
    b, t, _ = proj3.shape
    tp = SEQ_TILE
    n_tiles = t // tp
    sub = tp // 8
    last8 = t // 8 - 1
    out_sd = jax.ShapeDtypeStruct((b, t, d_rwkv), F32)
    o_spec = pl.BlockSpec((1, tp, d_rwkv), lambda i, j: (i, j, 0))

    def full(a):
        nd = a.ndim
        return pl.BlockSpec(a.shape, lambda i, j: (0,) * nd)

    params = (mu_p, w0, w2, a0, a2, g2, k_k.reshape(1, d_rwkv), k_a.reshape(1, d_rwkv),
              r_k.reshape(1, d_rwkv), ones_bd)
    return pl.pallas_call(
        functools.partial(_prep_kernel, n_tiles=n_tiles, d_rwkv=d_rwkv),
        grid=(b, n_tiles),
        in_specs=[pl.BlockSpec((1, tp, rw_cols_p), lambda i, j: (i, j, 0)),
                  pl.BlockSpec((1, 8, rw_cols_p), lambda i, j: (i, jnp.maximum(j * sub - 1, 0), 0)),
                  pl.BlockSpec((1, 8, rw_cols_p), lambda i, j: (i, jnp.minimum((j + 1) * sub, last8), 0)),
                  ] + [full(a) for a in params],
        out_specs=[o_spec] * 10,
        out_shape=[out_sd, out_sd, jax.ShapeDtypeStruct((b, t, d_rwkv), BF16)] + [out_sd] * 7,
        compiler_params=_cparams(("parallel", "arbitrary")),
        name="rwkv_prep",
    )(proj3, proj3, proj3, *params)


def _block_diag_rhs(y, bd):
    yb = y.astype(BF16)
    return jnp.concatenate([yb] * HEADS_PER_GROUP, axis=0) * bd


def _mmc(x, y_bd):
    return _dot(x.astype(BF16), y_bd)


def _scan_masks(rev):
    c = CHUNK
    ti = lax.broadcasted_iota(jnp.int32, (c, c), 0)
    si = lax.broadcasted_iota(jnp.int32, (c, c), 1)
    tri = jnp.where((si >= ti) if rev else (si <= ti), 1.0, 0.0).astype(BF16)
    tc = lax.broadcasted_iota(jnp.int32, (c, MXU_DIM), 0)
    sc = lax.broadcasted_iota(jnp.int32, (c, MXU_DIM), 1) % c
    before = (sc > tc) if rev else (sc < tc)
    upto = (sc >= tc) if rev else (sc <= tc)
    return tri, before, upto


def _scan_chunk_local(out, refs, sl, dir_masks, blk_masks, ka, last_row, bd):
    r_ref, k_ref, v_ref, kk_ref, lw_ref, a_ref = refs
    tri, before, upto = dir_masks
    eye, same16, off32, off64 = blk_masks
    c = CHUNK
    kk = kk_ref[0, sl, :]
    lw = lw_ref[0, sl, :]
    a = a_ref[0, sl, :]
    lw0, lw1, lw2 = _split3(lw)
    cum = _dot(tri, lw0) + (_dot(tri, lw1) + _dot(tri, lw2))
    yield
    e_in = jnp.exp(cum)
    e_inv = jnp.exp(-cum)
    al = kk * jnp.exp(cum - lw)
    rh = r_ref[0, sl, :] * e_in
    be = (kk * a) * e_inv
    kap = (k_ref[0, sl, :] * (1.0 + (a - 1.0) * ka)) * e_inv
    ptot = e_in[last_row:last_row + 1, :]
    ar = jnp.concatenate([al, rh], axis=0).astype(BF16)
    ab = _dot(ar, _block_diag_rhs(be, bd), _NT)
    ak = _dot(ar, _block_diag_rhs(kap, bd), _NT)
    yield
    lm = jnp.where(before, ab[:c], 0.0)
    aak = jnp.where(before, ak[:c], 0.0)
    kq = jnp.where(upto, ak[c:], 0.0)
    out["bq"] = jnp.where(upto, ab[c:], 0.0).astype(BF16)
    ld = jnp.where(same16, lm, 0.0)
    p = eye - ld
    ld_bd = _block_diag_rhs(ld, bd)
    l2 = _mmc(ld, ld_bd)
    yield
    pl2 = _mmc(jnp.concatenate([p, l2], axis=0), _block_diag_rhs(l2, bd))
    yield
    p = p + pl2[:c]
    l4 = pl2[c:]
    pl4 = _mmc(jnp.concatenate([p, l4], axis=0), _block_diag_rhs(l4, bd))
    yield
    p = p + pl4[:c]
    p8 = _mmc(p, _block_diag_rhs(pl4[c:], bd))
    v = v_ref[0, sl, :]
    akv = _mmc(jnp.concatenate([aak, kq], axis=0), _block_diag_rhs(v, bd))
    aak_v = akv[:c]
    out["kq_v"] = akv[c:]
    yield
    p = p + p8
    for off in (off32, off64):
        lo = jnp.where(off, lm, 0.0)
        t1 = _mmc(lo, _block_diag_rhs(p, bd))
        yield
        t2 = _mmc(p, _block_diag_rhs(t1, bd))
        yield
        p = p - t2
    wt = _mmc(p, _block_diag_rhs(al, bd))
    out["ut"] = _mmc(p, _block_diag_rhs(aak_v, bd))
    yield
    out["wr"] = jnp.concatenate([wt, rh], axis=0).astype(BF16)
    out["v"] = v.astype(BF16)
    out["bkT"] = jnp.transpose(jnp.concatenate([be, kap], axis=0)).astype(BF16)
    pc = jnp.transpose(jnp.broadcast_to(ptot, (8, MXU_DIM)))
    out["pcol"] = jnp.broadcast_to(pc[:, 0:1], (MXU_DIM, MXU_DIM))


def _scan_chain(items, m_scr, y_ref, bd, bd_mask):
    c = CHUNK
    for loc, sl in items:
        m = m_scr[...]
        zs = _dot(loc["wr"], m.astype(BF16))
        yield
        u = -(zs[:c] + loc["ut"])
        uv = jnp.concatenate([u.astype(BF16), loc["v"]], axis=0)
        dm = _dot(loc["bkT"], uv)
        yu = _mmc(loc["bq"], _block_diag_rhs(u, bd))
        yield
        y_ref[0, sl, :] = zs[c:] + loc["kq_v"] + yu
        m_scr[...] = loc["pcol"] * (m + jnp.where(bd_mask, dm, 0.0))


def _run_lockstep(gens):
    gens = list(gens)
    while gens:
        alive = []
        for g in gens:
            try:
                next(g)
                alive.append(g)
            except StopIteration:
                pass
        gens = alive


def _scan_kernel(rf_ref, kf_ref, vf_ref, kkf_ref, lwf_ref, af_ref, rb_ref, kb_ref, vb_ref, kkb_ref, lwb_ref,
                 ab_ref, ka_ref, m0f_ref, m0b_ref, yf_ref, yb_ref, mff_ref, mfb_ref, mf_scr, mb_scr,
                 *, n_sub, n_blk):
    c = CHUNK
    blk = pl.program_id(2)

    @pl.when(blk == 0)
    def _():
        mf_scr[...] = m0f_ref[0, 0]
        mb_scr[...] = m0b_ref[0, 0]

    tc = lax.broadcasted_iota(jnp.int32, (c, MXU_DIM), 0)
    sc = lax.broadcasted_iota(jnp.int32, (c, MXU_DIM), 1) % c
    same16 = (sc // 16) == (tc // 16)
    same32 = (sc // 32) == (tc // 32)
    blk_masks = (jnp.where(sc == tc, 1.0, 0.0), same16,
                 jnp.logical_and(same32, jnp.logical_not(same16)), jnp.logical_not(same32))
    bd_mask = (lax.broadcasted_iota(jnp.int32, (MXU_DIM, MXU_DIM), 0) // HEAD) == \
              (lax.broadcasted_iota(jnp.int32, (MXU_DIM, MXU_DIM), 1) // HEAD)
    bd = jnp.where(bd_mask, 1.0, 0.0).astype(BF16)
    ka = ka_ref[...]

    refs_f = (rf_ref, kf_ref, vf_ref, kkf_ref, lwf_ref, af_ref)
    refs_b = (rb_ref, kb_ref, vb_ref, kkb_ref, lwb_ref, ab_ref)
    masks_f = _scan_masks(False)
    masks_b = _scan_masks(True)
    slices = [slice(j * c, (j + 1) * c) for j in range(n_sub)]
    items_f = [({}, sl) for sl in slices]
    items_b = [({}, sl) for sl in reversed(slices)]
    chains = []
    for w0 in range(0, n_sub, SCAN_WAVE):
        wave_f = items_f[w0:w0 + SCAN_WAVE]
        wave_b = items_b[w0:w0 + SCAN_WAVE]
        gens = []
        for (of, sf), (ob, sb) in zip(wave_f, wave_b):
            gens.append(_scan_chunk_local(of, refs_f, sf, masks_f, blk_masks, ka, c - 1, bd))
            gens.append(_scan_chunk_local(ob, refs_b, sb, masks_b, blk_masks, ka, 0, bd))
        _run_lockstep(gens + chains)
        chains = [_scan_chain(wave_f, mf_scr, yf_ref, bd, bd_mask),
                  _scan_chain(wave_b, mb_scr, yb_ref, bd, bd_mask)]
    _run_lockstep(chains)

    @pl.when(blk == n_blk - 1)
    def _():
        mff_ref[0, 0] = mf_scr[...]
        mfb_ref[0, 0] = mb_scr[...]


def _scan(r, k, v, kk, lwf, af, lwb, ab, k_a, m0f, m0b):
    b, t, d_rwkv = r.shape
    groups = d_rwkv // MXU_DIM
    tb = min(SCAN_BLOCK, t)
    n_blk = t // tb
    n_sub = tb // CHUNK
    tok_f =pl.BlockSpec((1, tb, MXU_DIM), lambda i, g, j: (i, j, g))
    tok_b = pl.BlockSpec((1, tb, MXU_DIM), lambda i, g, j: (i, n_blk - 1 - j, g))
    st = pl.BlockSpec((1, 1, MXU_DIM, MXU_DIM), lambda i, g, j: (i, g, 0, 0))
    y_sd = jax.ShapeDtypeStruct((b, t, d_rwkv), F32)
    m_sd = jax.ShapeDtypeStruct((b, groups, MXU_DIM, MXU_DIM), F32)
    return pl.pallas_call(
        functools.partial(_scan_kernel, n_sub=n_sub, n_blk=n_blk),
        grid=(b, groups, n_blk),
        in_specs=[tok_f] * 6 + [tok_b] * 6 + [pl.BlockSpec((1, MXU_DIM), lambda i, g, j: (0, g)), st, st],
        out_specs=[tok_f, tok_b, st, st],
        out_shape=[y_sd, y_sd, m_sd, m_sd],
        scratch_shapes=[pltpu.VMEM((MXU_DIM, MXU_DIM), F32), pltpu.VMEM((MXU_DIM, MXU_DIM), F32)],
        compiler_params=_cparams(("parallel", "parallel", "arbitrary")),
        name="scan",
    )(r, k, v, kk, lwf, af, r, k, v, kk, lwb, ab, k_a.reshape(1, d_rwkv), m0f, m0b)


def _to_block_diag(s):
    b, h, n, _ = s.shape
    g = h // HEADS_PER_GROUP
    s5 = s.reshape(b, g, HEADS_PER_GROUP, n, n)
    eye = jnp.eye(HEADS_PER_GROUP, dtype=s.dtype)
    out = s5[:, :, :, :, None, :] * eye[None, None, :, None, :, None]
    return out.reshape(b, g, HEADS_PER_GROUP * n, HEADS_PER_GROUP * n)


def _from_block_diag(sbd):
    b, g, m, _ = sbd.shape
    n = m // HEADS_PER_GROUP
    s6 = sbd.reshape(b, g, HEADS_PER_GROUP, n, HEADS_PER_GROUP, n)
    idx = jnp.arange(HEADS_PER_GROUP)
    diag = s6[:, :, idx, :, idx, :]
    return jnp.moveaxis(diag, 0, 2).reshape(b, g * HEADS_PER_GROUP, n, n)


def _post_kernel(yf_ref, yb_ref, g_ref, bg_ref, lg_ref, lb_ref, ones_ref, o_ref):
    ones_bd = ones_ref[...]
    inv_n = 1.0 / HEAD
    y = yf_ref[0] + yb_ref[0]
    mu = _head_sum(y, ones_bd) * inv_n
    d = y - mu
    var = _head_sum(d * d, ones_bd) * inv_n
    yn = d * lax.rsqrt(var + GN_EPS) * lg_ref[...] + lb_ref[...]
    o_ref[0] = (yn * g_ref[0] + bg_ref[0]).astype(BF16)


def _post(yf, yb, g, bg, lnx_g, lnx_b, ones_bd):
    b, t, d_rwkv = yf.shape
    tp = min(512, t)
    tok = pl.BlockSpec((1, tp, MXU_DIM), lambda i, j, q: (i, j, q))
    par = pl.BlockSpec((1, MXU_DIM), lambda i, j, q: (0, q))
    return pl.pallas_call(
        _post_kernel,
        grid=(b, t // tp, d_rwkv // MXU_DIM),
        in_specs=[tok] * 4 + [par] * 2 + [pl.BlockSpec((MXU_DIM, MXU_DIM), lambda i, j, q: (0, 0))],
        out_specs=tok,
        out_shape=jax.ShapeDtypeStruct((b, t, d_rwkv), BF16),
        compiler_params=_cparams(("parallel", "parallel", "arbitrary")),
        name="rwkv_post",
    )(yf, yb, g, bg, lnx_g.reshape(1, d_rwkv), lnx_b.reshape(1, d_rwkv), ones_bd)


def _window_bounds(n, w):
    t = np.arange(n)
    lo = np.clip(t - w // 2, 0, n)
    hi = np.clip(t - w // 2 + w, 0, n)
    return lo, hi


def _window_matrix(n, w):
    lo, hi = _window_bounds(n, w)
    s = np.arange(n)[None, :]
    m = ((s >= lo[:, None]) & (s < hi[:, None])).astype(np.float32)
    return m / (hi - lo)[:, None].astype(np.float32)


def _pool_ctx_kernel(z_ref, band_ref, pw_ref, ps_ref, o_ref):
    z = z_ref[0]
    m = _dot(band_ref[0], z, precision=lax.Precision.HIGHEST)
    d = (m - z).astype(BF16)
    o_ref[0] = (_dot(d, pw_ref[0]) * ps_ref[...]).astype(BF16)


def _pool_ctx(proj3, col0, pool_w_bf16, pool_scale):
    b, t, _ = proj3.shape
    ng, pg, _ = pool_w_bf16.shape
    band = jnp.asarray(np.stack([_window_matrix(t, w) for w in POOL_WINDOWS]))
    cb = col0 // pg
    return pl.pallas_call(
        _pool_ctx_kernel,
        grid=(b, ng),
        in_specs=[pl.BlockSpec((1, t, pg), lambda i, q: (i, 0, cb + q)),
                  pl.BlockSpec((1, t, t), lambda i, q: (q, 0, 0)),
                  pl.BlockSpec((1, pg, pg), lambda i, q: (q, 0, 0)),
                  pl.BlockSpec((1, pg), lambda i, q: (0, q))],
        out_specs=pl.BlockSpec((1, t, pg), lambda i, q: (i, 0, q)),
        out_shape=jax.ShapeDtypeStruct((b, t, ng * pg), BF16),
        compiler_params=_cparams(("parallel", "arbitrary")),
        name="pool_ctx",
    )(proj3, band, pool_w_bf16, pool_scale.reshape(1, ng * pg))


def _pool_lat_kernel(z_ref, band_ref, pw_ref, ps_ref, o_ref, m1_ref, cs_ref, *, n_rows):
    tile = band_ref.shape[1]
    n_tok = n_rows * GRID_W
    band = band_ref[0]
    for i in range(n_tok // tile):
        sl = slice(i * tile, (i + 1) * tile)
        m1_ref[sl, :] = _dot(band, z_ref[0, sl, :], precision=lax.Precision.HIGHEST)
    cs_ref[0:GRID_W, :] = jnp.zeros((GRID_W, cs_ref.shape[1]), F32)
    for rr in range(n_rows):
        cs_ref[(rr + 1) * GRID_W:(rr + 2) * GRID_W, :] = (
            cs_ref[rr * GRID_W:(rr + 1) * GRID_W, :] + m1_ref[rr * GRID_W:(rr + 1) * GRID_W, :])
    group = pl.program_id(1)
    for gi, w in enumerate(POOL_WINDOWS):
        @pl.when(group == gi)
        def _(w=w):
            lo, hi = _window_bounds(n_rows, w)
            for rr in range(n_rows):
                l, h = int(lo[rr]), int(hi[rr])
                m2 = (cs_ref[h * GRID_W:(h + 1) * GRID_W, :]
                      - cs_ref[l * GRID_W:(l + 1) * GRID_W, :]) / float(h - l)
                sl = slice(rr * GRID_W, (rr + 1) * GRID_W)
                m1_ref[sl, :] = m2 - z_ref[0, sl, :]
    pw = pw_ref[0]
    ps = ps_ref[...]
    for i in range(n_tok // tile):
        sl = slice(i * tile, (i + 1) * tile)
        o_ref[0, sl, :] = (_dot(m1_ref[sl, :].astype(BF16), pw) * ps).astype(BF16)


def _pool_lat(proj3, col0, pool_w_bf16, pool_scale):
    b, t, _ = proj3.shape
    ng, pg, _ = pool_w_bf16.shape
    n_rows = t // GRID_W
    tile = MXU_DIM
    band = jnp.asarray(np.stack(
        [np.kron(np.eye(tile // GRID_W, dtype=np.float32), _window_matrix(GRID_W, w)) for w in POOL_WINDOWS]))
    cb = col0 // pg
    return pl.pallas_call(
        functools.partial(_pool_lat_kernel, n_rows=n_rows),
        grid=(b, ng),
        in_specs=[pl.BlockSpec((1, t, pg), lambda i, q: (i, 0, cb + q)),
                  pl.BlockSpec((1, tile, tile), lambda i, q: (q, 0, 0)),
                  pl.BlockSpec((1, pg, pg), lambda i, q: (q, 0, 0)),
                  pl.BlockSpec((1, pg), lambda i, q: (0, q))],
        out_specs=pl.BlockSpec((1, t, pg), lambda i, q: (i, 0, q)),
        out_shape=jax.ShapeDtypeStruct((b, t, ng * pg), BF16),
        scratch_shapes=[pltpu.VMEM((t, pg), F32), pltpu.VMEM((t + GRID_W, pg), F32)],
        compiler_params=_cparams(("parallel", "arbitrary")),
        name="pool_lat",
    )(proj3, band, pool_w_bf16, pool_scale.reshape(1, ng * pg))


ROW_SPLIT = 2


def _resident(shape):
    nd = len(shape)
    return pl.BlockSpec(shape, lambda i: (0,) * nd, pipeline_mode=pl.Buffered(1))


def _up_out_kernel(ya_ref, yb_ref, ga_ref, gb_ref, x_ref, mod_ref, g_ref, wa_ref, wb_ref, wo_ref,
                   x1_ref, h2_ref):
    rows = x_ref.shape[0] // ROW_SPLIT
    for part in range(ROW_SPLIT):
        sl = slice(part * rows, (part + 1) * rows)
        ua = _dot(ya_ref[sl, :], wa_ref[...])
        ub = _dot(yb_ref[sl, :], wb_ref[...])
        merged = (_sigmoid(ga_ref[sl, :]) * ua + _sigmoid(gb_ref[sl, :]) * ub).astype(BF16)
        x1 = x_ref[sl, :] + mod_ref[0, 2:3, :] * _dot(merged, wo_ref[...])
        x1_ref[sl, :] = x1
        y = x1 * lax.rsqrt(jnp.mean(x1 * x1, axis=-1, keepdims=True) + NORM_EPS)
        h2 = (y * g_ref[...]) * (1.0 + mod_ref[0, 4:5, :]) + mod_ref[0, 3:4, :]
        h2_ref[sl, :] = h2.astype(BF16)


def _up_out(ya2, yb2, gates, wa, wb, w_out, x2, mod3, tiles_per_mod_tokens, norm2_g, tm=256):
    m, d = x2.shape
    tm = min(tm, m)
    ka, kb = ya2.shape[1], yb2.shape[1]
    tpm = tiles_per_mod_tokens // tm
    row = lambda width, col: pl.BlockSpec((tm, width), lambda i: (i, col))
    return pl.pallas_call(
        _up_out_kernel,
        grid=(m // tm,),
        in_specs=[row(ka, 0), row(kb, 0), row(d, 0), row(d, 1), row(d, 0),
                  pl.BlockSpec((1, 6, d), lambda i: (i // tpm, 0, 0)),
                  _resident((1, d)), _resident((ka, d)), _resident((kb, d)), _resident((d, d))],
        out_specs=[row(d, 0), row(d, 0)],
        out_shape=[jax.ShapeDtypeStruct((m, d), F32), jax.ShapeDtypeStruct((m, d), BF16)],
        compiler_params=_cparams(("parallel",)),
        name="up_out",
    )(ya2, yb2, gates, gates, x2, mod3, norm2_g.reshape(1, d), wa, wb, w_out)


def _ffn1_kernel(h_ref, w1_ref, w3_ref, o_ref):
    h = h_ref[...]
    u1 = _dot(h, w1_ref[...])
    u3 = _dot(h, w3_ref[...])
    o_ref[...] = (u1 * _sigmoid(u1) * u3).astype(BF16)


def _ffn1(h2, w13, d_ff, tm=1024, tn=512):
    m, d = h2.shape
    tm = min(tm, m)
    nj = d_ff // tn
    return pl.pallas_call(
        _ffn1_kernel,
        grid=(m // tm, nj),
        in_specs=[pl.BlockSpec((tm, d), lambda i, j: (i, 0)),
                  pl.BlockSpec((d, tn), lambda i, j: (0, j)),
                  pl.BlockSpec((d, tn), lambda i, j: (0, nj + j))],
        out_specs=pl.BlockSpec((tm, tn), lambda i, j: (i, j)),
        out_shape=jax.ShapeDtypeStruct((m, d_ff), BF16),
        compiler_params=_cparams(("parallel", "arbitrary")),
        name="ffn1",
    )(h2, w13, w13)


def _ffn2_kernel(a_ref, x_ref, mod_ref, g_ref, w_ref, o_ref):
    rows = x_ref.shape[0] // ROW_SPLIT
    for part in range(ROW_SPLIT):
        sl = slice(part * rows, (part + 1) * rows)
        x2 = x_ref[sl, :] + mod_ref[0, 5:6, :] * _dot(a_ref[sl, :], w_ref[...])
        y = x2 * lax.rsqrt(jnp.mean(x2 * x2, axis=-1, keepdims=True) + NORM_EPS)
        o_ref[sl, :] = y * g_ref[...]


def _ffn2(act, w2, x1, mod3, tiles_per_mod_tokens, final_g, tm=256):
    m, d = x1.shape
    tm = min(tm, m)
    d_ff = act.shape[1]
    tpm = tiles_per_mod_tokens // tm
    return pl.pallas_call(
        _ffn2_kernel,
        grid=(m // tm,),
        in_specs=[pl.BlockSpec((tm, d_ff), lambda i: (i, 0)),
                  pl.BlockSpec((tm, d), lambda i: (i, 0)),
                  pl.BlockSpec((1, 6, d), lambda i: (i // tpm, 0, 0)),
                  _resident((1, d)), _resident((d_ff, d))],
        out_specs=pl.BlockSpec((tm, d), lambda i: (i, 0)),
        out_shape=jax.ShapeDtypeStruct((m, d), F32),
        compiler_params=_cparams(("parallel",)),
        name="ffn2",
    )(act, x1, mod3, final_g.reshape(1, d), w2)


def _trunk(x, mod, s0, latent, p):
    b, t, d = x.shape
    m = b * t
    d_rwkv = p["d_rwkv"]
    d_pool = d - d_rwkv
    heads = d_rwkv // HEAD
    x2 = x.reshape(m, d)
    nb = mod.shape[0]
    mod3 = mod.reshape(nb, 6, d)
    tokens_per_mod = t if nb == b else m

    proj, gates = _in_proj(x2, p["norm1_g"], mod3, tokens_per_mod, p["w_in"], p["rw_cols_p"] + d_pool)
    proj3 = proj.reshape(b, t, proj.shape[1])

    r, k, v, kk, lwf, lwb, af, ab, g, bg = _prep(
        proj3, p["mu"], p["w0"], p["w2"], p["a0"], p["a2"], p["g2"], p["k_k"], p["k_a"], p["r_k"],
        p["ones_bd"], d_rwkv, p["rw_cols_p"])

    if s0 is None:
        s0 = jnp.zeros((b, 2, heads, HEAD, HEAD), F32)
    m0 = jnp.swapaxes(s0, -1, -2)
    yf, yb, mf, mb = _scan(r, k, v, kk, lwf, af, lwb, ab, p["k_a"],
                           _to_block_diag(m0[:, 0]), _to_block_diag(m0[:, 1]))
    s_fin = jnp.swapaxes(jnp.stack([_from_block_diag(mf), _from_block_diag(mb)], axis=1), -1, -2)
    ya = _post(yf, yb, g, bg, p["lnx_g"], p["lnx_b"], p["ones_bd"])

    pool_col0 = p["rw_cols_p"]
    if latent:
        yb_pool = _pool_lat(proj3, pool_col0, p["pool_w"], p["pool_scale"])
    else:
        yb_pool = _pool_ctx(proj3, pool_col0, p["pool_w"], p["pool_scale"])

    x1, h2 = _up_out(ya.reshape(m, d_rwkv), yb_pool.reshape(m, d_pool), gates, p["w_up_a"], p["w_up_b"],
                     p["w_out"], x2, mod3, tokens_per_mod, p["norm2_g"])
    act = _ffn1(h2, p["ffn_w13"], p["d_ff"])
    y = _ffn2(act, p["ffn_w2"], x1, mod3, tokens_per_mod, p["final_g"])
    return y.reshape(b, t, d), s_fin


def kernel(x_prompt, x_sample, c, state_rwkv, c_ctx, w_mod, b_mod, norm1_g, w_in, shift_mu, w0, w2, a0, a2, g2,
           k_k, k_a, r_k, lnx_g, lnx_b, w_up_a, pool_w, pool_scale, w_up_b, w_out, norm2_g, ffn_w13, ffn_w2,
           final_g):
    depth = w_in.shape[0]
    assert depth == 1, "final norm is fused into the last layer: single-layer trunk only"
    d = x_prompt.shape[-1]
    d_rwkv = w_up_a.shape[1]
    rw_cols = shift_mu.shape[1]
    rw_cols_p = -(-rw_cols // 512) * 512
    assert x_prompt.shape[1] == SEQ_TILE and x_sample.shape[1] % SEQ_TILE == 0

    l = 0
    w_in_l = w_in[l].astype(BF16)
    w_in_p = jnp.concatenate(
        [w_in_l[:, :rw_cols], jnp.zeros((d, rw_cols_p - rw_cols), BF16), w_in_l[:, rw_cols:]], axis=1)
    ones_bd = jnp.asarray(np.kron(np.eye(HEADS_PER_GROUP, dtype=np.float32),
                                  np.ones((HEAD, HEAD), np.float32))).astype(BF16)
    p = {
        "d_rwkv": d_rwkv, "rw_cols_p": rw_cols_p, "d_ff": ffn_w2.shape[1],
        "norm1_g": norm1_g[l], "w_in": w_in_p,
        "mu": jnp.pad(shift_mu[l], (0, rw_cols_p - rw_cols)).reshape(1, rw_cols_p),
        "w0": w0[l], "w2": w2[l], "a0": a0[l], "a2": a2[l], "g2": g2[l],
        "k_k": k_k[l], "k_a": k_a[l], "r_k": r_k[l].reshape(-1), "lnx_g": lnx_g[l], "lnx_b": lnx_b[l],
        "ones_bd": ones_bd,
        "w_up_a": w_up_a[l].astype(BF16), "w_up_b": w_up_b[l].astype(BF16),
        "pool_w": pool_w[l].astype(BF16), "pool_scale": pool_scale[l],
        "w_out": w_out[l].astype(BF16), "norm2_g": norm2_g[l],
        "ffn_w13": ffn_w13[l].astype(BF16), "ffn_w2": ffn_w2[l].astype(BF16), "final_g": final_g,
    }
    cond = jnp.concatenate([c_ctx[None, :], c], axis=0)
    mod = _mod(cond, w_mod[l], b_mod[l])
    y_prompt, s_ctx = _trunk(x_prompt, mod[:1], None, False, p)
    y_sample, _ = _trunk(x_sample, mod[1:], state_rwkv[:, l], True, p)
    new_state = s_ctx[:, None].astype(x_prompt.dtype)
    return (y_prompt, y_sample, new_state)
```

```python
import functools

import numpy as np
import jax
import jax.numpy as jnp
from jax import lax
from jax.experimental import pallas as pl
from jax.experimental.pallas import tpu as pltpu

F32 = jnp.float32
BF16 = jnp.bfloat16

HEAD = 64
POOL_WINDOWS = (2, 4, 8, 16)
GRID_W = 64
DECAY_LORA = 64
AAA_LORA = 64
GATE_LORA = 128
NORM_EPS = 1e-6
GN_EPS = 64e-5

LANES = 128
MXU_DIM = 256
HEADS_PER_GROUP = MXU_DIM // HEAD
VMEM_LIMIT = 56 * 1024 * 1024

CHUNK = 64
SEQ_TILE = 256
SCAN_BLOCK = 1024
SCAN_WAVE = 4


def _cparams(sem):
    return pltpu.CompilerParams(dimension_semantics=sem, vmem_limit_bytes=VMEM_LIMIT)


def _dot(a, b, dims=(((1,), (0,)), ((), ())), precision=None):
    return lax.dot_general(a, b, dims, precision=precision, preferred_element_type=F32)


_NN = (((1,), (0,)), ((), ()))
_NT = (((1,), (1,)), ((), ()))
_TN = (((0,), (0,)), ((), ()))


def _split2(x):
    hi = x.astype(BF16)
    lo = (x - hi.astype(F32)).astype(BF16)
    return hi, lo


def _split3(x):
    hi = x.astype(BF16)
    r1 = x - hi.astype(F32)
    mid = r1.astype(BF16)
    lo = (r1 - mid.astype(F32)).astype(BF16)
    return hi, mid, lo


def _mm(a, b, dims=_NN, passes=1):
    if passes == 1:
        return _dot(a.astype(BF16), b.astype(BF16), dims)
    if passes == 3:
        ah, al = _split2(a)
        bh, bl = _split2(b)
        return _dot(ah, bh, dims) + (_dot(ah, bl, dims) + _dot(al, bh, dims))
    return _dot(a, b, dims, precision=lax.Precision.HIGHEST)


def _mm_exact_rhs(a, b_bf16):
    a0, a1 = _split2(a)
    return _dot(a0, b_bf16) + _dot(a1, b_bf16)


def _sigmoid(x):
    return 1.0 / (1.0 + jnp.exp(-x))


def _mod_kernel(c_ref, w_ref, b_ref, o_ref):
    c = c_ref[...]
    s = (c * _sigmoid(c)).astype(BF16)
    o_ref[...] = _dot(s, w_ref[...].astype(BF16)) + b_ref[...]


def _mod(cond, w_mod, b_mod):
    nb, d = cond.shape
    rows = -(-nb // 16) * 16
    cond_p = jnp.pad(cond, ((0, rows - nb), (0, 0)))
    n = w_mod.shape[1]
    tn = 1024
    out = pl.pallas_call(
        _mod_kernel,
        grid=(n // tn,),
        in_specs=[pl.BlockSpec((rows, d), lambda j: (0, 0)),
                  pl.BlockSpec((d, tn), lambda j: (0, j)),
                  pl.BlockSpec((1, tn), lambda j: (0, j))],
        out_specs=pl.BlockSpec((rows, tn), lambda j: (0, j)),
        out_shape=jax.ShapeDtypeStruct((rows, n), F32),
        compiler_params=_cparams(("arbitrary",)),
        name="mod",
    )(cond_p, w_mod, b_mod.reshape(1, n))
    return out[:nb]


NORM_ROWS = 16


def _in_kernel(x_ref, g_ref, mod_ref, w_ref, om_ref, og_ref, h_ref, *, nj_main):
    j = pl.program_id(1)

    @pl.when(j == 0)
    def _():
        gain = g_ref[...]
        scale = 1.0 + mod_ref[0, 1:2, :]
        shift = mod_ref[0, 0:1, :]

        def norm_rows(i, carry):
            rows = pl.ds(pl.multiple_of(i * NORM_ROWS, NORM_ROWS), NORM_ROWS)
            x = x_ref[rows, :]
            y = x * lax.rsqrt(jnp.mean(x * x, axis=-1, keepdims=True) + NORM_EPS)
            h_ref[rows, :] = ((y * gain) * scale + shift).astype(BF16)
            return carry

        lax.fori_loop(0, x_ref.shape[0] // NORM_ROWS, norm_rows, 0, unroll=8)

    @pl.when(j < nj_main)
    def _():
        om_ref[...] = _dot(h_ref[...], w_ref[...])

    @pl.when(j >= nj_main)
    def _():
        og_ref[...] = _dot(h_ref[...], w_ref[...])


def _in_proj(x2, norm_g, mod3, tiles_per_mod_tokens, w_bf16, n_main, tm=1024, tn=512):
    m, d = x2.shape
    tm = min(tm, m)
    n = w_bf16.shape[1]
    tpm = tiles_per_mod_tokens // tm
    nj_main = n_main // tn
    return pl.pallas_call(
        functools.partial(_in_kernel, nj_main=nj_main),
        grid=(m // tm, n // tn),
        in_specs=[pl.BlockSpec((tm, d), lambda i, j: (i, 0)),
                  pl.BlockSpec((1, d), lambda i, j: (0, 0)),
                  pl.BlockSpec((1, 6, d), lambda i, j: (i // tpm, 0, 0)),
                  pl.BlockSpec((d, tn), lambda i, j: (0, j))],
        out_specs=[pl.BlockSpec((tm, tn), lambda i, j: (i, jnp.minimum(j, nj_main - 1))),
                   pl.BlockSpec((tm, tn), lambda i, j: (i, jnp.maximum(j - nj_main, 0)))],
        out_shape=[jax.ShapeDtypeStruct((m, n_main), F32), jax.ShapeDtypeStruct((m, n - n_main), F32)],
        scratch_shapes=[pltpu.VMEM((tm, d), BF16)],
        compiler_params=_cparams(("parallel", "arbitrary")),
        name="in_proj",
    )(x2, norm_g.reshape(1, d), mod3, w_bf16)


P_LORA_DECAY = 3
P_LORA = 1


def _head_sum(x, ones_bd):
    return _mm_exact_rhs(x, ones_bd)


def _prep_kernel(main_ref, prev_ref, next_ref, mu_ref, w0_ref, w2_ref, a0_ref, a2_ref, g2_ref,
                 kkw_ref, ka_ref, rk_ref, ones_ref, r_o, k_o, v_o, kk_o, lwf_o, lwb_o, af_o, ab_o, g_o, bg_o,
                 *, n_tiles, d_rwkv):
    t = pl.program_id(1)
    rows = main_ref.shape[1]
    row8 = lax.broadcasted_iota(jnp.int32, (8, 1), 0)
    has_prev = t > 0
    has_next = t < n_tiles - 1

    def shifted(c0, c1):
        z = main_ref[0, :, c0:c1]
        p = jnp.where(has_prev, prev_ref[0, 7:8, c0:c1], 0.0)
        n = jnp.where(has_next, next_ref[0, 0:1, c0:c1], 0.0)
        zp = pltpu.roll(z, 1, 0)
        zn = pltpu.roll(z, rows - 1, 0)
        zp = jnp.concatenate([jnp.where(row8 == 0, p, zp[:8]), zp[8:]], axis=0)
        zn = jnp.concatenate([zn[:rows - 8], jnp.where(row8 == 7, n, zn[rows - 8:])], axis=0)
        mu = mu_ref[:, c0:c1]
        return (1.0 - mu) * z + (0.5 * mu) * (zp + zn)

    dr = d_rwkv
    r_o[0] = shifted(0, dr)
    k = shifted(dr, 2 * dr)
    k_o[0] = k
    v = shifted(2 * dr, 3 * dr)
    v_o[0] = v.astype(BF16)

    kkv = k * kkw_ref[...]
    ones_bd = ones_ref[...]
    for j in range(dr // MXU_DIM):
        sl = slice(j * MXU_DIM, (j + 1) * MXU_DIM)
        x = kkv[:, sl]
        ss = _head_sum(x * x, ones_bd)
        kk_o[0, :, sl] = x * lax.rsqrt(jnp.maximum(ss, 1e-24))

    c = 3 * dr
    sm = shifted(c, c + 2 * DECAY_LORA + 2 * AAA_LORA + GATE_LORA)
    o = 0
    for d, lw_o in enumerate((lwf_o, lwb_o)):
        wd = jnp.tanh(sm[:, o:o + DECAY_LORA])
        o += DECAY_LORA
        xw = w0_ref[d:d + 1, :] + _mm(wd, w2_ref[d], passes=P_LORA_DECAY)
        u = -xw
        softplus = jnp.maximum(u, 0.0) + jnp.log1p(jnp.exp(-jnp.abs(u)))
        lw_o[0] = -jnp.exp(-softplus - 0.5)
    for d, a_o in enumerate((af_o, ab_o)):
        ad = sm[:, o:o + AAA_LORA]
        o += AAA_LORA
        a_o[0] = _sigmoid(a0_ref[d:d + 1, :] + _mm(ad, a2_ref[d], passes=P_LORA))
    gd = _sigmoid(sm[:, o:o + GATE_LORA])
    g_o[0] = _mm(gd, g2_ref[...], passes=P_LORA)

    for j in range(dr // MXU_DIM):
        sl = slice(j * MXU_DIM, (j + 1) * MXU_DIM)
        ka = ka_ref[:, sl]
        kj = k_o[0, :, sl]
        kd_sum = kj * (1.0 + (af_o[0, :, sl] - 1.0) * ka) + kj * (1.0 + (ab_o[0, :, sl] - 1.0) * ka)
        bonus = _head_sum(r_o[0, :, sl] * kd_sum * rk_ref[:, sl], ones_bd) * v[:, sl]
        bg_o[0, :, sl] = bonus * g_o[0, :, sl]


def _prep(proj3, mu_p, w0, w2, a0, a2, g2, k_k, k_a, r_k, ones_bd, d_rwkv, rw_cols_p):
    b, t, _ = proj3.shape
    tp = SEQ_TILE
    n_tiles = t // tp
    sub = tp // 8
    last8 = t // 8 - 1
    out_sd = jax.ShapeDtypeStruct((b, t, d_rwkv), F32)
    o_spec = pl.BlockSpec((1, tp, d_rwkv), lambda i, j: (i, j, 0))

    def full(a):
        nd = a.ndim
        return pl.BlockSpec(a.shape, lambda i, j: (0,) * nd)

    params = (mu_p, w0, w2, a0, a2, g2, k_k.reshape(1, d_rwkv), k_a.reshape(1, d_rwkv),
              r_k.reshape(1, d_rwkv), ones_bd)
    return pl.pallas_call(
        functools.partial(_prep_kernel, n_tiles=n_tiles, d_rwkv=d_rwkv),
        grid=(b, n_tiles),
        in_specs=[pl.BlockSpec((1, tp, rw_cols_p), lambda i, j: (i, j, 0)),
                  pl.BlockSpec((1, 8, rw_cols_p), lambda i, j: (i, jnp.maximum(j * sub - 1, 0), 0)),
                  pl.BlockSpec((1, 8, rw_cols_p), lambda i, j: (i, jnp.minimum((j + 1) * sub, last8), 0)),
                  ] + [full(a) for a in params],
        out_specs=[o_spec] * 10,
        out_shape=[out_sd, out_sd, jax.ShapeDtypeStruct((b, t, d_rwkv), BF16)] + [out_sd] * 7,
        compiler_params=_cparams(("parallel", "arbitrary")),
        name="rwkv_prep",
    )(proj3, proj3, proj3, *params)


def _block_diag_rhs(y, bd):
    yb = y.astype(BF16)
    return jnp.concatenate([yb] * HEADS_PER_GROUP, axis=0) * bd


def _mmc(x, y_bd):
    return _dot(x.astype(BF16), y_bd)


def _scan_masks(rev):
    c = CHUNK
    ti = lax.broadcasted_iota(jnp.int32, (c, c), 0)
    si = lax.broadcasted_iota(jnp.int32, (c, c), 1)
    tri = jnp.where((si >= ti) if rev else (si <= ti), 1.0, 0.0).astype(BF16)
    tc = lax.broadcasted_iota(jnp.int32, (c, MXU_DIM), 0)
    sc = lax.broadcasted_iota(jnp.int32, (c, MXU_DIM), 1) % c
    before = (sc > tc) if rev else (sc < tc)
    upto = (sc >= tc) if rev else (sc <= tc)
    return tri, before, upto


def _scan_chunk_local(out, refs, sl, dir_masks, blk_masks, ka, last_row, bd):
    r_ref, k_ref, v_ref, kk_ref, lw_ref, a_ref = refs
    tri, before, upto = dir_masks
    eye, same16, off32, off64 = blk_masks
    c = CHUNK
    kk = kk_ref[0, sl, :]
    lw = lw_ref[0, sl, :]
    a = a_ref[0, sl, :]
    lw0, lw1, lw2 = _split3(lw)
    cum = _dot(tri, lw0) + (_dot(tri, lw1) + _dot(tri, lw2))
    yield
    e_in = jnp.exp(cum)
    e_inv = jnp.exp(-cum)
    al = kk * jnp.exp(cum - lw)
    rh = r_ref[0, sl, :] * e_in
    be = (kk * a) * e_inv
    kap = (k_ref[0, sl, :] * (1.0 + (a - 1.0) * ka)) * e_inv
    ptot = e_in[last_row:last_row + 1, :]
    ar = jnp.concatenate([al, rh], axis=0).astype(BF16)
    ab = _dot(ar, _block_diag_rhs(be, bd), _NT)
    ak = _dot(ar, _block_diag_rhs(kap, bd), _NT)
    yield
    lm = jnp.where(before, ab[:c], 0.0)
    aak = jnp.where(before, ak[:c], 0.0)
    kq = jnp.where(upto, ak[c:], 0.0)
    out["bq"] = jnp.where(upto, ab[c:], 0.0).astype(BF16)
    ld = jnp.where(same16, lm, 0.0)
    p = eye - ld
    ld_bd = _block_diag_rhs(ld, bd)
    l2 = _mmc(ld, ld_bd)
    yield
    pl2 = _mmc(jnp.concatenate([p, l2], axis=0), _block_diag_rhs(l2, bd))
    yield
    p = p + pl2[:c]
    l4 = pl2[c:]
    pl4 = _mmc(jnp.concatenate([p, l4], axis=0), _block_diag_rhs(l4, bd))
    yield
    p = p + pl4[:c]
    p8 = _mmc(p, _block_diag_rhs(pl4[c:], bd))
    v = v_ref[0, sl, :]
    akv = _mmc(jnp.concatenate([aak, kq], axis=0), _block_diag_rhs(v, bd))
    aak_v = akv[:c]
    out["kq_v"] = akv[c:]
    yield
    p = p + p8
    for off in (off32, off64):
        lo = jnp.where(off, lm, 0.0)
        t1 = _mmc(lo, _block_diag_rhs(p, bd))
        yield
        t2 = _mmc(p, _block_diag_rhs(t1, bd))
        yield
        p = p - t2
    wt = _mmc(p, _block_diag_rhs(al, bd))
    out["ut"] = _mmc(p, _block_diag_rhs(aak_v, bd))
    yield
    out["wr"] = jnp.concatenate([wt, rh], axis=0).astype(BF16)
    out["v"] = v.astype(BF16)
    out["bkT"] = jnp.transpose(jnp.concatenate([be, kap], axis=0)).astype(BF16)
    out["ptot"] = ptot


def _scan_chain(items, m_scr, y_ref, bd, bd_mask):
    c = CHUNK
    for loc, sl in items:
        m = m_scr[...]
        zs = _dot(loc["wr"], m.astype(BF16))
        yield
        u = -(zs[:c] + loc["ut"])
        uv = jnp.concatenate([u.astype(BF16), loc["v"]], axis=0)
        dm = _dot(loc["bkT"], uv)
        yu = _mmc(loc["bq"], _block_diag_rhs(u, bd))
        yield
        y_ref[0, sl, :] = zs[c:] + loc["kq_v"] + yu
        pc = jnp.transpose(jnp.broadcast_to(loc["ptot"], (8, MXU_DIM)))
        m_scr[...] = jnp.broadcast_to(pc[:, 0:1], (MXU_DIM, MXU_DIM)) * (m + jnp.where(bd_mask, dm, 0.0))


def _run_lockstep(gens):
    gens = list(gens)
    while gens:
        alive = []
        for g in gens:
            try:
                next(g)
                alive.append(g)
            except StopIteration:
                pass
        gens = alive


def _scan_kernel(rf_ref, kf_ref, vf_ref, kkf_ref, lwf_ref, af_ref, rb_ref, kb_ref, vb_ref, kkb_ref, lwb_ref,
                 ab_ref, ka_ref, s0_ref, yf_ref, yb_ref, sfin_ref, mf_scr, mb_scr,
                 *, n_sub, n_blk):
    c = CHUNK
    blk = pl.program_id(2)

    @pl.when(blk == 0)
    def _():
        zero = jnp.zeros((HEAD, HEAD), F32)
        for d, scr in enumerate((mf_scr, mb_scr)):
            for h in range(HEADS_PER_GROUP):
                blocks = [zero] * HEADS_PER_GROUP
                blocks[h] = jnp.transpose(s0_ref[0, d, h])
                scr[h * HEAD:(h + 1) * HEAD, :] = jnp.concatenate(blocks, axis=1)

    tc = lax.broadcasted_iota(jnp.int32, (c, MXU_DIM), 0)
    sc = lax.broadcasted_iota(jnp.int32, (c, MXU_DIM), 1) % c
    same16 = (sc // 16) == (tc // 16)
    same32 = (sc // 32) == (tc // 32)
    blk_masks = (jnp.where(sc == tc, 1.0, 0.0), same16,
                 jnp.logical_and(same32, jnp.logical_not(same16)), jnp.logical_not(same32))
    bd_mask = (lax.broadcasted_iota(jnp.int32, (MXU_DIM, MXU_DIM), 0) // HEAD) == \
              (lax.broadcasted_iota(jnp.int32, (MXU_DIM, MXU_DIM), 1) // HEAD)
    bd = jnp.where(bd_mask, 1.0, 0.0).astype(BF16)
    ka = ka_ref[...]

    refs_f = (rf_ref, kf_ref, vf_ref, kkf_ref, lwf_ref, af_ref)
    refs_b = (rb_ref, kb_ref, vb_ref, kkb_ref, lwb_ref, ab_ref)
    masks_f = _scan_masks(False)
    masks_b = _scan_masks(True)
    slices = [slice(j * c, (j + 1) * c) for j in range(n_sub)]
    items_f = [({}, sl) for sl in slices]
    items_b = [({}, sl) for sl in reversed(slices)]
    chains = []
    for w0 in range(0, n_sub, SCAN_WAVE):
        wave_f = items_f[w0:w0 + SCAN_WAVE]
        wave_b = items_b[w0:w0 + SCAN_WAVE]
        gens = []
        for (of, sf), (ob, sb) in zip(wave_f, wave_b):
            gens.append(_scan_chunk_local(of, refs_f, sf, masks_f, blk_masks, ka, c - 1, bd))
            gens.append(_scan_chunk_local(ob, refs_b, sb, masks_b, blk_masks, ka, 0, bd))
        _run_lockstep(gens + chains)
        chains = [_scan_chain(wave_f, mf_scr, yf_ref, bd, bd_mask),
                  _scan_chain(wave_b, mb_scr, yb_ref, bd, bd_mask)]
    _run_lockstep(chains)

    @pl.when(blk == n_blk - 1)
    def _():
        for d, scr in enumerate((mf_scr, mb_scr)):
            for h in range(HEADS_PER_GROUP):
                rows = scr[h * HEAD:(h + 1) * HEAD, :]
                sfin_ref[0, d, h] = jnp.transpose(rows[:, h * HEAD:(h + 1) * HEAD])


def _scan(r, k, v, kk, lwf, af, lwb, ab, k_a, s0):
    b, t, d_rwkv = r.shape
    groups = d_rwkv // MXU_DIM
    tb = min(SCAN_BLOCK, t)
    n_blk = t // tb
    n_sub = tb // CHUNK
    tok_f =pl.BlockSpec((1, tb, MXU_DIM), lambda i, g, j: (i, j, g))
    tok_b = pl.BlockSpec((1, tb, MXU_DIM), lambda i, g, j: (i, n_blk - 1 - j, g))
    st = pl.BlockSpec((1, 2, HEADS_PER_GROUP, HEAD, HEAD), lambda i, g, j: (i, 0, g, 0, 0))
    y_sd = jax.ShapeDtypeStruct((b, t, d_rwkv), F32)
    return pl.pallas_call(
        functools.partial(_scan_kernel, n_sub=n_sub, n_blk=n_blk),
        grid=(b, groups, n_blk),
        in_specs=[tok_f] * 6 + [tok_b] * 6 + [pl.BlockSpec((1, MXU_DIM), lambda i, g, j: (0, g)), st],
        out_specs=[tok_f, tok_b, st],
        out_shape=[y_sd, y_sd, jax.ShapeDtypeStruct(s0.shape, F32)],
        scratch_shapes=[pltpu.VMEM((MXU_DIM, MXU_DIM), F32), pltpu.VMEM((MXU_DIM, MXU_DIM), F32)],
        compiler_params=_cparams(("parallel", "parallel", "arbitrary")),
        name="scan",
    )(r, k, v, kk, lwf, af, r, k, v, kk, lwb, ab, k_a.reshape(1, d_rwkv), s0)


def _post_kernel(yf_ref, yb_ref, g_ref, bg_ref, lg_ref, lb_ref, ones_ref, o_ref):
    ones_bd = ones_ref[...]
    inv_n = 1.0 / HEAD
    y = yf_ref[0] + yb_ref[0]
    mu = _head_sum(y, ones_bd) * inv_n
    d = y - mu
    var = _head_sum(d * d, ones_bd) * inv_n
    yn = d * lax.rsqrt(var + GN_EPS) * lg_ref[...] + lb_ref[...]
    o_ref[0] = (yn * g_ref[0] + bg_ref[0]).astype(BF16)


def _post(yf, yb, g, bg, lnx_g, lnx_b, ones_bd):
    b, t, d_rwkv = yf.shape
    tp = min(512, t)
    tok = pl.BlockSpec((1, tp, MXU_DIM), lambda i, j, q: (i, j, q))
    par = pl.BlockSpec((1, MXU_DIM), lambda i, j, q: (0, q))
    return pl.pallas_call(
        _post_kernel,
        grid=(b, t // tp, d_rwkv // MXU_DIM),
        in_specs=[tok] * 4 + [par] * 2 + [pl.BlockSpec((MXU_DIM, MXU_DIM), lambda i, j, q: (0, 0))],
        out_specs=tok,
        out_shape=jax.ShapeDtypeStruct((b, t, d_rwkv), BF16),
        compiler_params=_cparams(("parallel", "parallel", "arbitrary")),
        name="rwkv_post",
    )(yf, yb, g, bg, lnx_g.reshape(1, d_rwkv), lnx_b.reshape(1, d_rwkv), ones_bd)


def _window_bounds(n, w):
    t = np.arange(n)
    lo = np.clip(t - w // 2, 0, n)
    hi = np.clip(t - w // 2 + w, 0, n)
    return lo, hi


def _window_matrix(n, w):
    lo, hi = _window_bounds(n, w)
    s = np.arange(n)[None, :]
    m = ((s >= lo[:, None]) & (s < hi[:, None])).astype(np.float32)
    return m / (hi - lo)[:, None].astype(np.float32)


def _pool_ctx_kernel(z_ref, band_ref, pw_ref, ps_ref, o_ref):
    z = z_ref[0]
    m = _dot(band_ref[0], z, precision=lax.Precision.HIGHEST)
    d = (m - z).astype(BF16)
    o_ref[0] = (_dot(d, pw_ref[0]) * ps_ref[...]).astype(BF16)


def _pool_ctx(proj3, col0, pool_w_bf16, pool_scale):
    b, t, _ = proj3.shape
    ng, pg, _ = pool_w_bf16.shape
    band = jnp.asarray(np.stack([_window_matrix(t, w) for w in POOL_WINDOWS]))
    cb = col0 // pg
    return pl.pallas_call(
        _pool_ctx_kernel,
        grid=(b, ng),
        in_specs=[pl.BlockSpec((1, t, pg), lambda i, q: (i, 0, cb + q)),
                  pl.BlockSpec((1, t, t), lambda i, q: (q, 0, 0)),
                  pl.BlockSpec((1, pg, pg), lambda i, q: (q, 0, 0)),
                  pl.BlockSpec((1, pg), lambda i, q: (0, q))],
        out_specs=pl.BlockSpec((1, t, pg), lambda i, q: (i, 0, q)),
        out_shape=jax.ShapeDtypeStruct((b, t, ng * pg), BF16),
        compiler_params=_cparams(("parallel", "arbitrary")),
        name="pool_ctx",
    )(proj3, band, pool_w_bf16, pool_scale.reshape(1, ng * pg))


def _pool_lat_kernel(z_ref, band_ref, pw_ref, ps_ref, o_ref, m1_ref, cs_ref, *, n_rows):
    tile = band_ref.shape[1]
    n_tok = n_rows * GRID_W
    band = band_ref[0]
    for i in range(n_tok // tile):
        sl = slice(i * tile, (i + 1) * tile)
        m1_ref[sl, :] = _dot(band, z_ref[0, sl, :], precision=lax.Precision.HIGHEST)
    cs_ref[0:GRID_W, :] = jnp.zeros((GRID_W, cs_ref.shape[1]), F32)
    for rr in range(n_rows):
        cs_ref[(rr + 1) * GRID_W:(rr + 2) * GRID_W, :] = (
            cs_ref[rr * GRID_W:(rr + 1) * GRID_W, :] + m1_ref[rr * GRID_W:(rr + 1) * GRID_W, :])
    group = pl.program_id(1)
    for gi, w in enumerate(POOL_WINDOWS):
        @pl.when(group == gi)
        def _(w=w):
            lo, hi = _window_bounds(n_rows, w)
            for rr in range(n_rows):
                l, h = int(lo[rr]), int(hi[rr])
                m2 = (cs_ref[h * GRID_W:(h + 1) * GRID_W, :]
                      - cs_ref[l * GRID_W:(l + 1) * GRID_W, :]) / float(h - l)
                sl = slice(rr * GRID_W, (rr + 1) * GRID_W)
                m1_ref[sl, :] = m2 - z_ref[0, sl, :]
    pw = pw_ref[0]
    ps = ps_ref[...]
    for i in range(n_tok // tile):
        sl = slice(i * tile, (i + 1) * tile)
        o_ref[0, sl, :] = (_dot(m1_ref[sl, :].astype(BF16), pw) * ps).astype(BF16)


def _pool_lat(proj3, col0, pool_w_bf16, pool_scale):
    b, t, _ = proj3.shape
    ng, pg, _ = pool_w_bf16.shape
    n_rows = t // GRID_W
    tile = MXU_DIM
    band = jnp.asarray(np.stack(
        [np.kron(np.eye(tile // GRID_W, dtype=np.float32), _window_matrix(GRID_W, w)) for w in POOL_WINDOWS]))
    cb = col0 // pg
    return pl.pallas_call(
        functools.partial(_pool_lat_kernel, n_rows=n_rows),
        grid=(b, ng),
        in_specs=[pl.BlockSpec((1, t, pg), lambda i, q: (i, 0, cb + q)),
                  pl.BlockSpec((1, tile, tile), lambda i, q: (q, 0, 0)),
                  pl.BlockSpec((1, pg, pg), lambda i, q: (q, 0, 0)),
                  pl.BlockSpec((1, pg), lambda i, q: (0, q))],
        out_specs=pl.BlockSpec((1, t, pg), lambda i, q: (i, 0, q)),
        out_shape=jax.ShapeDtypeStruct((b, t, ng * pg), BF16),
        scratch_shapes=[pltpu.VMEM((t, pg), F32), pltpu.VMEM((t + GRID_W, pg), F32)],
        compiler_params=_cparams(("parallel", "arbitrary")),
        name="pool_lat",
    )(proj3, band, pool_w_bf16, pool_scale.reshape(1, ng * pg))


ROW_SPLIT = 2


def _resident(shape):
    nd = len(shape)
    return pl.BlockSpec(shape, lambda i: (0,) * nd, pipeline_mode=pl.Buffered(1))


def _up_out_kernel(ya_ref, yb_ref, ga_ref, gb_ref, x_ref, mod_ref, g_ref, wa_ref, wb_ref, wo_ref,
                   x1_ref, h2_ref):
    rows = x_ref.shape[0] // ROW_SPLIT
    for part in range(ROW_SPLIT):
        sl = slice(part * rows, (part + 1) * rows)
        ua = _dot(ya_ref[sl, :], wa_ref[...])
        ub = _dot(yb_ref[sl, :], wb_ref[...])
        merged = (_sigmoid(ga_ref[sl, :]) * ua + _sigmoid(gb_ref[sl, :]) * ub).astype(BF16)
        x1 = x_ref[sl, :] + mod_ref[0, 2:3, :] * _dot(merged, wo_ref[...])
        x1_ref[sl, :] = x1
        y = x1 * lax.rsqrt(jnp.mean(x1 * x1, axis=-1, keepdims=True) + NORM_EPS)
        h2 = (y * g_ref[...]) * (1.0 + mod_ref[0, 4:5, :]) + mod_ref[0, 3:4, :]
        h2_ref[sl, :] = h2.astype(BF16)


def _up_out(ya2, yb2, gates, wa, wb, w_out, x2, mod3, tiles_per_mod_tokens, norm2_g, tm=256):
    m, d = x2.shape
    tm = min(tm, m)
    ka, kb = ya2.shape[1], yb2.shape[1]
    tpm = tiles_per_mod_tokens // tm
    row = lambda width, col: pl.BlockSpec((tm, width), lambda i: (i, col))
    return pl.pallas_call(
        _up_out_kernel,
        grid=(m // tm,),
        in_specs=[row(ka, 0), row(kb, 0), row(d, 0), row(d, 1), row(d, 0),
                  pl.BlockSpec((1, 6, d), lambda i: (i // tpm, 0, 0)),
                  _resident((1, d)), _resident((ka, d)), _resident((kb, d)), _resident((d, d))],
        out_specs=[row(d, 0), row(d, 0)],
        out_shape=[jax.ShapeDtypeStruct((m, d), F32), jax.ShapeDtypeStruct((m, d), BF16)],
        compiler_params=_cparams(("parallel",)),
        name="up_out",
    )(ya2, yb2, gates, gates, x2, mod3, norm2_g.reshape(1, d), wa, wb, w_out)


def _ffn1_kernel(h_ref, w1_ref, w3_ref, o_ref):
    h = h_ref[...]
    u1 = _dot(h, w1_ref[...])
    u3 = _dot(h, w3_ref[...])
    o_ref[...] = (u1 * _sigmoid(u1) * u3).astype(BF16)


def _ffn1(h2, w13, d_ff, tm=1024, tn=512):
    m, d = h2.shape
    tm = min(tm, m)
    nj = d_ff // tn
    return pl.pallas_call(
        _ffn1_kernel,
        grid=(m // tm, nj),
        in_specs=[pl.BlockSpec((tm, d), lambda i, j: (i, 0)),
                  pl.BlockSpec((d, tn), lambda i, j: (0, j)),
                  pl.BlockSpec((d, tn), lambda i, j: (0, nj + j))],
        out_specs=pl.BlockSpec((tm, tn), lambda i, j: (i, j)),
        out_shape=jax.ShapeDtypeStruct((m, d_ff), BF16),
        compiler_params=_cparams(("parallel", "arbitrary")),
        name="ffn1",
    )(h2, w13, w13)


def _ffn2_kernel(a_ref, x_ref, mod_ref, g_ref, w_ref, o_ref):
    rows = x_ref.shape[0] // ROW_SPLIT
    for part in range(ROW_SPLIT):
        sl = slice(part * rows, (part + 1) * rows)
        x2 = x_ref[sl, :] + mod_ref[0, 5:6, :] * _dot(a_ref[sl, :], w_ref[...])
        y = x2 * lax.rsqrt(jnp.mean(x2 * x2, axis=-1, keepdims=True) + NORM_EPS)
        o_ref[sl, :] = y * g_ref[...]


def _ffn2(act, w2, x1, mod3, tiles_per_mod_tokens, final_g, tm=256):
    m, d = x1.shape
    tm = min(tm, m)
    d_ff = act.shape[1]
    tpm = tiles_per_mod_tokens // tm
    return pl.pallas_call(
        _ffn2_kernel,
        grid=(m // tm,),
        in_specs=[pl.BlockSpec((tm, d_ff), lambda i: (i, 0)),
                  pl.BlockSpec((tm, d), lambda i: (i, 0)),
                  pl.BlockSpec((1, 6, d), lambda i: (i // tpm, 0, 0)),
                  _resident((1, d)), _resident((d_ff, d))],
        out_specs=pl.BlockSpec((tm, d), lambda i: (i, 0)),
        out_shape=jax.ShapeDtypeStruct((m, d), F32),
        compiler_params=_cparams(("parallel",)),
        name="ffn2",
    )(act, x1, mod3, final_g.reshape(1, d), w2)


def _trunk(x, mod, s0, latent, p):
    b, t, d = x.shape
    m = b * t
    d_rwkv = p["d_rwkv"]
    d_pool = d - d_rwkv
    heads = d_rwkv // HEAD
    x2 = x.reshape(m, d)
    nb = mod.shape[0]
    mod3 = mod.reshape(nb, 6, d)
    tokens_per_mod = t if nb == b else m

    proj, gates = _in_proj(x2, p["norm1_g"], mod3, tokens_per_mod, p["w_in"], p["rw_cols_p"] + d_pool)
    proj3 = proj.reshape(b, t, proj.shape[1])

    r, k, v, kk, lwf, lwb, af, ab, g, bg = _prep(
        proj3, p["mu"], p["w0"], p["w2"], p["a0"], p["a2"], p["g2"], p["k_k"], p["k_a"], p["r_k"],
        p["ones_bd"], d_rwkv, p["rw_cols_p"])

    if s0 is None:
        s0 = jnp.zeros((b, 2, heads, HEAD, HEAD), F32)
    yf, yb, s_fin = _scan(r, k, v, kk, lwf, af, lwb, ab, p["k_a"], s0)
    ya = _post(yf, yb, g, bg, p["lnx_g"], p["lnx_b"], p["ones_bd"])

    pool_col0 = p["rw_cols_p"]
    if latent:
        yb_pool = _pool_lat(proj3, pool_col0, p["pool_w"], p["pool_scale"])
    else:
        yb_pool = _pool_ctx(proj3, pool_col0, p["pool_w"], p["pool_scale"])

    x1, h2 = _up_out(ya.reshape(m, d_rwkv), yb_pool.reshape(m, d_pool), gates, p["w_up_a"], p["w_up_b"],
                     p["w_out"], x2, mod3, tokens_per_mod, p["norm2_g"])
    act = _ffn1(h2, p["ffn_w13"], p["d_ff"])
    y = _ffn2(act, p["ffn_w2"], x1, mod3, tokens_per_mod, p["final_g"])
    return y.reshape(b, t, d), s_fin


def kernel(x_prompt, x_sample, c, state_rwkv, c_ctx, w_mod, b_mod, norm1_g, w_in, shift_mu, w0, w2, a0, a2, g2,
           k_k, k_a, r_k, lnx_g, lnx_b, w_up_a, pool_w, pool_scale, w_up_b, w_out, norm2_g, ffn_w13, ffn_w2,
           final_g):
    depth = w_in.shape[0]
    assert depth == 1, "final norm is fused into the last layer: single-layer trunk only"
    d = x_prompt.shape[-1]
    d_rwkv = w_up_a.shape[1]
    rw_cols = shift_mu.shape[1]
    rw_cols_p = -(-rw_cols // 512) * 512
    assert x_prompt.shape[1] == SEQ_TILE and x_sample.shape[1] % SEQ_TILE == 0

    l = 0
    w_in_l = w_in[l].astype(BF16)
    w_in_p = jnp.concatenate(
        [w_in_l[:, :rw_cols], jnp.zeros((d, rw_cols_p - rw_cols), BF16), w_in_l[:, rw_cols:]], axis=1)
    ones_bd = jnp.asarray(np.kron(np.eye(HEADS_PER_GROUP, dtype=np.float32),
                                  np.ones((HEAD, HEAD), np.float32))).astype(BF16)
    p = {
        "d_rwkv": d_rwkv, "rw_cols_p": rw_cols_p, "d_ff": ffn_w2.shape[1],
        "norm1_g": norm1_g[l], "w_in": w_in_p,
        "mu": jnp.pad(shift_mu[l], (0, rw_cols_p - rw_cols)).reshape(1, rw_cols_p),
        "w0": w0[l], "w2": w2[l], "a0": a0[l], "a2": a2[l], "g2": g2[l],
        "k_k": k_k[l], "k_a": k_a[l], "r_k": r_k[l].reshape(-1), "lnx_g": lnx_g[l], "lnx_b": lnx_b[l],
        "ones_bd": ones_bd,
        "w_up_a": w_up_a[l].astype(BF16), "w_up_b": w_up_b[l].astype(BF16),
        "pool_w": pool_w[l].astype(BF16), "pool_scale": pool_scale[l],
        "w_out": w_out[l].astype(BF16), "norm2_g": norm2_g[l],
        "ffn_w13": ffn_w13[l].astype(BF16), "ffn_w2": ffn_w2[l].astype(BF16), "final_g": final_g,
    }
    cond = jnp.concatenate([c_ctx[None, :], c], axis=0)
    mod = _mod(cond, w_mod[l], b_mod[l])
    y_prompt, s_ctx = _trunk(x_prompt, mod[:1], None, False, p)
    y_sample, _ = _trunk(x_sample, mod[1:], state_rwkv[:, l], True, p)
    new_state = s_ctx[:, None].astype(x_prompt.dtype)
    return (y_prompt, y_sample, new_state)
```

```python
import functools

import numpy as np
import jax
import jax.numpy as jnp
from jax import lax
from jax.experimental import pallas as pl
from jax.experimental.pallas import tpu as pltpu

F32 = jnp.float32
BF16 = jnp.bfloat16

HEAD = 64
POOL_WINDOWS = (2, 4, 8, 16)
GRID_W = 64
DECAY_LORA = 64
AAA_LORA = 64
GATE_LORA = 128
NORM_EPS = 1e-6
GN_EPS = 64e-5

LANES = 128
MXU_DIM = 256
HEADS_PER_GROUP = MXU_DIM // HEAD
VMEM_LIMIT = 56 * 1024 * 1024

CHUNK = 64
SEQ_TILE = 256
PROJ_TN = 512
SCAN_BLOCK = 1024
SCAN_WAVE = 4


def _cparams(sem):
    return pltpu.CompilerParams(dimension_semantics=sem, vmem_limit_bytes=VMEM_LIMIT)


def _dot(a, b, dims=(((1,), (0,)), ((), ())), precision=None):
    return lax.dot_general(a, b, dims, precision=precision, preferred_element_type=F32)


_NN = (((1,), (0,)), ((), ()))
_NT = (((1,), (1,)), ((), ()))
_TN = (((0,), (0,)), ((), ()))


def _split2(x):
    hi = x.astype(BF16)
    lo = (x - hi.astype(F32)).astype(BF16)
    return hi, lo


def _split3(x):
    hi = x.astype(BF16)
    r1 = x - hi.astype(F32)
    mid = r1.astype(BF16)
    lo = (r1 - mid.astype(F32)).astype(BF16)
    return hi, mid, lo


def _mm(a, b, dims=_NN, passes=1):
    if passes == 1:
        return _dot(a.astype(BF16), b.astype(BF16), dims)
    if passes == 3:
        ah, al = _split2(a)
        bh, bl = _split2(b)
        return _dot(ah, bh, dims) + (_dot(ah, bl, dims) + _dot(al, bh, dims))
    return _dot(a, b, dims, precision=lax.Precision.HIGHEST)


def _mm_exact_rhs(a, b_bf16):
    a0, a1 = _split2(a)
    return _dot(a0, b_bf16) + _dot(a1, b_bf16)


def _sigmoid(x):
    return 1.0 / (1.0 + jnp.exp(-x))


def _mod_kernel(c_ref, w_ref, b_ref, o_ref):
    c = c_ref[...]
    s = (c * _sigmoid(c)).astype(BF16)
    o_ref[...] = _dot(s, w_ref[...].astype(BF16)) + b_ref[...]


def _mod(cond, w_mod, b_mod):
    nb, d = cond.shape
    rows = -(-nb // 16) * 16
    cond_p = jnp.pad(cond, ((0, rows - nb), (0, 0)))
    n = w_mod.shape[1]
    tn = 1024
    out = pl.pallas_call(
        _mod_kernel,
        grid=(n // tn,),
        in_specs=[pl.BlockSpec((rows, d), lambda j: (0, 0)),
                  pl.BlockSpec((d, tn), lambda j: (0, j)),
                  pl.BlockSpec((1, tn), lambda j: (0, j))],
        out_specs=pl.BlockSpec((rows, tn), lambda j: (0, j)),
        out_shape=jax.ShapeDtypeStruct((rows, n), F32),
        compiler_params=_cparams(("arbitrary",)),
        name="mod",
    )(cond_p, w_mod, b_mod.reshape(1, n))
    return out[:nb]


NORM_ROWS = 16


def _in_kernel(x_ref, g_ref, mod_ref, w_ref, om_ref, og_ref, h_ref, *, nj_main):
    j = pl.program_id(1)

    @pl.when(j == 0)
    def _():
        gain = g_ref[...]
        scale = 1.0 + mod_ref[0, 1:2, :]
        shift = mod_ref[0, 0:1, :]

        def norm_rows(i, carry):
            rows = pl.ds(pl.multiple_of(i * NORM_ROWS, NORM_ROWS), NORM_ROWS)
            x = x_ref[rows, :]
            y = x * lax.rsqrt(jnp.mean(x * x, axis=-1, keepdims=True) + NORM_EPS)
            h_ref[rows, :] = ((y * gain) * scale + shift).astype(BF16)
            return carry

        lax.fori_loop(0, x_ref.shape[0] // NORM_ROWS, norm_rows, 0, unroll=8)

    @pl.when(j < nj_main)
    def _():
        om_ref[...] = _dot(h_ref[...], w_ref[0])

    @pl.when(j >= nj_main)
    def _():
        og_ref[...] = _dot(h_ref[...], w_ref[0])


def _col_tiles(w, tn):
    k, n = w.shape
    return jnp.transpose(w.reshape(k, n // tn, tn), (1, 0, 2))


def _in_proj(x2, norm_g, mod3, tiles_per_mod_tokens, w_tiles, n_main, tm=1024):
    m, d = x2.shape
    tm = min(tm, m)
    tn = w_tiles.shape[2]
    n = w_tiles.shape[0] * tn
    tpm = tiles_per_mod_tokens // tm
    nj_main = n_main // tn
    return pl.pallas_call(
        functools.partial(_in_kernel, nj_main=nj_main),
        grid=(m // tm, n // tn),
        in_specs=[pl.BlockSpec((tm, d), lambda i, j: (i, 0)),
                  pl.BlockSpec((1, d), lambda i, j: (0, 0)),
                  pl.BlockSpec((1, 6, d), lambda i, j: (i // tpm, 0, 0)),
                  pl.BlockSpec((1, d, tn), lambda i, j: (j, 0, 0))],
        out_specs=[pl.BlockSpec((tm, tn), lambda i, j: (i, jnp.minimum(j, nj_main - 1))),
                   pl.BlockSpec((tm, tn), lambda i, j: (i, jnp.maximum(j - nj_main, 0)))],
        out_shape=[jax.ShapeDtypeStruct((m, n_main), F32), jax.ShapeDtypeStruct((m, n - n_main), F32)],
        scratch_shapes=[pltpu.VMEM((tm, d), BF16)],
        compiler_params=_cparams(("parallel", "arbitrary")),
        name="in_proj",
    )(x2, norm_g.reshape(1, d), mod3, w_tiles)


P_LORA_DECAY = 3
P_LORA = 1


def _head_sum(x, ones_bd):
    return _mm_exact_rhs(x, ones_bd)


def _prep_kernel(main_ref, prev_ref, next_ref, mu_ref, w0_ref, w2_ref, a0_ref, a2_ref, g2_ref,
                 kkw_ref, ka_ref, rk_ref, ones_ref, r_o, k_o, v_o, kk_o, lwf_o, lwb_o, af_o, ab_o, g_o, bg_o,
                 *, n_tiles, d_rwkv):
    t = pl.program_id(1)
    rows = main_ref.shape[1]
    row8 = lax.broadcasted_iota(jnp.int32, (8, 1), 0)
    has_prev = t > 0
    has_next = t < n_tiles - 1

    def shifted(c0, c1):
        z = main_ref[0, :, c0:c1]
        p = jnp.where(has_prev, prev_ref[0, 7:8, c0:c1], 0.0)
        n = jnp.where(has_next, next_ref[0, 0:1, c0:c1], 0.0)
        zp = pltpu.roll(z, 1, 0)
        zn = pltpu.roll(z, rows - 1, 0)
        zp = jnp.concatenate([jnp.where(row8 == 0, p, zp[:8]), zp[8:]], axis=0)
        zn = jnp.concatenate([zn[:rows - 8], jnp.where(row8 == 7, n, zn[rows - 8:])], axis=0)
        mu = mu_ref[:, c0:c1]
        return (1.0 - mu) * z + (0.5 * mu) * (zp + zn)

    dr = d_rwkv
    r = shifted(0, dr)
    r_o[0] = r.astype(BF16)
    k = shifted(dr, 2 * dr)
    k_o[0] = k.astype(BF16)
    v = shifted(2 * dr, 3 * dr)
    v_o[0] = v.astype(BF16)

    kkv = k * kkw_ref[...]
    ones_bd = ones_ref[...]
    for j in range(dr // MXU_DIM):
        sl = slice(j * MXU_DIM, (j + 1) * MXU_DIM)
        x = kkv[:, sl]
        ss = _head_sum(x * x, ones_bd)
        kk_o[0, :, sl] = (x * lax.rsqrt(jnp.maximum(ss, 1e-24))).astype(BF16)

    c = 3 * dr
    sm = shifted(c, c + 2 * DECAY_LORA + 2 * AAA_LORA + GATE_LORA)
    o = 0
    for d, lw_o in enumerate((lwf_o, lwb_o)):
        wd = jnp.tanh(sm[:, o:o + DECAY_LORA])
        o += DECAY_LORA
        xw = w0_ref[d:d + 1, :] + _mm(wd, w2_ref[d], passes=P_LORA_DECAY)
        u = -xw
        softplus = jnp.maximum(u, 0.0) + jnp.log1p(jnp.exp(-jnp.abs(u)))
        lw_o[0] = -jnp.exp(-softplus - 0.5)
    for d, a_o in enumerate((af_o, ab_o)):
        ad = sm[:, o:o + AAA_LORA]
        o += AAA_LORA
        a_o[0] = _sigmoid(a0_ref[d:d + 1, :] + _mm(ad, a2_ref[d], passes=P_LORA))
    gd = _sigmoid(sm[:, o:o + GATE_LORA])
    g = _mm(gd, g2_ref[...], passes=P_LORA)
    g_o[0] = g.astype(BF16)

    for j in range(dr // MXU_DIM):
        sl = slice(j * MXU_DIM, (j + 1) * MXU_DIM)
        ka = ka_ref[:, sl]
        kj = k[:, sl]
        kd_sum = kj * (1.0 + (af_o[0, :, sl] - 1.0) * ka) + kj * (1.0 + (ab_o[0, :, sl] - 1.0) * ka)
        bonus = _head_sum(r[:, sl] * kd_sum * rk_ref[:, sl], ones_bd) * v[:, sl]
        bg_o[0, :, sl] = (bonus * g[:, sl]).astype(BF16)


def _prep(proj3, mu_p, w0, w2, a0, a2, g2, k_k, k_a, r_k, ones_bd, d_rwkv, rw_cols_p):
    b, t, _ = proj3.shape
    tp = SEQ_TILE
    n_tiles = t // tp
    sub = tp // 8
    last8 = t // 8 - 1
    out_sd = jax.ShapeDtypeStruct((b, t, d_rwkv), F32)
    half_sd = jax.ShapeDtypeStruct((b, t, d_rwkv), BF16)
    o_spec =pl.BlockSpec((1, tp, d_rwkv), lambda i, j: (i, j, 0))

    def full(a):
        nd = a.ndim
        return pl.BlockSpec(a.shape, lambda i, j: (0,) * nd)

    params = (mu_p, w0, w2, a0, a2, g2, k_k.reshape(1, d_rwkv), k_a.reshape(1, d_rwkv),
              r_k.reshape(1, d_rwkv), ones_bd)
    return pl.pallas_call(
        functools.partial(_prep_kernel, n_tiles=n_tiles, d_rwkv=d_rwkv),
        grid=(b, n_tiles),
        in_specs=[pl.BlockSpec((1, tp, rw_cols_p), lambda i, j: (i, j, 0)),
                  pl.BlockSpec((1, 8, rw_cols_p), lambda i, j: (i, jnp.maximum(j * sub - 1, 0), 0)),
                  pl.BlockSpec((1, 8, rw_cols_p), lambda i, j: (i, jnp.minimum((j + 1) * sub, last8), 0)),
                  ] + [full(a) for a in params],
        out_specs=[o_spec] * 10,
        out_shape=[half_sd] * 4 + [out_sd] * 4 + [half_sd] * 2,
        compiler_params=_cparams(("parallel", "arbitrary")),
        name="rwkv_prep",
    )(proj3, proj3, proj3, *params)


def _block_diag_rhs(y, bd):
    yb = y.astype(BF16)
    return jnp.concatenate([yb] * HEADS_PER_GROUP, axis=0) * bd


def _mmc(x, y_bd):
    return _dot(x.astype(BF16), y_bd)


def _scan_masks(rev):
    c = CHUNK
    ti = lax.broadcasted_iota(jnp.int32, (c, c), 0)
    si = lax.broadcasted_iota(jnp.int32, (c, c), 1)
    tri = jnp.where((si >= ti) if rev else (si <= ti), 1.0, 0.0).astype(BF16)
    tc = lax.broadcasted_iota(jnp.int32, (c, MXU_DIM), 0)
    sc = lax.broadcasted_iota(jnp.int32, (c, MXU_DIM), 1) % c
    before = (sc > tc) if rev else (sc < tc)
    upto = (sc >= tc) if rev else (sc <= tc)
    return tri, before, upto


def _scan_chunk_local(out, refs, sl, dir_masks, blk_masks, ka, last_row, bd):
    r_ref, k_ref, v_ref, kk_ref, lw_ref, a_ref = refs
    tri, before, upto = dir_masks
    eye, same16, off32, off64 = blk_masks
    c = CHUNK
    kk = kk_ref[0, sl, :].astype(F32)
    lw = lw_ref[0, sl, :]
    a = a_ref[0, sl, :]
    lw0, lw1, lw2 = _split3(lw)
    cum = _dot(tri, lw0) + (_dot(tri, lw1) + _dot(tri, lw2))
    yield
    e_in = jnp.exp(cum)
    e_inv = jnp.exp(-cum)
    al = kk * jnp.exp(cum - lw)
    rh = r_ref[0, sl, :].astype(F32) * e_in
    be = (kk * a) * e_inv
    kap = (k_ref[0, sl, :].astype(F32) * (1.0 + (a - 1.0) * ka)) * e_inv
    ptot = e_in[last_row:last_row + 1, :]
    ar = jnp.concatenate([al, rh], axis=0).astype(BF16)
    ab = _dot(ar, _block_diag_rhs(be, bd), _NT)
    ak = _dot(ar, _block_diag_rhs(kap, bd), _NT)
    yield
    lm = jnp.where(before, ab[:c], 0.0)
    aak = jnp.where(before, ak[:c], 0.0)
    kq = jnp.where(upto, ak[c:], 0.0)
    out["bq"] = jnp.where(upto, ab[c:], 0.0).astype(BF16)
    ld = jnp.where(same16, lm, 0.0)
    p = eye - ld
    ld_bd = _block_diag_rhs(ld, bd)
    l2 = _mmc(ld, ld_bd)
    yield
    pl2 = _mmc(jnp.concatenate([p, l2], axis=0), _block_diag_rhs(l2, bd))
    yield
    p = p + pl2[:c]
    l4 = pl2[c:]
    pl4 = _mmc(jnp.concatenate([p, l4], axis=0), _block_diag_rhs(l4, bd))
    yield
    p = p + pl4[:c]
    p8 = _mmc(p, _block_diag_rhs(pl4[c:], bd))
    v = v_ref[0, sl, :]
    akv = _mmc(jnp.concatenate([aak, kq], axis=0), _block_diag_rhs(v, bd))
    aak_v = akv[:c]
    out["kq_v"] = akv[c:]
    yield
    p = p + p8
    for off in (off32, off64):
        lo = jnp.where(off, lm, 0.0)
        t1 = _mmc(lo, _block_diag_rhs(p, bd))
        yield
        t2 = _mmc(p, _block_diag_rhs(t1, bd))
        yield
        p = p - t2
    wt = _mmc(p, _block_diag_rhs(al, bd))
    out["ut"] = _mmc(p, _block_diag_rhs(aak_v, bd))
    yield
    out["wr"] = jnp.concatenate([wt, rh], axis=0).astype(BF16)
    out["v"] = v.astype(BF16)
    out["bkT"] = jnp.transpose(jnp.concatenate([be, kap], axis=0)).astype(BF16)
    out["ptot"] = ptot


def _scan_chain(items, m_scr, y_ref, bd, bd_mask):
    c = CHUNK
    for loc, sl in items:
        m = m_scr[...]
        zs = _dot(loc["wr"], m.astype(BF16))
        yield
        u = -(zs[:c] + loc["ut"])
        uv = jnp.concatenate([u.astype(BF16), loc["v"]], axis=0)
        dm = _dot(loc["bkT"], uv)
        yu = _mmc(loc["bq"], _block_diag_rhs(u, bd))
        yield
        y_ref[0, sl, :] = (zs[c:] + loc["kq_v"] + yu).astype(BF16)
        pc = jnp.transpose(jnp.broadcast_to(loc["ptot"], (8, MXU_DIM)))
        m_scr[...] = jnp.broadcast_to(pc[:, 0:1], (MXU_DIM, MXU_DIM)) * (m + jnp.where(bd_mask, dm, 0.0))


def _run_lockstep(gens):
    gens = list(gens)
    while gens:
        alive = []
        for g in gens:
            try:
                next(g)
                alive.append(g)
            except StopIteration:
                pass
        gens = alive


def _scan_kernel(rf_ref, kf_ref, vf_ref, kkf_ref, lwf_ref, af_ref, rb_ref, kb_ref, vb_ref, kkb_ref, lwb_ref,
                 ab_ref, ka_ref, s0_ref, yf_ref, yb_ref, sfin_ref, mf_scr, mb_scr,
                 *, n_sub, n_blk):
    c = CHUNK
    blk = pl.program_id(2)

    @pl.when(blk == 0)
    def _():
        zero = jnp.zeros((HEAD, HEAD), F32)
        for d, scr in enumerate((mf_scr, mb_scr)):
            for h in range(HEADS_PER_GROUP):
                blocks = [zero] * HEADS_PER_GROUP
                blocks[h] = jnp.transpose(s0_ref[0, d, h])
                scr[h * HEAD:(h + 1) * HEAD, :] = jnp.concatenate(blocks, axis=1)

    tc = lax.broadcasted_iota(jnp.int32, (c, MXU_DIM), 0)
    sc = lax.broadcasted_iota(jnp.int32, (c, MXU_DIM), 1) % c
    same16 = (sc // 16) == (tc // 16)
    same32 = (sc // 32) == (tc // 32)
    blk_masks = (jnp.where(sc == tc, 1.0, 0.0), same16,
                 jnp.logical_and(same32, jnp.logical_not(same16)), jnp.logical_not(same32))
    bd_mask = (lax.broadcasted_iota(jnp.int32, (MXU_DIM, MXU_DIM), 0) // HEAD) == \
              (lax.broadcasted_iota(jnp.int32, (MXU_DIM, MXU_DIM), 1) // HEAD)
    bd = jnp.where(bd_mask, 1.0, 0.0).astype(BF16)
    ka = ka_ref[...]

    refs_f = (rf_ref, kf_ref, vf_ref, kkf_ref, lwf_ref, af_ref)
    refs_b = (rb_ref, kb_ref, vb_ref, kkb_ref, lwb_ref, ab_ref)
    masks_f = _scan_masks(False)
    masks_b = _scan_masks(True)
    slices = [slice(j * c, (j + 1) * c) for j in range(n_sub)]
    items_f = [({}, sl) for sl in slices]
    items_b = [({}, sl) for sl in reversed(slices)]
    chains = []
    for w0 in range(0, n_sub, SCAN_WAVE):
        wave_f = items_f[w0:w0 + SCAN_WAVE]
        wave_b = items_b[w0:w0 + SCAN_WAVE]
        gens = []
        for (of, sf), (ob, sb) in zip(wave_f, wave_b):
            gens.append(_scan_chunk_local(of, refs_f, sf, masks_f, blk_masks, ka, c - 1, bd))
            gens.append(_scan_chunk_local(ob, refs_b, sb, masks_b, blk_masks, ka, 0, bd))
        _run_lockstep(gens + chains)
        chains = [_scan_chain(wave_f, mf_scr, yf_ref, bd, bd_mask),
                  _scan_chain(wave_b, mb_scr, yb_ref, bd, bd_mask)]
    _run_lockstep(chains)

    @pl.when(blk == n_blk - 1)
    def _():
        for d, scr in enumerate((mf_scr, mb_scr)):
            for h in range(HEADS_PER_GROUP):
                rows = scr[h * HEAD:(h + 1) * HEAD, :]
                sfin_ref[0, d, h] = jnp.transpose(rows[:, h * HEAD:(h + 1) * HEAD])


def _scan(r, k, v, kk, lwf, af, lwb, ab, k_a, s0):
    b, t, d_rwkv = r.shape
    groups = d_rwkv // MXU_DIM
    tb = min(SCAN_BLOCK, t)
    n_blk = t // tb
    n_sub = tb // CHUNK
    tok_f =pl.BlockSpec((1, tb, MXU_DIM), lambda i, g, j: (i, j, g))
    tok_b = pl.BlockSpec((1, tb, MXU_DIM), lambda i, g, j: (i, n_blk - 1 - j, g))
    st = pl.BlockSpec((1, 2, HEADS_PER_GROUP, HEAD, HEAD), lambda i, g, j: (i, 0, g, 0, 0))
    y_sd = jax.ShapeDtypeStruct((b, t, d_rwkv), BF16)
    return pl.pallas_call(
        functools.partial(_scan_kernel, n_sub=n_sub, n_blk=n_blk),
        grid=(b, groups, n_blk),
        in_specs=[tok_f] * 6 + [tok_b] * 6 + [pl.BlockSpec((1, MXU_DIM), lambda i, g, j: (0, g)), st],
        out_specs=[tok_f, tok_b, st],
        out_shape=[y_sd, y_sd, jax.ShapeDtypeStruct(s0.shape, F32)],
        scratch_shapes=[pltpu.VMEM((MXU_DIM, MXU_DIM), F32), pltpu.VMEM((MXU_DIM, MXU_DIM), F32)],
        compiler_params=_cparams(("parallel", "parallel", "arbitrary")),
        name="scan",
    )(r, k, v, kk, lwf, af, r, k, v, kk, lwb, ab, k_a.reshape(1, d_rwkv), s0)


def _post_kernel(yf_ref, yb_ref, g_ref, bg_ref, lg_ref, lb_ref, ones_ref, o_ref):
    ones_bd = ones_ref[...]
    inv_n = 1.0 / HEAD
    y = yf_ref[0].astype(F32) + yb_ref[0].astype(F32)
    mu = _head_sum(y, ones_bd) * inv_n
    d = y - mu
    var = _head_sum(d * d, ones_bd) * inv_n
    yn = d * lax.rsqrt(var + GN_EPS) * lg_ref[...] + lb_ref[...]
    o_ref[0] = (yn * g_ref[0].astype(F32) + bg_ref[0].astype(F32)).astype(BF16)


def _post(yf, yb, g, bg, lnx_g, lnx_b, ones_bd):
    b, t, d_rwkv = yf.shape
    tp = min(512, t)
    tok = pl.BlockSpec((1, tp, MXU_DIM), lambda i, j, q: (i, j, q))
    par = pl.BlockSpec((1, MXU_DIM), lambda i, j, q: (0, q))
    return pl.pallas_call(
        _post_kernel,
        grid=(b, t // tp, d_rwkv // MXU_DIM),
        in_specs=[tok] * 4 + [par] * 2 + [pl.BlockSpec((MXU_DIM, MXU_DIM), lambda i, j, q: (0, 0))],
        out_specs=tok,
        out_shape=jax.ShapeDtypeStruct((b, t, d_rwkv), BF16),
        compiler_params=_cparams(("parallel", "parallel", "arbitrary")),
        name="rwkv_post",
    )(yf, yb, g, bg, lnx_g.reshape(1, d_rwkv), lnx_b.reshape(1, d_rwkv), ones_bd)


def _window_bounds(n, w):
    t = np.arange(n)
    lo = np.clip(t - w // 2, 0, n)
    hi = np.clip(t - w // 2 + w, 0, n)
    return lo, hi


def _window_matrix(n, w):
    lo, hi = _window_bounds(n, w)
    s = np.arange(n)[None, :]
    m = ((s >= lo[:, None]) & (s < hi[:, None])).astype(np.float32)
    return m / (hi - lo)[:, None].astype(np.float32)


def _pool_ctx_kernel(z_ref, band_ref, pw_ref, ps_ref, o_ref):
    z = z_ref[0]
    m = _dot(band_ref[0], z, precision=lax.Precision.HIGHEST)
    d = (m - z).astype(BF16)
    o_ref[0] = (_dot(d, pw_ref[0]) * ps_ref[...]).astype(BF16)


def _pool_ctx(proj3, col0, pool_w_bf16, pool_scale):
    b, t, _ = proj3.shape
    ng, pg, _ = pool_w_bf16.shape
    band = jnp.asarray(np.stack([_window_matrix(t, w) for w in POOL_WINDOWS]))
    cb = col0 // pg
    return pl.pallas_call(
        _pool_ctx_kernel,
        grid=(b, ng),
        in_specs=[pl.BlockSpec((1, t, pg), lambda i, q: (i, 0, cb + q)),
                  pl.BlockSpec((1, t, t), lambda i, q: (q, 0, 0)),
                  pl.BlockSpec((1, pg, pg), lambda i, q: (q, 0, 0)),
                  pl.BlockSpec((1, pg), lambda i, q: (0, q))],
        out_specs=pl.BlockSpec((1, t, pg), lambda i, q: (i, 0, q)),
        out_shape=jax.ShapeDtypeStruct((b, t, ng * pg), BF16),
        compiler_params=_cparams(("parallel", "arbitrary")),
        name="pool_ctx",
    )(proj3, band, pool_w_bf16, pool_scale.reshape(1, ng * pg))


def _pool_lat_kernel(z_ref, band_ref, pw_ref, ps_ref, o_ref, m1_ref, cs_ref, *, n_rows):
    tile = band_ref.shape[1]
    n_tok = n_rows * GRID_W
    band = band_ref[0]
    for i in range(n_tok // tile):
        sl = slice(i * tile, (i + 1) * tile)
        m1_ref[sl, :] = _dot(band, z_ref[0, sl, :], precision=lax.Precision.HIGHEST)
    cs_ref[0:GRID_W, :] = jnp.zeros((GRID_W, cs_ref.shape[1]), F32)
    for rr in range(n_rows):
        cs_ref[(rr + 1) * GRID_W:(rr + 2) * GRID_W, :] = (
            cs_ref[rr * GRID_W:(rr + 1) * GRID_W, :] + m1_ref[rr * GRID_W:(rr + 1) * GRID_W, :])
    group = pl.program_id(1)
    for gi, w in enumerate(POOL_WINDOWS):
        @pl.when(group == gi)
        def _(w=w):
            lo, hi = _window_bounds(n_rows, w)
            for rr in range(n_rows):
                l, h = int(lo[rr]), int(hi[rr])
                m2 = (cs_ref[h * GRID_W:(h + 1) * GRID_W, :]
                      - cs_ref[l * GRID_W:(l + 1) * GRID_W, :]) / float(h - l)
                sl = slice(rr * GRID_W, (rr + 1) * GRID_W)
                m1_ref[sl, :] = m2 - z_ref[0, sl, :]
    pw = pw_ref[0]
    ps = ps_ref[...]
    for i in range(n_tok // tile):
        sl = slice(i * tile, (i + 1) * tile)
        o_ref[0, sl, :] = (_dot(m1_ref[sl, :].astype(BF16), pw) * ps).astype(BF16)


def _pool_lat(proj3, col0, pool_w_bf16, pool_scale):
    b, t, _ = proj3.shape
    ng, pg, _ = pool_w_bf16.shape
    n_rows = t // GRID_W
    tile = MXU_DIM
    band = jnp.asarray(np.stack(
        [np.kron(np.eye(tile // GRID_W, dtype=np.float32), _window_matrix(GRID_W, w)) for w in POOL_WINDOWS]))
    cb = col0 // pg
    return pl.pallas_call(
        functools.partial(_pool_lat_kernel, n_rows=n_rows),
        grid=(b, ng),
        in_specs=[pl.BlockSpec((1, t, pg), lambda i, q: (i, 0, cb + q)),
                  pl.BlockSpec((1, tile, tile), lambda i, q: (q, 0, 0)),
                  pl.BlockSpec((1, pg, pg), lambda i, q: (q, 0, 0)),
                  pl.BlockSpec((1, pg), lambda i, q: (0, q))],
        out_specs=pl.BlockSpec((1, t, pg), lambda i, q: (i, 0, q)),
        out_shape=jax.ShapeDtypeStruct((b, t, ng * pg), BF16),
        scratch_shapes=[pltpu.VMEM((t, pg), F32), pltpu.VMEM((t + GRID_W, pg), F32)],
        compiler_params=_cparams(("parallel", "arbitrary")),
        name="pool_lat",
    )(proj3, band, pool_w_bf16, pool_scale.reshape(1, ng * pg))


ROW_SPLIT = 2


def _resident(shape):
    nd = len(shape)
    return pl.BlockSpec(shape, lambda i: (0,) * nd, pipeline_mode=pl.Buffered(1))


def _up_out_kernel(ya_ref, yb_ref, ga_ref, gb_ref, x_ref, mod_ref, g_ref, wa_ref, wb_ref, wo_ref,
                   x1_ref, h2_ref):
    rows = x_ref.shape[0] // ROW_SPLIT
    for part in range(ROW_SPLIT):
        sl = slice(part * rows, (part + 1) * rows)
        ua = _dot(ya_ref[sl, :], wa_ref[...])
        ub = _dot(yb_ref[sl, :], wb_ref[...])
        merged = (_sigmoid(ga_ref[sl, :]) * ua + _sigmoid(gb_ref[sl, :]) * ub).astype(BF16)
        x1 = x_ref[sl, :] + mod_ref[0, 2:3, :] * _dot(merged, wo_ref[...])
        x1_ref[sl, :] = x1
        y = x1 * lax.rsqrt(jnp.mean(x1 * x1, axis=-1, keepdims=True) + NORM_EPS)
        h2 = (y * g_ref[...]) * (1.0 + mod_ref[0, 4:5, :]) + mod_ref[0, 3:4, :]
        h2_ref[sl, :] = h2.astype(BF16)


def _up_out(ya2, yb2, gates, wa, wb, w_out, x2, mod3, tiles_per_mod_tokens, norm2_g, tm=256):
    m, d = x2.shape
    tm = min(tm, m)
    ka, kb = ya2.shape[1], yb2.shape[1]
    tpm = tiles_per_mod_tokens // tm
    row = lambda width, col: pl.BlockSpec((tm, width), lambda i: (i, col))
    return pl.pallas_call(
        _up_out_kernel,
        grid=(m // tm,),
        in_specs=[row(ka, 0), row(kb, 0), row(d, 0), row(d, 1), row(d, 0),
                  pl.BlockSpec((1, 6, d), lambda i: (i // tpm, 0, 0)),
                  _resident((1, d)), _resident((ka, d)), _resident((kb, d)), _resident((d, d))],
        out_specs=[row(d, 0), row(d, 0)],
        out_shape=[jax.ShapeDtypeStruct((m, d), F32), jax.ShapeDtypeStruct((m, d), BF16)],
        compiler_params=_cparams(("parallel",)),
        name="up_out",
    )(ya2, yb2, gates, gates, x2, mod3, norm2_g.reshape(1, d), wa, wb, w_out)


def _ffn1_kernel(h_ref, w1_ref, w3_ref, o_ref):
    h = h_ref[...]
    u1 = _dot(h, w1_ref[0])
    u3 = _dot(h, w3_ref[0])
    o_ref[...] = (u1 * _sigmoid(u1) * u3).astype(BF16)


def _ffn1(h2, w13_tiles, tm=1024):
    m, d = h2.shape
    tm = min(tm, m)
    tn = w13_tiles.shape[2]
    nj = w13_tiles.shape[0] // 2
    return pl.pallas_call(
        _ffn1_kernel,
        grid=(m // tm, nj),
        in_specs=[pl.BlockSpec((tm, d), lambda i, j: (i, 0)),
                  pl.BlockSpec((1, d, tn), lambda i, j: (j, 0, 0)),
                  pl.BlockSpec((1, d, tn), lambda i, j: (nj + j, 0, 0))],
        out_specs=pl.BlockSpec((tm, tn), lambda i, j: (i, j)),
        out_shape=jax.ShapeDtypeStruct((m, nj * tn), BF16),
        compiler_params=_cparams(("parallel", "arbitrary")),
        name="ffn1",
    )(h2, w13_tiles, w13_tiles)


def _ffn2_kernel(a_ref, x_ref, mod_ref, g_ref, w_ref, o_ref):
    rows = x_ref.shape[0] // ROW_SPLIT
    for part in range(ROW_SPLIT):
        sl = slice(part * rows, (part + 1) * rows)
        x2 = x_ref[sl, :] + mod_ref[0, 5:6, :] * _dot(a_ref[sl, :], w_ref[...])
        y = x2 * lax.rsqrt(jnp.mean(x2 * x2, axis=-1, keepdims=True) + NORM_EPS)
        o_ref[sl, :] = y * g_ref[...]


def _ffn2(act, w2, x1, mod3, tiles_per_mod_tokens, final_g, tm=256):
    m, d = x1.shape
    tm = min(tm, m)
    d_ff = act.shape[1]
    tpm = tiles_per_mod_tokens // tm
    return pl.pallas_call(
        _ffn2_kernel,
        grid=(m // tm,),
        in_specs=[pl.BlockSpec((tm, d_ff), lambda i: (i, 0)),
                  pl.BlockSpec((tm, d), lambda i: (i, 0)),
                  pl.BlockSpec((1, 6, d), lambda i: (i // tpm, 0, 0)),
                  _resident((1, d)), _resident((d_ff, d))],
        out_specs=pl.BlockSpec((tm, d), lambda i: (i, 0)),
        out_shape=jax.ShapeDtypeStruct((m, d), F32),
        compiler_params=_cparams(("parallel",)),
        name="ffn2",
    )(act, x1, mod3, final_g.reshape(1, d), w2)


def _trunk(x, mod, s0, latent, p):
    b, t, d = x.shape
    m = b * t
    d_rwkv = p["d_rwkv"]
    d_pool = d - d_rwkv
    heads = d_rwkv // HEAD
    x2 = x.reshape(m, d)
    nb = mod.shape[0]
    mod3 = mod.reshape(nb, 6, d)
    tokens_per_mod = t if nb == b else m

    proj, gates = _in_proj(x2, p["norm1_g"], mod3, tokens_per_mod, p["w_in"], p["rw_cols_p"] + d_pool)
    proj3 = proj.reshape(b, t, proj.shape[1])

    r, k, v, kk, lwf, lwb, af, ab, g, bg = _prep(
        proj3, p["mu"], p["w0"], p["w2"], p["a0"], p["a2"], p["g2"], p["k_k"], p["k_a"], p["r_k"],
        p["ones_bd"], d_rwkv, p["rw_cols_p"])

    if s0 is None:
        s0 = jnp.zeros((b, 2, heads, HEAD, HEAD), F32)
    yf, yb, s_fin = _scan(r, k, v, kk, lwf, af, lwb, ab, p["k_a"], s0)
    ya = _post(yf, yb, g, bg, p["lnx_g"], p["lnx_b"], p["ones_bd"])

    pool_col0 = p["rw_cols_p"]
    if latent:
        yb_pool = _pool_lat(proj3, pool_col0, p["pool_w"], p["pool_scale"])
    else:
        yb_pool = _pool_ctx(proj3, pool_col0, p["pool_w"], p["pool_scale"])

    x1, h2 = _up_out(ya.reshape(m, d_rwkv), yb_pool.reshape(m, d_pool), gates, p["w_up_a"], p["w_up_b"],
                     p["w_out"], x2, mod3, tokens_per_mod, p["norm2_g"])
    act = _ffn1(h2, p["ffn_w13"])
    y = _ffn2(act, p["ffn_w2"], x1, mod3, tokens_per_mod, p["final_g"])
    return y.reshape(b, t, d), s_fin


def kernel(x_prompt, x_sample, c, state_rwkv, c_ctx, w_mod, b_mod, norm1_g, w_in, shift_mu, w0, w2, a0, a2, g2,
           k_k, k_a, r_k, lnx_g, lnx_b, w_up_a, pool_w, pool_scale, w_up_b, w_out, norm2_g, ffn_w13, ffn_w2,
           final_g):
    depth = w_in.shape[0]
    assert depth == 1, "final norm is fused into the last layer: single-layer trunk only"
    d = x_prompt.shape[-1]
    d_rwkv = w_up_a.shape[1]
    rw_cols = shift_mu.shape[1]
    rw_cols_p = -(-rw_cols // PROJ_TN) * PROJ_TN
    assert x_prompt.shape[1] == SEQ_TILE and x_sample.shape[1] % SEQ_TILE == 0

    l = 0
    w_in_l = w_in[l].astype(BF16)
    w_in_p = jnp.concatenate(
        [w_in_l[:, :rw_cols], jnp.zeros((d, rw_cols_p - rw_cols), BF16), w_in_l[:, rw_cols:]], axis=1)
    ones_bd = jnp.asarray(np.kron(np.eye(HEADS_PER_GROUP, dtype=np.float32),
                                  np.ones((HEAD, HEAD), np.float32))).astype(BF16)
    p = {
        "d_rwkv": d_rwkv, "rw_cols_p": rw_cols_p,
        "norm1_g": norm1_g[l], "w_in": _col_tiles(w_in_p, PROJ_TN),
        "mu": jnp.pad(shift_mu[l], (0, rw_cols_p - rw_cols)).reshape(1, rw_cols_p),
        "w0": w0[l], "w2": w2[l], "a0": a0[l], "a2": a2[l], "g2": g2[l],
        "k_k": k_k[l], "k_a": k_a[l], "r_k": r_k[l].reshape(-1), "lnx_g": lnx_g[l], "lnx_b": lnx_b[l],
        "ones_bd": ones_bd,
        "w_up_a": w_up_a[l].astype(BF16), "w_up_b": w_up_b[l].astype(BF16),
        "pool_w": pool_w[l].astype(BF16), "pool_scale": pool_scale[l],
        "w_out": w_out[l].astype(BF16), "norm2_g": norm2_g[l],
        "ffn_w13": _col_tiles(ffn_w13[l].astype(BF16), PROJ_TN), "ffn_w2": ffn_w2[l].astype(BF16), "final_g": final_g,
    }
    cond = jnp.concatenate([c_ctx[None, :], c], axis=0)
    mod = _mod(cond, w_mod[l], b_mod[l])
    y_prompt, s_ctx = _trunk(x_prompt, mod[:1], None, False, p)
    y_sample, _ = _trunk(x_sample, mod[1:], state_rwkv[:, l], True, p)
    new_state = s_ctx[:, None].astype(x_prompt.dtype)
    return (y_prompt, y_sample, new_state)
```

```python
import functools

import numpy as np
import jax
import jax.numpy as jnp
from jax import lax
from jax.experimental import pallas as pl
from jax.experimental.pallas import tpu as pltpu

F32 = jnp.float32
BF16 = jnp.bfloat16

HEAD = 64
POOL_WINDOWS = (2, 4, 8, 16)
GRID_W = 64
DECAY_LORA = 64
AAA_LORA = 64
GATE_LORA = 128
NORM_EPS = 1e-6
GN_EPS = 64e-5

MXU_DIM = 256
HEADS_PER_GROUP = MXU_DIM // HEAD
VMEM_LIMIT = 56 * 1024 * 1024

CHUNK = 64
SEQ_TILE = 256
PROJ_TN = 512
SCAN_BLOCK = 1024
SCAN_WAVE = 4


def _cparams(sem):
    return pltpu.CompilerParams(dimension_semantics=sem, vmem_limit_bytes=VMEM_LIMIT)


def _dot(a, b, dims=(((1,), (0,)), ((), ())), precision=None):
    return lax.dot_general(a, b, dims, precision=precision, preferred_element_type=F32)


_NN = (((1,), (0,)), ((), ()))
_NT = (((1,), (1,)), ((), ()))


def _split2(x):
    hi = x.astype(BF16)
    lo = (x - hi.astype(F32)).astype(BF16)
    return hi, lo


def _split3(x):
    hi = x.astype(BF16)
    r1 = x - hi.astype(F32)
    mid = r1.astype(BF16)
    lo = (r1 - mid.astype(F32)).astype(BF16)
    return hi, mid, lo


def _mm(a, b, passes):
    if passes == 1:
        return _dot(a.astype(BF16), b.astype(BF16))
    assert passes == 3
    ah, al = _split2(a)
    bh, bl = _split2(b)
    return _dot(ah, bh) + (_dot(ah, bl) + _dot(al, bh))


def _mm_exact_rhs(a, b_bf16):
    a0, a1 = _split2(a)
    return _dot(a0, b_bf16) + _dot(a1, b_bf16)


def _sigmoid(x):
    return 1.0 / (1.0 + jnp.exp(-x))


def _mod_kernel(c_ref, w_ref, b_ref, o_ref):
    c = c_ref[...]
    s = (c * _sigmoid(c)).astype(BF16)
    o_ref[...] = _dot(s, w_ref[...].astype(BF16)) + b_ref[...]


def _mod(cond, w_mod, b_mod):
    nb, d = cond.shape
    rows = -(-nb // 16) * 16
    cond_p = jnp.pad(cond, ((0, rows - nb), (0, 0)))
    n = w_mod.shape[1]
    tn = 1024
    out = pl.pallas_call(
        _mod_kernel,
        grid=(n // tn,),
        in_specs=[pl.BlockSpec((rows, d), lambda j: (0, 0)),
                  pl.BlockSpec((d, tn), lambda j: (0, j)),
                  pl.BlockSpec((1, tn), lambda j: (0, j))],
        out_specs=pl.BlockSpec((rows, tn), lambda j: (0, j)),
        out_shape=jax.ShapeDtypeStruct((rows, n), F32),
        compiler_params=_cparams(("arbitrary",)),
        name="mod",
    )(cond_p, w_mod, b_mod.reshape(1, n))
    return out[:nb]


NORM_ROWS = 16


def _in_kernel(x_ref, g_ref, mod_ref, w_ref, om_ref, og_ref, h_ref, *, nj_main):
    j = pl.program_id(1)

    @pl.when(j == 0)
    def _():
        gain = g_ref[...]
        scale = 1.0 + mod_ref[0, 1:2, :]
        shift = mod_ref[0, 0:1, :]

        def norm_rows(i, carry):
            rows = pl.ds(pl.multiple_of(i * NORM_ROWS, NORM_ROWS), NORM_ROWS)
            x = x_ref[rows, :]
            y = x * lax.rsqrt(jnp.mean(x * x, axis=-1, keepdims=True) + NORM_EPS)
            h_ref[rows, :] = ((y * gain) * scale + shift).astype(BF16)
            return carry

        lax.fori_loop(0, x_ref.shape[0] // NORM_ROWS, norm_rows, 0, unroll=8)

    @pl.when(j < nj_main)
    def _():
        om_ref[...] = _dot(h_ref[...], w_ref[...])

    @pl.when(j >= nj_main)
    def _():
        og_ref[...] = _dot(h_ref[...], w_ref[...]).astype(BF16)


def _in_proj(x2, norm_g, mod3, tiles_per_mod_tokens, w_bf16, n_main, tm=1024, tn=PROJ_TN):
    m, d = x2.shape
    tm = min(tm, m)
    n = w_bf16.shape[1]
    tpm = tiles_per_mod_tokens // tm
    nj_main = n_main // tn
    return pl.pallas_call(
        functools.partial(_in_kernel, nj_main=nj_main),
        grid=(m // tm, n // tn),
        in_specs=[pl.BlockSpec((tm, d), lambda i, j: (i, 0)),
                  pl.BlockSpec((1, d), lambda i, j: (0, 0)),
                  pl.BlockSpec((1, 6, d), lambda i, j: (i // tpm, 0, 0)),
                  pl.BlockSpec((d, tn), lambda i, j: (0, j))],
        out_specs=[pl.BlockSpec((tm, tn), lambda i, j: (i, jnp.minimum(j, nj_main - 1))),
                   pl.BlockSpec((tm, tn), lambda i, j: (i, jnp.maximum(j - nj_main, 0)))],
        out_shape=[jax.ShapeDtypeStruct((m, n_main), F32), jax.ShapeDtypeStruct((m, n - n_main), BF16)],
        scratch_shapes=[pltpu.VMEM((tm, d), BF16)],
        compiler_params=_cparams(("parallel", "arbitrary")),
        name="in_proj",
    )(x2, norm_g.reshape(1, d), mod3, w_bf16)


P_LORA_DECAY = 3
P_LORA = 1


def _head_sum(x, ones_bd):
    return _mm_exact_rhs(x, ones_bd)


def _prep_kernel(main_ref, prev_ref, next_ref, mu_ref, w0_ref, w2_ref, a0_ref, a2_ref, g2_ref,
                 kkw_ref, ka_ref, rk_ref, ones_ref, r_o, k_o, v_o, kk_o, lwf_o, lwb_o, af_o, ab_o, g_o, bg_o,
                 *, n_tiles, d_rwkv):
    t = pl.program_id(1)
    rows = main_ref.shape[1]
    row8 = lax.broadcasted_iota(jnp.int32, (8, 1), 0)
    has_prev = t > 0
    has_next = t < n_tiles - 1

    def shifted(c0, c1):
        z = main_ref[0, :, c0:c1]
        p = jnp.where(has_prev, prev_ref[0, 7:8, c0:c1], 0.0)
        n = jnp.where(has_next, next_ref[0, 0:1, c0:c1], 0.0)
        zp = pltpu.roll(z, 1, 0)
        zn = pltpu.roll(z, rows - 1, 0)
        zp = jnp.concatenate([jnp.where(row8 == 0, p, zp[:8]), zp[8:]], axis=0)
        zn = jnp.concatenate([zn[:rows - 8], jnp.where(row8 == 7, n, zn[rows - 8:])], axis=0)
        mu = mu_ref[:, c0:c1]
        return (1.0 - mu) * z + (0.5 * mu) * (zp + zn)

    dr = d_rwkv
    r_o[0] = shifted(0, dr)
    k = shifted(dr, 2 * dr)
    k_o[0] = k
    v = shifted(2 * dr, 3 * dr)
    v_o[0] = v.astype(BF16)

    kkv = k * kkw_ref[...]
    ones_bd = ones_ref[...]
    for j in range(dr // MXU_DIM):
        sl = slice(j * MXU_DIM, (j + 1) * MXU_DIM)
        x = kkv[:, sl]
        ss = _head_sum(x * x, ones_bd)
        kk_o[0, :, sl] = x * lax.rsqrt(jnp.maximum(ss, 1e-24))

    c = 3 * dr
    sm = shifted(c, c + 2 * DECAY_LORA + 2 * AAA_LORA + GATE_LORA)
    o = 0
    for d, lw_o in enumerate((lwf_o, lwb_o)):
        wd = jnp.tanh(sm[:, o:o + DECAY_LORA])
        o += DECAY_LORA
        xw = w0_ref[d:d + 1, :] + _mm(wd, w2_ref[d], passes=P_LORA_DECAY)
        u = -xw
        softplus = jnp.maximum(u, 0.0) + jnp.log1p(jnp.exp(-jnp.abs(u)))
        lw_o[0] = -jnp.exp(-softplus - 0.5)
    for d, a_o in enumerate((af_o, ab_o)):
        ad = sm[:, o:o + AAA_LORA]
        o += AAA_LORA
        a_o[0] = _sigmoid(a0_ref[d:d + 1, :] + _mm(ad, a2_ref[d], passes=P_LORA))
    gd = _sigmoid(sm[:, o:o + GATE_LORA])
    g_o[0] = _mm(gd, g2_ref[...], passes=P_LORA)

    for j in range(dr // MXU_DIM):
        sl = slice(j * MXU_DIM, (j + 1) * MXU_DIM)
        ka = ka_ref[:, sl]
        kj = k_o[0, :, sl]
        kd_sum = kj * (1.0 + (af_o[0, :, sl] - 1.0) * ka) + kj * (1.0 + (ab_o[0, :, sl] - 1.0) * ka)
        bonus = _head_sum(r_o[0, :, sl] * kd_sum * rk_ref[:, sl], ones_bd) * v[:, sl]
        bg_o[0, :, sl] = bonus * g_o[0, :, sl]


def _prep(proj3, mu_p, w0, w2, a0, a2, g2, k_k, k_a, r_k, ones_bd, d_rwkv, rw_cols_p):
    b, t, _ = proj3.shape
    tp = SEQ_TILE
    n_tiles = t // tp
    sub = tp // 8
    last8 = t // 8 - 1
    out_sd = jax.ShapeDtypeStruct((b, t, d_rwkv), F32)
    o_spec = pl.BlockSpec((1, tp, d_rwkv), lambda i, j: (i, j, 0))

    def full(a):
        nd = a.ndim
        return pl.BlockSpec(a.shape, lambda i, j: (0,) * nd)

    params = (mu_p, w0, w2, a0, a2, g2, k_k.reshape(1, d_rwkv), k_a.reshape(1, d_rwkv),
              r_k.reshape(1, d_rwkv), ones_bd)
    return pl.pallas_call(
        functools.partial(_prep_kernel, n_tiles=n_tiles, d_rwkv=d_rwkv),
        grid=(b, n_tiles),
        in_specs=[pl.BlockSpec((1, tp, rw_cols_p), lambda i, j: (i, j, 0)),
                  pl.BlockSpec((1, 8, rw_cols_p), lambda i, j: (i, jnp.maximum(j * sub - 1, 0), 0)),
                  pl.BlockSpec((1, 8, rw_cols_p), lambda i, j: (i, jnp.minimum((j + 1) * sub, last8), 0)),
                  ] + [full(a) for a in params],
        out_specs=[o_spec] * 10,
        out_shape=[out_sd, out_sd, jax.ShapeDtypeStruct((b, t, d_rwkv), BF16)] + [out_sd] * 7,
        compiler_params=_cparams(("parallel", "arbitrary")),
        name="rwkv_prep",
    )(proj3, proj3, proj3, *params)


def _block_diag_rhs(y, bd):
    yb = y.astype(BF16)
    return jnp.concatenate([yb] * HEADS_PER_GROUP, axis=0) * bd


def _mmc(x, y_bd):
    return _dot(x.astype(BF16), y_bd)


def _scan_masks(rev):
    c = CHUNK
    ti = lax.broadcasted_iota(jnp.int32, (c, c), 0)
    si = lax.broadcasted_iota(jnp.int32, (c, c), 1)
    tri = jnp.where((si >= ti) if rev else (si <= ti), 1.0, 0.0).astype(BF16)
    tc = lax.broadcasted_iota(jnp.int32, (c, MXU_DIM), 0)
    sc = lax.broadcasted_iota(jnp.int32, (c, MXU_DIM), 1) % c
    before = (sc > tc) if rev else (sc < tc)
    upto = (sc >= tc) if rev else (sc <= tc)
    return tri, before, upto


def _scan_chunk_local(out, refs, sl, dir_masks, blk_masks, ka, last_row, bd):
    r_ref, k_ref, v_ref, kk_ref, lw_ref, a_ref = refs
    tri, before, upto = dir_masks
    eye, same16, off32, off64 = blk_masks
    c = CHUNK
    kk = kk_ref[0, sl, :]
    lw = lw_ref[0, sl, :]
    a = a_ref[0, sl, :]
    lw0, lw1, lw2 = _split3(lw)
    cum = _dot(tri, lw0) + (_dot(tri, lw1) + _dot(tri, lw2))
    yield
    e_in = jnp.exp(cum)
    e_inv = jnp.exp(-cum)
    al = kk * jnp.exp(cum - lw)
    rh = r_ref[0, sl, :] * e_in
    be = (kk * a) * e_inv
    kap = (k_ref[0, sl, :] * (1.0 + (a - 1.0) * ka)) * e_inv
    ptot = e_in[last_row:last_row + 1, :]
    ar = jnp.concatenate([al, rh], axis=0).astype(BF16)
    ab = _dot(ar, _block_diag_rhs(be, bd), _NT)
    ak = _dot(ar, _block_diag_rhs(kap, bd), _NT)
    yield
    lm = jnp.where(before, ab[:c], 0.0)
    aak = jnp.where(before, ak[:c], 0.0)
    kq = jnp.where(upto, ak[c:], 0.0)
    out["bq"] = jnp.where(upto, ab[c:], 0.0).astype(BF16)
    ld = jnp.where(same16, lm, 0.0)
    p = eye - ld
    ld_bd = _block_diag_rhs(ld, bd)
    l2 = _mmc(ld, ld_bd)
    yield
    pl2 = _mmc(jnp.concatenate([p, l2], axis=0), _block_diag_rhs(l2, bd))
    yield
    p = p + pl2[:c]
    l4 = pl2[c:]
    pl4 = _mmc(jnp.concatenate([p, l4], axis=0), _block_diag_rhs(l4, bd))
    yield
    p = p + pl4[:c]
    p8 = _mmc(p, _block_diag_rhs(pl4[c:], bd))
    v = v_ref[0, sl, :]
    akv = _mmc(jnp.concatenate([aak, kq], axis=0), _block_diag_rhs(v, bd))
    aak_v = akv[:c]
    out["kq_v"] = akv[c:]
    yield
    p = p + p8
    for off in (off32, off64):
        lo = jnp.where(off, lm, 0.0)
        t1 = _mmc(lo, _block_diag_rhs(p, bd))
        yield
        t2 = _mmc(p, _block_diag_rhs(t1, bd))
        yield
        p = p - t2
    wt = _mmc(p, _block_diag_rhs(al, bd))
    out["ut"] = _mmc(p, _block_diag_rhs(aak_v, bd))
    yield
    out["wr"] = jnp.concatenate([wt, rh], axis=0).astype(BF16)
    out["v"] = v.astype(BF16)
    out["bkT"] = jnp.transpose(jnp.concatenate([be, kap], axis=0)).astype(BF16)
    out["ptot"] = ptot


def _scan_chain(items, m_scr, y_ref, bd, bd_mask):
    c = CHUNK
    for loc, sl in items:
        m = m_scr[...]
        zs = _dot(loc["wr"], m.astype(BF16))
        yield
        u = -(zs[:c] + loc["ut"])
        uv = jnp.concatenate([u.astype(BF16), loc["v"]], axis=0)
        dm = _dot(loc["bkT"], uv)
        yu = _mmc(loc["bq"], _block_diag_rhs(u, bd))
        yield
        y_ref[0, sl, :] = zs[c:] + loc["kq_v"] + yu
        pc = jnp.transpose(jnp.broadcast_to(loc["ptot"], (8, MXU_DIM)))
        m_scr[...] = jnp.broadcast_to(pc[:, 0:1], (MXU_DIM, MXU_DIM)) * (m + jnp.where(bd_mask, dm, 0.0))


def _run_lockstep(gens):
    gens = list(gens)
    while gens:
        alive = []
        for g in gens:
            try:
                next(g)
                alive.append(g)
            except StopIteration:
                pass
        gens = alive


def _scan_kernel(rf_ref, kf_ref, vf_ref, kkf_ref, lwf_ref, af_ref, rb_ref, kb_ref, vb_ref, kkb_ref, lwb_ref,
                 ab_ref, ka_ref, s0_ref, yf_ref, yb_ref, sfin_ref, mf_scr, mb_scr,
                 *, n_sub, n_blk):
    c = CHUNK
    blk = pl.program_id(2)

    @pl.when(blk == 0)
    def _():
        zero = jnp.zeros((HEAD, HEAD), F32)
        for d, scr in enumerate((mf_scr, mb_scr)):
            for h in range(HEADS_PER_GROUP):
                blocks = [zero] * HEADS_PER_GROUP
                blocks[h] = jnp.transpose(s0_ref[0, d, h])
                scr[h * HEAD:(h + 1) * HEAD, :] = jnp.concatenate(blocks, axis=1)

    tc = lax.broadcasted_iota(jnp.int32, (c, MXU_DIM), 0)
    sc = lax.broadcasted_iota(jnp.int32, (c, MXU_DIM), 1) % c
    same16 = (sc // 16) == (tc // 16)
    same32 = (sc // 32) == (tc // 32)
    blk_masks = (jnp.where(sc == tc, 1.0, 0.0), same16,
                 jnp.logical_and(same32, jnp.logical_not(same16)), jnp.logical_not(same32))
    bd_mask = (lax.broadcasted_iota(jnp.int32, (MXU_DIM, MXU_DIM), 0) // HEAD) == \
              (lax.broadcasted_iota(jnp.int32, (MXU_DIM, MXU_DIM), 1) // HEAD)
    bd = jnp.where(bd_mask, 1.0, 0.0).astype(BF16)
    ka = ka_ref[...]

    refs_f = (rf_ref, kf_ref, vf_ref, kkf_ref, lwf_ref, af_ref)
    refs_b = (rb_ref, kb_ref, vb_ref, kkb_ref, lwb_ref, ab_ref)
    masks_f = _scan_masks(False)
    masks_b = _scan_masks(True)
    slices = [slice(j * c, (j + 1) * c) for j in range(n_sub)]
    items_f = [({}, sl) for sl in slices]
    items_b = [({}, sl) for sl in reversed(slices)]
    chains = []
    for w0 in range(0, n_sub, SCAN_WAVE):
        wave_f = items_f[w0:w0 + SCAN_WAVE]
        wave_b = items_b[w0:w0 + SCAN_WAVE]
        gens = []
        for (of, sf), (ob, sb) in zip(wave_f, wave_b):
            gens.append(_scan_chunk_local(of, refs_f, sf, masks_f, blk_masks, ka, c - 1, bd))
            gens.append(_scan_chunk_local(ob, refs_b, sb, masks_b, blk_masks, ka, 0, bd))
        _run_lockstep(gens + chains)
        chains = [_scan_chain(wave_f, mf_scr, yf_ref, bd, bd_mask),
                  _scan_chain(wave_b, mb_scr, yb_ref, bd, bd_mask)]
    _run_lockstep(chains)

    @pl.when(blk == n_blk - 1)
    def _():
        for d, scr in enumerate((mf_scr, mb_scr)):
            for h in range(HEADS_PER_GROUP):
                rows = scr[h * HEAD:(h + 1) * HEAD, :]
                sfin_ref[0, d, h] = jnp.transpose(rows[:, h * HEAD:(h + 1) * HEAD])


def _scan(r, k, v, kk, lwf, af, lwb, ab, k_a, s0):
    b, t, d_rwkv = r.shape
    groups = d_rwkv // MXU_DIM
    tb = min(SCAN_BLOCK, t)
    n_blk = t // tb
    n_sub = tb // CHUNK
    tok_f = pl.BlockSpec((1, tb, MXU_DIM), lambda i, g, j: (i, j, g))
    tok_b = pl.BlockSpec((1, tb, MXU_DIM), lambda i, g, j: (i, n_blk - 1 - j, g))
    st = pl.BlockSpec((1, 2, HEADS_PER_GROUP, HEAD, HEAD), lambda i, g, j: (i, 0, g, 0, 0))
    y_sd = jax.ShapeDtypeStruct((b, t, d_rwkv), F32)
    return pl.pallas_call(
        functools.partial(_scan_kernel, n_sub=n_sub, n_blk=n_blk),
        grid=(b, groups, n_blk),
        in_specs=[tok_f] * 6 + [tok_b] * 6 + [pl.BlockSpec((1, MXU_DIM), lambda i, g, j: (0, g)), st],
        out_specs=[tok_f, tok_b, st],
        out_shape=[y_sd, y_sd, jax.ShapeDtypeStruct(s0.shape, F32)],
        scratch_shapes=[pltpu.VMEM((MXU_DIM, MXU_DIM), F32), pltpu.VMEM((MXU_DIM, MXU_DIM), F32)],
        compiler_params=_cparams(("parallel", "parallel", "arbitrary")),
        name="scan",
    )(r, k, v, kk, lwf, af, r, k, v, kk, lwb, ab, k_a.reshape(1, d_rwkv), s0)


def _post_kernel(yf_ref, yb_ref, g_ref, bg_ref, lg_ref, lb_ref, ones_ref, o_ref):
    ones_bd = ones_ref[...]
    inv_n = 1.0 / HEAD
    y = yf_ref[0] + yb_ref[0]
    mu = _head_sum(y, ones_bd) * inv_n
    d = y - mu
    var = _head_sum(d * d, ones_bd) * inv_n
    yn = d * lax.rsqrt(var + GN_EPS) * lg_ref[...] + lb_ref[...]
    o_ref[0] = (yn * g_ref[0] + bg_ref[0]).astype(BF16)


def _post(yf, yb, g, bg, lnx_g, lnx_b, ones_bd):
    b, t, d_rwkv = yf.shape
    tp = min(512, t)
    tok = pl.BlockSpec((1, tp, MXU_DIM), lambda i, j, q: (i, j, q))
    par = pl.BlockSpec((1, MXU_DIM), lambda i, j, q: (0, q))
    return pl.pallas_call(
        _post_kernel,
        grid=(b, t // tp, d_rwkv // MXU_DIM),
        in_specs=[tok] * 4 + [par] * 2 + [pl.BlockSpec((MXU_DIM, MXU_DIM), lambda i, j, q: (0, 0))],
        out_specs=tok,
        out_shape=jax.ShapeDtypeStruct((b, t, d_rwkv), BF16),
        compiler_params=_cparams(("parallel", "parallel", "arbitrary")),
        name="rwkv_post",
    )(yf, yb, g, bg, lnx_g.reshape(1, d_rwkv), lnx_b.reshape(1, d_rwkv), ones_bd)


def _window_bounds(n, w):
    t = np.arange(n)
    lo = np.clip(t - w // 2, 0, n)
    hi = np.clip(t - w // 2 + w, 0, n)
    return lo, hi


def _window_matrix(n, w):
    lo, hi = _window_bounds(n, w)
    s = np.arange(n)[None, :]
    m = ((s >= lo[:, None]) & (s < hi[:, None])).astype(np.float32)
    return m / (hi - lo)[:, None].astype(np.float32)


def _pool_ctx_kernel(z_ref, band_ref, pw_ref, ps_ref, o_ref):
    z = z_ref[0]
    m = _dot(band_ref[0], z, precision=lax.Precision.HIGHEST)
    d = (m - z).astype(BF16)
    o_ref[0] = (_dot(d, pw_ref[0]) * ps_ref[...]).astype(BF16)


def _pool_ctx(proj3, col0, pool_w_bf16, pool_scale):
    b, t, _ = proj3.shape
    ng, pg, _ = pool_w_bf16.shape
    band = jnp.asarray(np.stack([_window_matrix(t, w) for w in POOL_WINDOWS]))
    cb = col0 // pg
    return pl.pallas_call(
        _pool_ctx_kernel,
        grid=(b, ng),
        in_specs=[pl.BlockSpec((1, t, pg), lambda i, q: (i, 0, cb + q)),
                  pl.BlockSpec((1, t, t), lambda i, q: (q, 0, 0)),
                  pl.BlockSpec((1, pg, pg), lambda i, q: (q, 0, 0)),
                  pl.BlockSpec((1, pg), lambda i, q: (0, q))],
        out_specs=pl.BlockSpec((1, t, pg), lambda i, q: (i, 0, q)),
        out_shape=jax.ShapeDtypeStruct((b, t, ng * pg), BF16),
        compiler_params=_cparams(("parallel", "arbitrary")),
        name="pool_ctx",
    )(proj3, band, pool_w_bf16, pool_scale.reshape(1, ng * pg))


def _pool_lat_kernel(z_ref, band_ref, pw_ref, ps_ref, o_ref, m1_ref, cs_ref, *, n_rows):
    tile = band_ref.shape[1]
    n_tok = n_rows * GRID_W
    band = band_ref[0]
    for i in range(n_tok // tile):
        sl = slice(i * tile, (i + 1) * tile)
        m1_ref[sl, :] = _dot(band, z_ref[0, sl, :], precision=lax.Precision.HIGHEST)
    cs_ref[0:GRID_W, :] = jnp.zeros((GRID_W, cs_ref.shape[1]), F32)
    for rr in range(n_rows):
        cs_ref[(rr + 1) * GRID_W:(rr + 2) * GRID_W, :] = (
            cs_ref[rr * GRID_W:(rr + 1) * GRID_W, :] + m1_ref[rr * GRID_W:(rr + 1) * GRID_W, :])
    group = pl.program_id(1)
    for gi, w in enumerate(POOL_WINDOWS):
        @pl.when(group == gi)
        def _(w=w):
            lo, hi = _window_bounds(n_rows, w)
            for rr in range(n_rows):
                l, h = int(lo[rr]), int(hi[rr])
                m2 = (cs_ref[h * GRID_W:(h + 1) * GRID_W, :]
                      - cs_ref[l * GRID_W:(l + 1) * GRID_W, :]) / float(h - l)
                sl = slice(rr * GRID_W, (rr + 1) * GRID_W)
                m1_ref[sl, :] = m2 - z_ref[0, sl, :]
    pw = pw_ref[0]
    ps = ps_ref[...]
    for i in range(n_tok // tile):
        sl = slice(i * tile, (i + 1) * tile)
        o_ref[0, sl, :] = (_dot(m1_ref[sl, :].astype(BF16), pw) * ps).astype(BF16)


def _pool_lat(proj3, col0, pool_w_bf16, pool_scale):
    b, t, _ = proj3.shape
    ng, pg, _ = pool_w_bf16.shape
    n_rows = t // GRID_W
    tile = MXU_DIM
    band = jnp.asarray(np.stack(
        [np.kron(np.eye(tile // GRID_W, dtype=np.float32), _window_matrix(GRID_W, w)) for w in POOL_WINDOWS]))
    cb = col0 // pg
    return pl.pallas_call(
        functools.partial(_pool_lat_kernel, n_rows=n_rows),
        grid=(b, ng),
        in_specs=[pl.BlockSpec((1, t, pg), lambda i, q: (i, 0, cb + q)),
                  pl.BlockSpec((1, tile, tile), lambda i, q: (q, 0, 0)),
                  pl.BlockSpec((1, pg, pg), lambda i, q: (q, 0, 0)),
                  pl.BlockSpec((1, pg), lambda i, q: (0, q))],
        out_specs=pl.BlockSpec((1, t, pg), lambda i, q: (i, 0, q)),
        out_shape=jax.ShapeDtypeStruct((b, t, ng * pg), BF16),
        scratch_shapes=[pltpu.VMEM((t, pg), F32), pltpu.VMEM((t + GRID_W, pg), F32)],
        compiler_params=_cparams(("parallel", "arbitrary")),
        name="pool_lat",
    )(proj3, band, pool_w_bf16, pool_scale.reshape(1, ng * pg))


ROW_SPLIT = 2


def _resident(shape):
    nd = len(shape)
    return pl.BlockSpec(shape, lambda i: (0,) * nd, pipeline_mode=pl.Buffered(1))


def _up_out_kernel(ya_ref, yb_ref, ga_ref, gb_ref, x_ref, mod_ref, g_ref, wa_ref, wb_ref, wo_ref,
                   x1_ref, h2_ref):
    rows = x_ref.shape[0] // ROW_SPLIT
    for part in range(ROW_SPLIT):
        sl = slice(part * rows, (part + 1) * rows)
        ua = _dot(ya_ref[sl, :], wa_ref[...])
        ub = _dot(yb_ref[sl, :], wb_ref[...])
        ga = ga_ref[sl, :].astype(F32)
        gb = gb_ref[sl, :].astype(F32)
        merged = (_sigmoid(ga) * ua + _sigmoid(gb) * ub).astype(BF16)
        x1 = x_ref[sl, :] + mod_ref[0, 2:3, :] * _dot(merged, wo_ref[...])
        x1_ref[sl, :] = x1
        y = x1 * lax.rsqrt(jnp.mean(x1 * x1, axis=-1, keepdims=True) + NORM_EPS)
        h2 = (y * g_ref[...]) * (1.0 + mod_ref[0, 4:5, :]) + mod_ref[0, 3:4, :]
        h2_ref[sl, :] = h2.astype(BF16)


def _up_out(ya2, yb2, gates, wa, wb, w_out, x2, mod3, tiles_per_mod_tokens, norm2_g, tm=256):
    m, d = x2.shape
    tm = min(tm, m)
    ka, kb = ya2.shape[1], yb2.shape[1]
    tpm = tiles_per_mod_tokens // tm
    row = lambda width, col: pl.BlockSpec((tm, width), lambda i: (i, col))
    return pl.pallas_call(
        _up_out_kernel,
        grid=(m // tm,),
        in_specs=[row(ka, 0), row(kb, 0), row(d, 0), row(d, 1), row(d, 0),
                  pl.BlockSpec((1, 6, d), lambda i: (i // tpm, 0, 0)),
                  _resident((1, d)), _resident((ka, d)), _resident((kb, d)), _resident((d, d))],
        out_specs=[row(d, 0), row(d, 0)],
        out_shape=[jax.ShapeDtypeStruct((m, d), F32), jax.ShapeDtypeStruct((m, d), BF16)],
        compiler_params=_cparams(("parallel",)),
        name="up_out",
    )(ya2, yb2, gates, gates, x2, mod3, norm2_g.reshape(1, d), wa, wb, w_out)


def _ffn1_kernel(h_ref, w1_ref, w3_ref, o_ref):
    h = h_ref[...]
    u1 = _dot(h, w1_ref[...])
    u3 = _dot(h, w3_ref[...])
    o_ref[...] = (u1 * _sigmoid(u1) * u3).astype(BF16)


def _ffn1(h2, w13, tm=1024, tn=PROJ_TN):
    m, d = h2.shape
    tm = min(tm, m)
    d_ff = w13.shape[1] // 2
    nj = d_ff // tn
    return pl.pallas_call(
        _ffn1_kernel,
        grid=(m // tm, nj),
        in_specs=[pl.BlockSpec((tm, d), lambda i, j: (i, 0)),
                  pl.BlockSpec((d, tn), lambda i, j: (0, j)),
                  pl.BlockSpec((d, tn), lambda i, j: (0, nj + j))],
        out_specs=pl.BlockSpec((tm, tn), lambda i, j: (i, j)),
        out_shape=jax.ShapeDtypeStruct((m, d_ff), BF16),
        compiler_params=_cparams(("parallel", "arbitrary")),
        name="ffn1",
    )(h2, w13, w13)


def _ffn2_kernel(a_ref, x_ref, mod_ref, g_ref, w_ref, o_ref):
    rows = x_ref.shape[0] // ROW_SPLIT
    for part in range(ROW_SPLIT):
        sl = slice(part * rows, (part + 1) * rows)
        x2 = x_ref[sl, :] + mod_ref[0, 5:6, :] * _dot(a_ref[sl, :], w_ref[...])
        y = x2 * lax.rsqrt(jnp.mean(x2 * x2, axis=-1, keepdims=True) + NORM_EPS)
        o_ref[sl, :] = y * g_ref[...]


def _ffn2(act, w2, x1, mod3, tiles_per_mod_tokens, final_g, tm=256):
    m, d = x1.shape
    tm = min(tm, m)
    d_ff = act.shape[1]
    tpm = tiles_per_mod_tokens // tm
    return pl.pallas_call(
        _ffn2_kernel,
        grid=(m // tm,),
        in_specs=[pl.BlockSpec((tm, d_ff), lambda i: (i, 0)),
                  pl.BlockSpec((tm, d), lambda i: (i, 0)),
                  pl.BlockSpec((1, 6, d), lambda i: (i // tpm, 0, 0)),
                  _resident((1, d)), _resident((d_ff, d))],
        out_specs=pl.BlockSpec((tm, d), lambda i: (i, 0)),
        out_shape=jax.ShapeDtypeStruct((m, d), F32),
        compiler_params=_cparams(("parallel",)),
        name="ffn2",
    )(act, x1, mod3, final_g.reshape(1, d), w2)


def _trunk(x, mod, s0, latent, p):
    b, t, d = x.shape
    m = b * t
    d_rwkv = p["d_rwkv"]
    d_pool = d - d_rwkv
    heads = d_rwkv // HEAD
    x2 = x.reshape(m, d)
    nb = mod.shape[0]
    mod3 = mod.reshape(nb, 6, d)
    tokens_per_mod = t if nb == b else m

    proj, gates = _in_proj(x2, p["norm1_g"], mod3, tokens_per_mod, p["w_in"], p["rw_cols_p"] + d_pool)
    proj3 = proj.reshape(b, t, proj.shape[1])

    r, k, v, kk, lwf, lwb, af, ab, g, bg = _prep(
        proj3, p["mu"], p["w0"], p["w2"], p["a0"], p["a2"], p["g2"], p["k_k"], p["k_a"], p["r_k"],
        p["ones_bd"], d_rwkv, p["rw_cols_p"])

    if s0 is None:
        s0 = jnp.zeros((b, 2, heads, HEAD, HEAD), F32)
    yf, yb, s_fin = _scan(r, k, v, kk, lwf, af, lwb, ab, p["k_a"], s0)
    ya = _post(yf, yb, g, bg, p["lnx_g"], p["lnx_b"], p["ones_bd"])

    pool_col0 = p["rw_cols_p"]
    if latent:
        yb_pool = _pool_lat(proj3, pool_col0, p["pool_w"], p["pool_scale"])
    else:
        yb_pool = _pool_ctx(proj3, pool_col0, p["pool_w"], p["pool_scale"])

    x1, h2 = _up_out(ya.reshape(m, d_rwkv), yb_pool.reshape(m, d_pool), gates, p["w_up_a"], p["w_up_b"],
                     p["w_out"], x2, mod3, tokens_per_mod, p["norm2_g"])
    act = _ffn1(h2, p["ffn_w13"])
    y = _ffn2(act, p["ffn_w2"], x1, mod3, tokens_per_mod, p["final_g"])
    return y.reshape(b, t, d), s_fin


def kernel(x_prompt, x_sample, c, state_rwkv, c_ctx, w_mod, b_mod, norm1_g, w_in, shift_mu, w0, w2, a0, a2, g2,
           k_k, k_a, r_k, lnx_g, lnx_b, w_up_a, pool_w, pool_scale, w_up_b, w_out, norm2_g, ffn_w13, ffn_w2,
           final_g):
    depth = w_in.shape[0]
    assert depth == 1, "final norm is fused into the last layer: single-layer trunk only"
    d = x_prompt.shape[-1]
    d_rwkv = w_up_a.shape[1]
    rw_cols = shift_mu.shape[1]
    rw_cols_p = -(-rw_cols // PROJ_TN) * PROJ_TN
    assert x_prompt.shape[1] == SEQ_TILE and x_sample.shape[1] % SEQ_TILE == 0

    l = 0
    w_in_l = w_in[l].astype(BF16)
    w_in_p = jnp.concatenate(
        [w_in_l[:, :rw_cols], jnp.zeros((d, rw_cols_p - rw_cols), BF16), w_in_l[:, rw_cols:]], axis=1)
    ones_bd = jnp.asarray(np.kron(np.eye(HEADS_PER_GROUP, dtype=np.float32),
                                  np.ones((HEAD, HEAD), np.float32))).astype(BF16)
    p = {
        "d_rwkv": d_rwkv, "rw_cols_p": rw_cols_p,
        "norm1_g": norm1_g[l], "w_in": w_in_p,
        "mu": jnp.pad(shift_mu[l], (0, rw_cols_p - rw_cols)).reshape(1, rw_cols_p),
        "w0": w0[l], "w2": w2[l], "a0": a0[l], "a2": a2[l], "g2": g2[l],
        "k_k": k_k[l], "k_a": k_a[l], "r_k": r_k[l].reshape(-1), "lnx_g": lnx_g[l], "lnx_b": lnx_b[l],
        "ones_bd": ones_bd,
        "w_up_a": w_up_a[l].astype(BF16), "w_up_b": w_up_b[l].astype(BF16),
        "pool_w": pool_w[l].astype(BF16), "pool_scale": pool_scale[l],
        "w_out": w_out[l].astype(BF16), "norm2_g": norm2_g[l],
        "ffn_w13": ffn_w13[l].astype(BF16), "ffn_w2": ffn_w2[l].astype(BF16), "final_g": final_g,
    }
    cond = jnp.concatenate([c_ctx[None, :], c], axis=0)
    mod = _mod(cond, w_mod[l], b_mod[l])
    y_prompt, s_ctx = _trunk(x_prompt, mod[:1], None, False, p)
    y_sample, _ = _trunk(x_sample, mod[1:], state_rwkv[:, l], True, p)
    new_state = s_ctx[:, None].astype(x_prompt.dtype)
    return (y_prompt, y_sample, new_state)
```

```python
import functools

import numpy as np
import jax
import jax.numpy as jnp
from jax import lax
from jax.experimental import pallas as pl
from jax.experimental.pallas import tpu as pltpu

F32 = jnp.float32
BF16 = jnp.bfloat16

HEAD = 64
POOL_WINDOWS = (2, 4, 8, 16)
GRID_W = 64
DECAY_LORA = 64
AAA_LORA = 64
GATE_LORA = 128
NORM_EPS = 1e-6
GN_EPS = 64e-5

LANE_TILE = 128
MXU_DIM = 256
HEADS_PER_GROUP = MXU_DIM // HEAD
VMEM_LIMIT = 56 * 1024 * 1024

CHUNK = 64
SEQ_TILE = 256
PROJ_TN = 512
SCAN_BLOCK = 1024
SCAN_WAVE = 4


def _cparams(sem):
    return pltpu.CompilerParams(dimension_semantics=sem, vmem_limit_bytes=VMEM_LIMIT)


def _dot(a, b, dims=(((1,), (0,)), ((), ())), precision=None):
    return lax.dot_general(a, b, dims, precision=precision, preferred_element_type=F32)


_NN = (((1,), (0,)), ((), ()))
_NT = (((1,), (1,)), ((), ()))


def _split2(x):
    hi = x.astype(BF16)
    lo = (x - hi.astype(F32)).astype(BF16)
    return hi, lo


def _split3(x):
    hi = x.astype(BF16)
    r1 = x - hi.astype(F32)
    mid = r1.astype(BF16)
    lo = (r1 - mid.astype(F32)).astype(BF16)
    return hi, mid, lo


def _mm(a, b, passes):
    if passes == 1:
        return _dot(a.astype(BF16), b.astype(BF16))
    assert passes == 3
    ah, al = _split2(a)
    bh, bl = _split2(b)
    return _dot(ah, bh) + (_dot(ah, bl) + _dot(al, bh))


def _mm_exact_rhs(a, b_bf16):
    a0, a1 = _split2(a)
    return _dot(a0, b_bf16) + _dot(a1, b_bf16)


def _sigmoid(x):
    return 1.0 / (1.0 + jnp.exp(-x))


def _mod_kernel(c_ref, w_ref, b_ref, o_ref):
    c = c_ref[...]
    s = (c * _sigmoid(c)).astype(BF16)
    o_ref[...] = _dot(s, w_ref[...].astype(BF16)) + b_ref[...]


def _mod(cond, w_mod, b_mod):
    nb, d = cond.shape
    rows = -(-nb // 16) * 16
    cond_p = jnp.pad(cond, ((0, rows - nb), (0, 0)))
    n = w_mod.shape[1]
    tn = 1024
    out = pl.pallas_call(
        _mod_kernel,
        grid=(n // tn,),
        in_specs=[pl.BlockSpec((rows, d), lambda j: (0, 0)),
                  pl.BlockSpec((d, tn), lambda j: (0, j)),
                  pl.BlockSpec((1, tn), lambda j: (0, j))],
        out_specs=pl.BlockSpec((rows, tn), lambda j: (0, j)),
        out_shape=jax.ShapeDtypeStruct((rows, n), F32),
        compiler_params=_cparams(("arbitrary",)),
        name="mod",
    )(cond_p, w_mod, b_mod.reshape(1, n))
    return out[:nb]


NORM_ROWS = 16


def _in_kernel(x_ref, g_ref, mod_ref, w_ref, om_ref, og_ref, h_ref, *, nj_main):
    j = pl.program_id(1)

    @pl.when(j == 0)
    def _():
        gain = g_ref[...]
        scale = 1.0 + mod_ref[0, 1:2, :]
        shift = mod_ref[0, 0:1, :]

        def norm_rows(i, carry):
            rows = pl.ds(pl.multiple_of(i * NORM_ROWS, NORM_ROWS), NORM_ROWS)
            x = x_ref[rows, :]
            y = x * lax.rsqrt(jnp.mean(x * x, axis=-1, keepdims=True) + NORM_EPS)
            h_ref[rows, :] = ((y * gain) * scale + shift).astype(BF16)
            return carry

        lax.fori_loop(0, x_ref.shape[0] // NORM_ROWS, norm_rows, 0, unroll=8)

    @pl.when(j < nj_main)
    def _():
        om_ref[...] = _dot(h_ref[...], w_ref[...])

    @pl.when(j >= nj_main)
    def _():
        og_ref[...] = _dot(h_ref[...], w_ref[...])


def _in_proj(x2, norm_g, mod3, tiles_per_mod_tokens, w_bf16, n_main, tm=1024, tn=PROJ_TN):
    m, d = x2.shape
    tm = min(tm, m)
    n = w_bf16.shape[1]
    tpm = tiles_per_mod_tokens // tm
    nj_main = n_main // tn
    return pl.pallas_call(
        functools.partial(_in_kernel, nj_main=nj_main),
        grid=(m // tm, n // tn),
        in_specs=[pl.BlockSpec((tm, d), lambda i, j: (i, 0)),
                  pl.BlockSpec((1, d), lambda i, j: (0, 0)),
                  pl.BlockSpec((1, 6, d), lambda i, j: (i // tpm, 0, 0)),
                  pl.BlockSpec((d, tn), lambda i, j: (0, j))],
        out_specs=[pl.BlockSpec((tm, tn), lambda i, j: (i, jnp.minimum(j, nj_main - 1))),
                   pl.BlockSpec((tm, tn), lambda i, j: (i, jnp.maximum(j - nj_main, 0)))],
        out_shape=[jax.ShapeDtypeStruct((m, n_main), F32), jax.ShapeDtypeStruct((m, n - n_main), F32)],
        scratch_shapes=[pltpu.VMEM((tm, d), BF16)],
        compiler_params=_cparams(("parallel", "arbitrary")),
        name="in_proj",
    )(x2, norm_g.reshape(1, d), mod3, w_bf16)


P_LORA_DECAY = 3
P_LORA = 1
DECAY_GAIN = float(np.exp(-0.5))


def _head_sum(x, ones_bd):
    return _mm_exact_rhs(x, ones_bd)


def _prep_kernel(main_ref, prev_ref, next_ref, mu_ref, w0_ref, w2_ref, a0_ref, a2_ref, g2_ref,
                 kkw_ref, ka_ref, rk_ref, ones_ref, r_o, k_o, v_o, kk_o, lwf_o, lwb_o, af_o, ab_o, g_o, bg_o,
                 *, n_tiles, d_rwkv):
    t = pl.program_id(1)
    rows = main_ref.shape[1]
    row8 = lax.broadcasted_iota(jnp.int32, (8, 1), 0)
    has_prev = t > 0
    has_next = t < n_tiles - 1

    def shifted(c0, c1):
        z = main_ref[0, :, c0:c1]
        p = jnp.where(has_prev, prev_ref[0, 7:8, c0:c1], 0.0)
        n = jnp.where(has_next, next_ref[0, 0:1, c0:c1], 0.0)
        zp = pltpu.roll(z, 1, 0)
        zn = pltpu.roll(z, rows - 1, 0)
        zp = jnp.concatenate([jnp.where(row8 == 0, p, zp[:8]), zp[8:]], axis=0)
        zn = jnp.concatenate([zn[:rows - 8], jnp.where(row8 == 7, n, zn[rows - 8:])], axis=0)
        mu = mu_ref[:, c0:c1]
        return (1.0 - mu) * z + (0.5 * mu) * (zp + zn)

    dr = d_rwkv
    r_o[0] = shifted(0, dr)
    k = shifted(dr, 2 * dr)
    k_o[0] = k
    v = shifted(2 * dr, 3 * dr)
    v_o[0] = v.astype(BF16)

    kkv = k * kkw_ref[...]
    ones_bd = ones_ref[...]
    for j in range(dr // MXU_DIM):
        sl = slice(j * MXU_DIM, (j + 1) * MXU_DIM)
        x = kkv[:, sl]
        ss = _head_sum(x * x, ones_bd)
        kk_o[0, :, sl] = x * lax.rsqrt(jnp.maximum(ss, 1e-24))

    c = 3 * dr
    sm = shifted(c, c + 2 * DECAY_LORA + 2 * AAA_LORA + GATE_LORA)
    o = 0
    for d, lw_o in enumerate((lwf_o, lwb_o)):
        wd = jnp.tanh(sm[:, o:o + DECAY_LORA])
        o += DECAY_LORA
        xw = w0_ref[d:d + 1, :] + _mm(wd, w2_ref[d], passes=P_LORA_DECAY)
        lw_o[0] = -DECAY_GAIN * _sigmoid(xw)
    for d, a_o in enumerate((af_o, ab_o)):
        ad = sm[:, o:o + AAA_LORA]
        o += AAA_LORA
        a_o[0] = _sigmoid(a0_ref[d:d + 1, :] + _mm(ad, a2_ref[d], passes=P_LORA))
    gd = _sigmoid(sm[:, o:o + GATE_LORA])
    g_o[0] = _mm(gd, g2_ref[...], passes=P_LORA)

    for j in range(dr // MXU_DIM):
        sl = slice(j * MXU_DIM, (j + 1) * MXU_DIM)
        ka = ka_ref[:, sl]
        kj = k_o[0, :, sl]
        kd_sum = kj * (1.0 + (af_o[0, :, sl] - 1.0) * ka) + kj * (1.0 + (ab_o[0, :, sl] - 1.0) * ka)
        bonus = _head_sum(r_o[0, :, sl] * kd_sum * rk_ref[:, sl], ones_bd) * v[:, sl]
        bg_o[0, :, sl] = bonus * g_o[0, :, sl]


def _prep(proj3, mu_p, w0, w2, a0, a2, g2, k_k, k_a, r_k, ones_bd, d_rwkv, rw_cols_p):
    b, t, _ = proj3.shape
    tp = SEQ_TILE
    n_tiles = t // tp
    sub = tp // 8
    last8 = t // 8 - 1
    out_sd = jax.ShapeDtypeStruct((b, t, d_rwkv), F32)
    o_spec = pl.BlockSpec((1, tp, d_rwkv), lambda i, j: (i, j, 0))

    def full(a):
        nd = a.ndim
        return pl.BlockSpec(a.shape, lambda i, j: (0,) * nd)

    params = (mu_p, w0, w2, a0, a2, g2, k_k.reshape(1, d_rwkv), k_a.reshape(1, d_rwkv),
              r_k.reshape(1, d_rwkv), ones_bd)
    return pl.pallas_call(
        functools.partial(_prep_kernel, n_tiles=n_tiles, d_rwkv=d_rwkv),
        grid=(b, n_tiles),
        in_specs=[pl.BlockSpec((1, tp, rw_cols_p), lambda i, j: (i, j, 0)),
                  pl.BlockSpec((1, 8, rw_cols_p), lambda i, j: (i, jnp.maximum(j * sub - 1, 0), 0)),
                  pl.BlockSpec((1, 8, rw_cols_p), lambda i, j: (i, jnp.minimum((j + 1) * sub, last8), 0)),
                  ] + [full(a) for a in params],
        out_specs=[o_spec] * 10,
        out_shape=[out_sd, out_sd, jax.ShapeDtypeStruct((b, t, d_rwkv), BF16)] + [out_sd] * 7,
        compiler_params=_cparams(("parallel", "arbitrary")),
        name="rwkv_prep",
    )(proj3, proj3, proj3, *params)


def _block_diag_rhs(y, bd):
    yb = y.astype(BF16)
    per_tile = LANE_TILE // HEAD
    zero = jnp.zeros((y.shape[0], LANE_TILE), BF16)
    rows = []
    for h in range(HEADS_PER_GROUP):
        t = h // per_tile
        tiles = [zero] * (MXU_DIM // LANE_TILE)
        tiles[t] = yb[:, t * LANE_TILE:(t + 1) * LANE_TILE] * bd[h % per_tile]
        rows.append(jnp.concatenate(tiles, axis=1))
    return jnp.concatenate(rows, axis=0)


def _mmc(x, y_bd):
    return _dot(x.astype(BF16), y_bd)


def _scan_masks(rev):
    c = CHUNK
    ti = lax.broadcasted_iota(jnp.int32, (c, c), 0)
    si = lax.broadcasted_iota(jnp.int32, (c, c), 1)
    tri = jnp.where((si >= ti) if rev else (si <= ti), 1.0, 0.0).astype(BF16)
    tc = lax.broadcasted_iota(jnp.int32, (c, MXU_DIM), 0)
    sc = lax.broadcasted_iota(jnp.int32, (c, MXU_DIM), 1) % c
    before = (sc > tc) if rev else (sc < tc)
    upto = (sc >= tc) if rev else (sc <= tc)
    return tri, before, upto


def _scan_chunk_local(out, refs, sl, dir_masks, blk_masks, ka, last_row, bd):
    r_ref, k_ref, v_ref, kk_ref, lw_ref, a_ref = refs
    tri, before, upto = dir_masks
    eye, same16, off32, off64 = blk_masks
    c = CHUNK
    kk = kk_ref[0, sl, :]
    lw = lw_ref[0, sl, :]
    a = a_ref[0, sl, :]
    lw0, lw1, lw2 = _split3(lw)
    cum = _dot(tri, lw0) + (_dot(tri, lw1) + _dot(tri, lw2))
    yield
    e_in = jnp.exp(cum)
    e_inv = jnp.exp(-cum)
    al = kk * jnp.exp(cum - lw)
    rh = r_ref[0, sl, :] * e_in
    be = (kk * a) * e_inv
    kap = (k_ref[0, sl, :] * (1.0 + (a - 1.0) * ka)) * e_inv
    ptot = e_in[last_row:last_row + 1, :]
    ar = jnp.concatenate([al, rh], axis=0).astype(BF16)
    ab = _dot(ar, _block_diag_rhs(be, bd), _NT)
    ak = _dot(ar, _block_diag_rhs(kap, bd), _NT)
    yield
    lm = jnp.where(before, ab[:c], 0.0)
    aak = jnp.where(before, ak[:c], 0.0)
    kq = jnp.where(upto, ak[c:], 0.0)
    out["bq"] = jnp.where(upto, ab[c:], 0.0).astype(BF16)
    ld = jnp.where(same16, lm, 0.0)
    p = eye - ld
    ld_bd = _block_diag_rhs(ld, bd)
    l2 = _mmc(ld, ld_bd)
    yield
    pl2 = _mmc(jnp.concatenate([p, l2], axis=0), _block_diag_rhs(l2, bd))
    yield
    p = p + pl2[:c]
    l4 = pl2[c:]
    pl4 = _mmc(jnp.concatenate([p, l4], axis=0), _block_diag_rhs(l4, bd))
    yield
    p = p + pl4[:c]
    p8 = _mmc(p, _block_diag_rhs(pl4[c:], bd))
    v = v_ref[0, sl, :]
    akv = _mmc(jnp.concatenate([aak, kq], axis=0), _block_diag_rhs(v, bd))
    aak_v = akv[:c]
    out["kq_v"] = akv[c:]
    yield
    p = p + p8
    for off in (off32, off64):
        lo = jnp.where(off, lm, 0.0)
        t1 = _mmc(lo, _block_diag_rhs(p, bd))
        yield
        t2 = _mmc(p, _block_diag_rhs(t1, bd))
        yield
        p = p - t2
    wt = _mmc(p, _block_diag_rhs(al, bd))
    out["ut"] = _mmc(p, _block_diag_rhs(aak_v, bd))
    yield
    out["wr"] = jnp.concatenate([wt, rh], axis=0).astype(BF16)
    out["v"] = v.astype(BF16)
    out["bkT"] = jnp.transpose(jnp.concatenate([be, kap], axis=0)).astype(BF16)
    out["ptot"] = ptot


def _scan_chain(items, m_scr, y_ref, bd, bd_mask):
    c = CHUNK
    for loc, sl in items:
        m = m_scr[...]
        zs = _dot(loc["wr"], m.astype(BF16))
        yield
        u = -(zs[:c] + loc["ut"])
        uv = jnp.concatenate([u.astype(BF16), loc["v"]], axis=0)
        dm = _dot(loc["bkT"], uv)
        yu = _mmc(loc["bq"], _block_diag_rhs(u, bd))
        yield
        y_ref[0, sl, :] = zs[c:] + loc["kq_v"] + yu
        pc = jnp.transpose(jnp.broadcast_to(loc["ptot"], (8, MXU_DIM)))
        m_scr[...] = jnp.broadcast_to(pc[:, 0:1], (MXU_DIM, MXU_DIM)) * (m + jnp.where(bd_mask, dm, 0.0))


def _run_lockstep(gens):
    gens = list(gens)
    while gens:
        alive = []
        for g in gens:
            try:
                next(g)
                alive.append(g)
            except StopIteration:
                pass
        gens = alive


def _scan_kernel(rf_ref, kf_ref, vf_ref, kkf_ref, lwf_ref, af_ref, rb_ref, kb_ref, vb_ref, kkb_ref, lwb_ref,
                 ab_ref, ka_ref, s0_ref, yf_ref, yb_ref, sfin_ref, mf_scr, mb_scr,
                 *, n_sub, n_blk):
    c = CHUNK
    blk = pl.program_id(2)

    @pl.when(blk == 0)
    def _():
        zero = jnp.zeros((HEAD, HEAD), F32)
        for d, scr in enumerate((mf_scr, mb_scr)):
            for h in range(HEADS_PER_GROUP):
                blocks = [zero] * HEADS_PER_GROUP
                blocks[h] = jnp.transpose(s0_ref[0, d, h])
                scr[h * HEAD:(h + 1) * HEAD, :] = jnp.concatenate(blocks, axis=1)

    tc = lax.broadcasted_iota(jnp.int32, (c, MXU_DIM), 0)
    sc = lax.broadcasted_iota(jnp.int32, (c, MXU_DIM), 1) % c
    same16 = (sc // 16) == (tc // 16)
    same32 = (sc // 32) == (tc // 32)
    blk_masks = (jnp.where(sc == tc, 1.0, 0.0), same16,
                 jnp.logical_and(same32, jnp.logical_not(same16)), jnp.logical_not(same32))
    bd_mask = (lax.broadcasted_iota(jnp.int32, (MXU_DIM, MXU_DIM), 0) // HEAD) == \
              (lax.broadcasted_iota(jnp.int32, (MXU_DIM, MXU_DIM), 1) // HEAD)
    lane_head = lax.broadcasted_iota(jnp.int32, (c, LANE_TILE), 1) // HEAD
    bd = tuple(jnp.where(lane_head == q, 1.0, 0.0).astype(BF16) for q in range(LANE_TILE // HEAD))
    ka = ka_ref[...]

    refs_f = (rf_ref, kf_ref, vf_ref, kkf_ref, lwf_ref, af_ref)
    refs_b = (rb_ref, kb_ref, vb_ref, kkb_ref, lwb_ref, ab_ref)
    masks_f = _scan_masks(False)
    masks_b = _scan_masks(True)
    slices = [slice(j * c, (j + 1) * c) for j in range(n_sub)]
    items_f = [({}, sl) for sl in slices]
    items_b = [({}, sl) for sl in reversed(slices)]
    chains = []
    for w0 in range(0, n_sub, SCAN_WAVE):
        wave_f = items_f[w0:w0 + SCAN_WAVE]
        wave_b = items_b[w0:w0 + SCAN_WAVE]
        gens = []
        for (of, sf), (ob, sb) in zip(wave_f, wave_b):
            gens.append(_scan_chunk_local(of, refs_f, sf, masks_f, blk_masks, ka, c - 1, bd))
            gens.append(_scan_chunk_local(ob, refs_b, sb, masks_b, blk_masks, ka, 0, bd))
        _run_lockstep(gens + chains)
        chains = [_scan_chain(wave_f, mf_scr, yf_ref, bd, bd_mask),
                  _scan_chain(wave_b, mb_scr, yb_ref, bd, bd_mask)]
    _run_lockstep(chains)

    @pl.when(blk == n_blk - 1)
    def _():
        for d, scr in enumerate((mf_scr, mb_scr)):
            for h in range(HEADS_PER_GROUP):
                rows = scr[h * HEAD:(h + 1) * HEAD, :]
                sfin_ref[0, d, h] = jnp.transpose(rows[:, h * HEAD:(h + 1) * HEAD])


def _scan(r, k, v, kk, lwf, af, lwb, ab, k_a, s0):
    b, t, d_rwkv = r.shape
    groups = d_rwkv // MXU_DIM
    tb = min(SCAN_BLOCK, t)
    n_blk = t // tb
    n_sub = tb // CHUNK
    tok_f = pl.BlockSpec((1, tb, MXU_DIM), lambda i, g, j: (i, j, g))
    tok_b = pl.BlockSpec((1, tb, MXU_DIM), lambda i, g, j: (i, n_blk - 1 - j, g))
    st = pl.BlockSpec((1, 2, HEADS_PER_GROUP, HEAD, HEAD), lambda i, g, j: (i, 0, g, 0, 0))
    y_sd = jax.ShapeDtypeStruct((b, t, d_rwkv), F32)
    return pl.pallas_call(
        functools.partial(_scan_kernel, n_sub=n_sub, n_blk=n_blk),
        grid=(b, groups, n_blk),
        in_specs=[tok_f] * 6 + [tok_b] * 6 + [pl.BlockSpec((1, MXU_DIM), lambda i, g, j: (0, g)), st],
        out_specs=[tok_f, tok_b, st],
        out_shape=[y_sd, y_sd, jax.ShapeDtypeStruct(s0.shape, F32)],
        scratch_shapes=[pltpu.VMEM((MXU_DIM, MXU_DIM), F32), pltpu.VMEM((MXU_DIM, MXU_DIM), F32)],
        compiler_params=_cparams(("parallel", "parallel", "arbitrary")),
        name="scan",
    )(r, k, v, kk, lwf, af, r, k, v, kk, lwb, ab, k_a.reshape(1, d_rwkv), s0)


def _post_kernel(yf_ref, yb_ref, g_ref, bg_ref, lg_ref, lb_ref, ones_ref, o_ref):
    ones_bd = ones_ref[...]
    inv_n = 1.0 / HEAD
    y = yf_ref[0] + yb_ref[0]
    mu = _head_sum(y, ones_bd) * inv_n
    d = y - mu
    var = _head_sum(d * d, ones_bd) * inv_n
    yn = d * lax.rsqrt(var + GN_EPS) * lg_ref[...] + lb_ref[...]
    o_ref[0] = (yn * g_ref[0] + bg_ref[0]).astype(BF16)


def _post(yf, yb, g, bg, lnx_g, lnx_b, ones_bd):
    b, t, d_rwkv = yf.shape
    tp = min(512, t)
    tok = pl.BlockSpec((1, tp, MXU_DIM), lambda i, j, q: (i, j, q))
    par = pl.BlockSpec((1, MXU_DIM), lambda i, j, q: (0, q))
    return pl.pallas_call(
        _post_kernel,
        grid=(b, t // tp, d_rwkv // MXU_DIM),
        in_specs=[tok] * 4 + [par] * 2 + [pl.BlockSpec((MXU_DIM, MXU_DIM), lambda i, j, q: (0, 0))],
        out_specs=tok,
        out_shape=jax.ShapeDtypeStruct((b, t, d_rwkv), BF16),
        compiler_params=_cparams(("parallel", "parallel", "arbitrary")),
        name="rwkv_post",
    )(yf, yb, g, bg, lnx_g.reshape(1, d_rwkv), lnx_b.reshape(1, d_rwkv), ones_bd)


def _window_bounds(n, w):
    t = np.arange(n)
    lo = np.clip(t - w // 2, 0, n)
    hi = np.clip(t - w // 2 + w, 0, n)
    return lo, hi


def _window_matrix(n, w):
    lo, hi = _window_bounds(n, w)
    s = np.arange(n)[None, :]
    m = ((s >= lo[:, None]) & (s < hi[:, None])).astype(np.float32)
    return m / (hi - lo)[:, None].astype(np.float32)


def _pool_ctx_kernel(z_ref, band_ref, pw_ref, ps_ref, o_ref):
    z = z_ref[0]
    m = _dot(band_ref[0], z, precision=lax.Precision.HIGHEST)
    d = (m - z).astype(BF16)
    o_ref[0] = (_dot(d, pw_ref[0]) * ps_ref[...]).astype(BF16)


def _pool_ctx(proj3, col0, pool_w_bf16, pool_scale):
    b, t, _ = proj3.shape
    ng, pg, _ = pool_w_bf16.shape
    band = jnp.asarray(np.stack([_window_matrix(t, w) for w in POOL_WINDOWS]))
    cb = col0 // pg
    return pl.pallas_call(
        _pool_ctx_kernel,
        grid=(b, ng),
        in_specs=[pl.BlockSpec((1, t, pg), lambda i, q: (i, 0, cb + q)),
                  pl.BlockSpec((1, t, t), lambda i, q: (q, 0, 0)),
                  pl.BlockSpec((1, pg, pg), lambda i, q: (q, 0, 0)),
                  pl.BlockSpec((1, pg), lambda i, q: (0, q))],
        out_specs=pl.BlockSpec((1, t, pg), lambda i, q: (i, 0, q)),
        out_shape=jax.ShapeDtypeStruct((b, t, ng * pg), BF16),
        compiler_params=_cparams(("parallel", "arbitrary")),
        name="pool_ctx",
    )(proj3, band, pool_w_bf16, pool_scale.reshape(1, ng * pg))


def _pool_lat_kernel(z_ref, band_ref, pw_ref, ps_ref, o_ref, m1_ref, cs_ref, *, n_rows):
    tile = band_ref.shape[1]
    n_tok = n_rows * GRID_W
    band = band_ref[0]
    for i in range(n_tok // tile):
        sl = slice(i * tile, (i + 1) * tile)
        m1_ref[sl, :] = _dot(band, z_ref[0, sl, :], precision=lax.Precision.HIGHEST)
    cs_ref[0:GRID_W, :] = jnp.zeros((GRID_W, cs_ref.shape[1]), F32)
    for rr in range(n_rows):
        cs_ref[(rr + 1) * GRID_W:(rr + 2) * GRID_W, :] = (
            cs_ref[rr * GRID_W:(rr + 1) * GRID_W, :] + m1_ref[rr * GRID_W:(rr + 1) * GRID_W, :])
    group = pl.program_id(1)
    for gi, w in enumerate(POOL_WINDOWS):
        @pl.when(group == gi)
        def _(w=w):
            lo, hi = _window_bounds(n_rows, w)
            for rr in range(n_rows):
                l, h = int(lo[rr]), int(hi[rr])
                m2 = (cs_ref[h * GRID_W:(h + 1) * GRID_W, :]
                      - cs_ref[l * GRID_W:(l + 1) * GRID_W, :]) / float(h - l)
                sl = slice(rr * GRID_W, (rr + 1) * GRID_W)
                m1_ref[sl, :] = m2 - z_ref[0, sl, :]
    pw = pw_ref[0]
    ps = ps_ref[...]
    for i in range(n_tok // tile):
        sl = slice(i * tile, (i + 1) * tile)
        o_ref[0, sl, :] = (_dot(m1_ref[sl, :].astype(BF16), pw) * ps).astype(BF16)


def _pool_lat(proj3, col0, pool_w_bf16, pool_scale):
    b, t, _ = proj3.shape
    ng, pg, _ = pool_w_bf16.shape
    n_rows = t // GRID_W
    tile = MXU_DIM
    band = jnp.asarray(np.stack(
        [np.kron(np.eye(tile // GRID_W, dtype=np.float32), _window_matrix(GRID_W, w)) for w in POOL_WINDOWS]))
    cb = col0 // pg
    return pl.pallas_call(
        functools.partial(_pool_lat_kernel, n_rows=n_rows),
        grid=(b, ng),
        in_specs=[pl.BlockSpec((1, t, pg), lambda i, q: (i, 0, cb + q)),
                  pl.BlockSpec((1, tile, tile), lambda i, q: (q, 0, 0)),
                  pl.BlockSpec((1, pg, pg), lambda i, q: (q, 0, 0)),
                  pl.BlockSpec((1, pg), lambda i, q: (0, q))],
        out_specs=pl.BlockSpec((1, t, pg), lambda i, q: (i, 0, q)),
        out_shape=jax.ShapeDtypeStruct((b, t, ng * pg), BF16),
        scratch_shapes=[pltpu.VMEM((t, pg), F32), pltpu.VMEM((t + GRID_W, pg), F32)],
        compiler_params=_cparams(("parallel", "arbitrary")),
        name="pool_lat",
    )(proj3, band, pool_w_bf16, pool_scale.reshape(1, ng * pg))


ROW_SPLIT = 2


def _resident(shape):
    nd = len(shape)
    return pl.BlockSpec(shape, lambda i: (0,) * nd, pipeline_mode=pl.Buffered(1))


def _up_out_kernel(ya_ref, yb_ref, ga_ref, gb_ref, x_ref, mod_ref, g_ref, wa_ref, wb_ref, wo_ref,
                   x1_ref, h2_ref):
    rows = x_ref.shape[0] // ROW_SPLIT
    for part in range(ROW_SPLIT):
        sl = slice(part * rows, (part + 1) * rows)
        ua = _dot(ya_ref[sl, :], wa_ref[...])
        ub = _dot(yb_ref[sl, :], wb_ref[...])
        merged = (_sigmoid(ga_ref[sl, :]) * ua + _sigmoid(gb_ref[sl, :]) * ub).astype(BF16)
        x1 = x_ref[sl, :] + mod_ref[0, 2:3, :] * _dot(merged, wo_ref[...])
        x1_ref[sl, :] = x1
        y = x1 * lax.rsqrt(jnp.mean(x1 * x1, axis=-1, keepdims=True) + NORM_EPS)
        h2 = (y * g_ref[...]) * (1.0 + mod_ref[0, 4:5, :]) + mod_ref[0, 3:4, :]
        h2_ref[sl, :] = h2.astype(BF16)


def _up_out(ya2, yb2, gates, wa, wb, w_out, x2, mod3, tiles_per_mod_tokens, norm2_g, tm=256):
    m, d = x2.shape
    tm = min(tm, m)
    ka, kb = ya2.shape[1], yb2.shape[1]
    tpm = tiles_per_mod_tokens // tm
    row = lambda width, col: pl.BlockSpec((tm, width), lambda i: (i, col))
    return pl.pallas_call(
        _up_out_kernel,
        grid=(m // tm,),
        in_specs=[row(ka, 0), row(kb, 0), row(d, 0), row(d, 1), row(d, 0),
                  pl.BlockSpec((1, 6, d), lambda i: (i // tpm, 0, 0)),
                  _resident((1, d)), _resident((ka, d)), _resident((kb, d)), _resident((d, d))],
        out_specs=[row(d, 0), row(d, 0)],
        out_shape=[jax.ShapeDtypeStruct((m, d), F32), jax.ShapeDtypeStruct((m, d), BF16)],
        compiler_params=_cparams(("parallel",)),
        name="up_out",
    )(ya2, yb2, gates, gates, x2, mod3, norm2_g.reshape(1, d), wa, wb, w_out)


def _ffn1_kernel(h_ref, w1_ref, w3_ref, o_ref):
    h = h_ref[...]
    u1 = _dot(h, w1_ref[...])
    u3 = _dot(h, w3_ref[...])
    o_ref[...] = (u1 * _sigmoid(u1) * u3).astype(BF16)


def _ffn1(h2, w13, tm=1024, tn=PROJ_TN):
    m, d = h2.shape
    tm = min(tm, m)
    d_ff = w13.shape[1] // 2
    nj = d_ff // tn
    return pl.pallas_call(
        _ffn1_kernel,
        grid=(m // tm, nj),
        in_specs=[pl.BlockSpec((tm, d), lambda i, j: (i, 0)),
                  pl.BlockSpec((d, tn), lambda i, j: (0, j)),
                  pl.BlockSpec((d, tn), lambda i, j: (0, nj + j))],
        out_specs=pl.BlockSpec((tm, tn), lambda i, j: (i, j)),
        out_shape=jax.ShapeDtypeStruct((m, d_ff), BF16),
        compiler_params=_cparams(("parallel", "arbitrary")),
        name="ffn1",
    )(h2, w13, w13)


def _ffn2_kernel(a_ref, x_ref, mod_ref, g_ref, w_ref, o_ref):
    rows = x_ref.shape[0] // ROW_SPLIT
    for part in range(ROW_SPLIT):
        sl = slice(part * rows, (part + 1) * rows)
        x2 = x_ref[sl, :] + mod_ref[0, 5:6, :] * _dot(a_ref[sl, :], w_ref[...])
        y = x2 * lax.rsqrt(jnp.mean(x2 * x2, axis=-1, keepdims=True) + NORM_EPS)
        o_ref[sl, :] = y * g_ref[...]


def _ffn2(act, w2, x1, mod3, tiles_per_mod_tokens, final_g, tm=256):
    m, d = x1.shape
    tm = min(tm, m)
    d_ff = act.shape[1]
    tpm = tiles_per_mod_tokens // tm
    return pl.pallas_call(
        _ffn2_kernel,
        grid=(m // tm,),
        in_specs=[pl.BlockSpec((tm, d_ff), lambda i: (i, 0)),
                  pl.BlockSpec((tm, d), lambda i: (i, 0)),
                  pl.BlockSpec((1, 6, d), lambda i: (i // tpm, 0, 0)),
                  _resident((1, d)), _resident((d_ff, d))],
        out_specs=pl.BlockSpec((tm, d), lambda i: (i, 0)),
        out_shape=jax.ShapeDtypeStruct((m, d), F32),
        compiler_params=_cparams(("parallel",)),
        name="ffn2",
    )(act, x1, mod3, final_g.reshape(1, d), w2)


def _trunk(x, mod, s0, latent, p):
    b, t, d = x.shape
    m = b * t
    d_rwkv = p["d_rwkv"]
    d_pool = d - d_rwkv
    heads = d_rwkv // HEAD
    x2 = x.reshape(m, d)
    nb = mod.shape[0]
    mod3 = mod.reshape(nb, 6, d)
    tokens_per_mod = t if nb == b else m

    proj, gates = _in_proj(x2, p["norm1_g"], mod3, tokens_per_mod, p["w_in"], p["rw_cols_p"] + d_pool)
    proj3 = proj.reshape(b, t, proj.shape[1])

    r, k, v, kk, lwf, lwb, af, ab, g, bg = _prep(
        proj3, p["mu"], p["w0"], p["w2"], p["a0"], p["a2"], p["g2"], p["k_k"], p["k_a"], p["r_k"],
        p["ones_bd"], d_rwkv, p["rw_cols_p"])

    if s0 is None:
        s0 = jnp.zeros((b, 2, heads, HEAD, HEAD), F32)
    yf, yb, s_fin = _scan(r, k, v, kk, lwf, af, lwb, ab, p["k_a"], s0)
    ya = _post(yf, yb, g, bg, p["lnx_g"], p["lnx_b"], p["ones_bd"])

    pool_col0 = p["rw_cols_p"]
    if latent:
        yb_pool = _pool_lat(proj3, pool_col0, p["pool_w"], p["pool_scale"])
    else:
        yb_pool = _pool_ctx(proj3, pool_col0, p["pool_w"], p["pool_scale"])

    x1, h2 = _up_out(ya.reshape(m, d_rwkv), yb_pool.reshape(m, d_pool), gates, p["w_up_a"], p["w_up_b"],
                     p["w_out"], x2, mod3, tokens_per_mod, p["norm2_g"])
    act = _ffn1(h2, p["ffn_w13"])
    y = _ffn2(act, p["ffn_w2"], x1, mod3, tokens_per_mod, p["final_g"])
    return y.reshape(b, t, d), s_fin


def kernel(x_prompt, x_sample, c, state_rwkv, c_ctx, w_mod, b_mod, norm1_g, w_in, shift_mu, w0, w2, a0, a2, g2,
           k_k, k_a, r_k, lnx_g, lnx_b, w_up_a, pool_w, pool_scale, w_up_b, w_out, norm2_g, ffn_w13, ffn_w2,
           final_g):
    depth = w_in.shape[0]
    assert depth == 1, "final norm is fused into the last layer: single-layer trunk only"
    d = x_prompt.shape[-1]
    d_rwkv = w_up_a.shape[1]
    rw_cols = shift_mu.shape[1]
    rw_cols_p = -(-rw_cols // PROJ_TN) * PROJ_TN
    assert x_prompt.shape[1] == SEQ_TILE and x_sample.shape[1] % SEQ_TILE == 0

    l = 0
    w_in_l = w_in[l].astype(BF16)
    w_in_p = jnp.concatenate(
        [w_in_l[:, :rw_cols], jnp.zeros((d, rw_cols_p - rw_cols), BF16), w_in_l[:, rw_cols:]], axis=1)
    ones_bd = jnp.asarray(np.kron(np.eye(HEADS_PER_GROUP, dtype=np.float32),
                                  np.ones((HEAD, HEAD), np.float32))).astype(BF16)
    p = {
        "d_rwkv": d_rwkv, "rw_cols_p": rw_cols_p,
        "norm1_g": norm1_g[l], "w_in": w_in_p,
        "mu": jnp.pad(shift_mu[l], (0, rw_cols_p - rw_cols)).reshape(1, rw_cols_p),
        "w0": w0[l], "w2": w2[l], "a0": a0[l], "a2": a2[l], "g2": g2[l],
        "k_k": k_k[l], "k_a": k_a[l], "r_k": r_k[l].reshape(-1), "lnx_g": lnx_g[l], "lnx_b": lnx_b[l],
        "ones_bd": ones_bd,
        "w_up_a": w_up_a[l].astype(BF16), "w_up_b": w_up_b[l].astype(BF16),
        "pool_w": pool_w[l].astype(BF16), "pool_scale": pool_scale[l],
        "w_out": w_out[l].astype(BF16), "norm2_g": norm2_g[l],
        "ffn_w13": ffn_w13[l].astype(BF16), "ffn_w2": ffn_w2[l].astype(BF16), "final_g": final_g,
    }
    cond = jnp.concatenate([c_ctx[None, :], c], axis=0)
    mod = _mod(cond, w_mod[l], b_mod[l])
    y_prompt, s_ctx = _trunk(x_prompt, mod[:1], None, False, p)
    y_sample, _ = _trunk(x_sample, mod[1:], state_rwkv[:, l], True, p)
    new_state = s_ctx[:, None].astype(x_prompt.dtype)
    return (y_prompt, y_sample, new_state)
```

```python
import functools

import numpy as np
import jax
import jax.numpy as jnp
from jax import lax
from jax.experimental import pallas as pl
from jax.experimental.pallas import tpu as pltpu

F32 = jnp.float32
BF16 = jnp.bfloat16

HEAD = 64
POOL_WINDOWS = (2, 4, 8, 16)
GRID_W = 64
DECAY_LORA = 64
AAA_LORA = 64
GATE_LORA = 128
NORM_EPS = 1e-6
GN_EPS = 64e-5

LANE_TILE = 128
MXU_DIM = 256
HEADS_PER_GROUP = MXU_DIM // HEAD
VMEM_LIMIT = 56 * 1024 * 1024

CHUNK = 64
SEQ_TILE = 256
PROJ_TN = 512
SCAN_BLOCK = 1024
SCAN_WAVE = 4


def _cparams(sem):
    return pltpu.CompilerParams(dimension_semantics=sem, vmem_limit_bytes=VMEM_LIMIT)


def _dot(a, b, dims=(((1,), (0,)), ((), ())), precision=None):
    return lax.dot_general(a, b, dims, precision=precision, preferred_element_type=F32)


_NN = (((1,), (0,)), ((), ()))
_NT = (((1,), (1,)), ((), ()))


def _split2(x):
    hi = x.astype(BF16)
    lo = (x - hi.astype(F32)).astype(BF16)
    return hi, lo


def _split3(x):
    hi = x.astype(BF16)
    r1 = x - hi.astype(F32)
    mid = r1.astype(BF16)
    lo = (r1 - mid.astype(F32)).astype(BF16)
    return hi, mid, lo


def _mm(a, b, passes):
    if passes == 1:
        return _dot(a.astype(BF16), b.astype(BF16))
    assert passes == 3
    ah, al = _split2(a)
    bh, bl = _split2(b)
    return _dot(ah, bh) + (_dot(ah, bl) + _dot(al, bh))


def _mm_exact_rhs(a, b_bf16):
    a0, a1 = _split2(a)
    return _dot(a0, b_bf16) + _dot(a1, b_bf16)


def _sigmoid(x):
    return 1.0 / (1.0 + jnp.exp(-x))


def _mod_kernel(c_ref, w_ref, b_ref, o_ref):
    c = c_ref[...]
    s = (c * _sigmoid(c)).astype(BF16)
    o_ref[...] = _dot(s, w_ref[...].astype(BF16)) + b_ref[...]


def _mod(cond, w_mod, b_mod):
    nb, d = cond.shape
    rows = -(-nb // 16) * 16
    cond_p = jnp.pad(cond, ((0, rows - nb), (0, 0)))
    n = w_mod.shape[1]
    tn = 1024
    out = pl.pallas_call(
        _mod_kernel,
        grid=(n // tn,),
        in_specs=[pl.BlockSpec((rows, d), lambda j: (0, 0)),
                  pl.BlockSpec((d, tn), lambda j: (0, j)),
                  pl.BlockSpec((1, tn), lambda j: (0, j))],
        out_specs=pl.BlockSpec((rows, tn), lambda j: (0, j)),
        out_shape=jax.ShapeDtypeStruct((rows, n), F32),
        compiler_params=_cparams(("arbitrary",)),
        name="mod",
    )(cond_p, w_mod, b_mod.reshape(1, n))
    return out[:nb]


NORM_ROWS = 16


def _in_kernel(x_ref, g_ref, mod_ref, w_ref, om_ref, og_ref, h_ref, *, nj_main):
    j = pl.program_id(1)

    @pl.when(j == 0)
    def _():
        gain = g_ref[...]
        scale = 1.0 + mod_ref[0, 1:2, :]
        shift = mod_ref[0, 0:1, :]

        def norm_rows(i, carry):
            rows = pl.ds(pl.multiple_of(i * NORM_ROWS, NORM_ROWS), NORM_ROWS)
            x = x_ref[rows, :]
            y = x * lax.rsqrt(jnp.mean(x * x, axis=-1, keepdims=True) + NORM_EPS)
            h_ref[rows, :] = ((y * gain) * scale + shift).astype(BF16)
            return carry

        lax.fori_loop(0, x_ref.shape[0] // NORM_ROWS, norm_rows, 0, unroll=8)

    @pl.when(j < nj_main)
    def _():
        om_ref[...] = _dot(h_ref[...], w_ref[...])

    @pl.when(j >= nj_main)
    def _():
        og_ref[...] = _dot(h_ref[...], w_ref[...])


def _in_proj(x2, norm_g, mod3, tiles_per_mod_tokens, w_bf16, n_main, tm=1024, tn=PROJ_TN):
    m, d = x2.shape
    tm = min(tm, m)
    n = w_bf16.shape[1]
    tpm = tiles_per_mod_tokens // tm
    nj_main = n_main // tn
    return pl.pallas_call(
        functools.partial(_in_kernel, nj_main=nj_main),
        grid=(m // tm, n // tn),
        in_specs=[pl.BlockSpec((tm, d), lambda i, j: (i, 0)),
                  pl.BlockSpec((1, d), lambda i, j: (0, 0)),
                  pl.BlockSpec((1, 6, d), lambda i, j: (i // tpm, 0, 0)),
                  pl.BlockSpec((d, tn), lambda i, j: (0, j))],
        out_specs=[pl.BlockSpec((tm, tn), lambda i, j: (i, jnp.minimum(j, nj_main - 1))),
                   pl.BlockSpec((tm, tn), lambda i, j: (i, jnp.maximum(j - nj_main, 0)))],
        out_shape=[jax.ShapeDtypeStruct((m, n_main), F32), jax.ShapeDtypeStruct((m, n - n_main), F32)],
        scratch_shapes=[pltpu.VMEM((tm, d), BF16)],
        compiler_params=_cparams(("parallel", "arbitrary")),
        name="in_proj",
    )(x2, norm_g.reshape(1, d), mod3, w_bf16)


P_LORA_DECAY = 3
P_LORA = 1
DECAY_GAIN = float(np.exp(-0.5))


def _head_sum(x, ones_bd):
    return _mm_exact_rhs(x, ones_bd)


def _prep_kernel(main_ref, prev_ref, next_ref, mu_ref, w0_ref, w2_ref, a0_ref, a2_ref, g2_ref,
                 kkw_ref, ka_ref, rk_ref, ones_ref, r_o, k_o, v_o, kk_o, lwf_o, lwb_o, af_o, ab_o, g_o, bg_o,
                 *, n_tiles, d_rwkv):
    t = pl.program_id(1)
    rows = main_ref.shape[1]
    row8 = lax.broadcasted_iota(jnp.int32, (8, 1), 0)
    has_prev = t > 0
    has_next = t < n_tiles - 1

    def shifted(c0, c1):
        z = main_ref[0, :, c0:c1]
        p = jnp.where(has_prev, prev_ref[0, 7:8, c0:c1], 0.0)
        n = jnp.where(has_next, next_ref[0, 0:1, c0:c1], 0.0)
        zp = pltpu.roll(z, 1, 0)
        zn = pltpu.roll(z, rows - 1, 0)
        zp = jnp.concatenate([jnp.where(row8 == 0, p, zp[:8]), zp[8:]], axis=0)
        zn = jnp.concatenate([zn[:rows - 8], jnp.where(row8 == 7, n, zn[rows - 8:])], axis=0)
        mu = mu_ref[:, c0:c1]
        return (1.0 - mu) * z + (0.5 * mu) * (zp + zn)

    dr = d_rwkv
    r = shifted(0, dr)
    r_o[0] = r.astype(BF16)
    k = shifted(dr, 2 * dr)
    k_o[0] = k.astype(BF16)
    v = shifted(2 * dr, 3 * dr)
    v_o[0] = v.astype(BF16)

    kkv = k * kkw_ref[...]
    ones_bd = ones_ref[...]
    for j in range(dr // MXU_DIM):
        sl = slice(j * MXU_DIM, (j + 1) * MXU_DIM)
        x = kkv[:, sl]
        ss = _head_sum(x * x, ones_bd)
        kk_o[0, :, sl] = (x * lax.rsqrt(jnp.maximum(ss, 1e-24))).astype(BF16)

    c = 3 * dr
    sm = shifted(c, c + 2 * DECAY_LORA + 2 * AAA_LORA + GATE_LORA)
    o = 0
    for d, lw_o in enumerate((lwf_o, lwb_o)):
        wd = jnp.tanh(sm[:, o:o + DECAY_LORA])
        o += DECAY_LORA
        xw = w0_ref[d:d + 1, :] + _mm(wd, w2_ref[d], passes=P_LORA_DECAY)
        lw_o[0] = -DECAY_GAIN * _sigmoid(xw)
    for d, a_o in enumerate((af_o, ab_o)):
        ad = sm[:, o:o + AAA_LORA]
        o += AAA_LORA
        a_o[0] = _sigmoid(a0_ref[d:d + 1, :] + _mm(ad, a2_ref[d], passes=P_LORA))
    gd = _sigmoid(sm[:, o:o + GATE_LORA])
    g = _mm(gd, g2_ref[...], passes=P_LORA)
    g_o[0] = g.astype(BF16)

    for j in range(dr // MXU_DIM):
        sl = slice(j * MXU_DIM, (j + 1) * MXU_DIM)
        ka = ka_ref[:, sl]
        kj = k[:, sl]
        kd_sum = kj * (1.0 + (af_o[0, :, sl] - 1.0) * ka) + kj * (1.0 + (ab_o[0, :, sl] - 1.0) * ka)
        bonus = _head_sum(r[:, sl] * kd_sum * rk_ref[:, sl], ones_bd) * v[:, sl]
        bg_o[0, :, sl] = (bonus * g[:, sl]).astype(BF16)


def _prep(proj3, mu_p, w0, w2, a0, a2, g2, k_k, k_a, r_k, ones_bd, d_rwkv, rw_cols_p):
    b, t, _ = proj3.shape
    tp = SEQ_TILE
    n_tiles = t // tp
    sub = tp // 8
    last8 = t // 8 - 1
    out_sd = jax.ShapeDtypeStruct((b, t, d_rwkv), F32)
    half_sd = jax.ShapeDtypeStruct((b, t, d_rwkv), BF16)
    o_spec = pl.BlockSpec((1, tp, d_rwkv), lambda i, j: (i, j, 0))

    def full(a):
        nd = a.ndim
        return pl.BlockSpec(a.shape, lambda i, j: (0,) * nd)

    params = (mu_p, w0, w2, a0, a2, g2, k_k.reshape(1, d_rwkv), k_a.reshape(1, d_rwkv),
              r_k.reshape(1, d_rwkv), ones_bd)
    return pl.pallas_call(
        functools.partial(_prep_kernel, n_tiles=n_tiles, d_rwkv=d_rwkv),
        grid=(b, n_tiles),
        in_specs=[pl.BlockSpec((1, tp, rw_cols_p), lambda i, j: (i, j, 0)),
                  pl.BlockSpec((1, 8, rw_cols_p), lambda i, j: (i, jnp.maximum(j * sub - 1, 0), 0)),
                  pl.BlockSpec((1, 8, rw_cols_p), lambda i, j: (i, jnp.minimum((j + 1) * sub, last8), 0)),
                  ] + [full(a) for a in params],
        out_specs=[o_spec] * 10,
        out_shape=[half_sd] * 4 + [out_sd] * 4 + [half_sd] * 2,
        compiler_params=_cparams(("parallel", "arbitrary")),
        name="rwkv_prep",
    )(proj3, proj3, proj3, *params)


def _block_diag_rhs(y, bd):
    yb = y.astype(BF16)
    per_tile = LANE_TILE // HEAD
    zero = jnp.zeros((y.shape[0], LANE_TILE), BF16)
    rows = []
    for h in range(HEADS_PER_GROUP):
        t = h // per_tile
        tiles = [zero] * (MXU_DIM // LANE_TILE)
        tiles[t] = yb[:, t * LANE_TILE:(t + 1) * LANE_TILE] * bd[h % per_tile]
        rows.append(jnp.concatenate(tiles, axis=1))
    return jnp.concatenate(rows, axis=0)


def _mmc(x, y_bd):
    return _dot(x.astype(BF16), y_bd)


def _scan_masks(rev):
    c = CHUNK
    ti = lax.broadcasted_iota(jnp.int32, (c, c), 0)
    si = lax.broadcasted_iota(jnp.int32, (c, c), 1)
    tri = jnp.where((si >= ti) if rev else (si <= ti), 1.0, 0.0).astype(BF16)
    tc = lax.broadcasted_iota(jnp.int32, (c, MXU_DIM), 0)
    sc = lax.broadcasted_iota(jnp.int32, (c, MXU_DIM), 1) % c
    before = (sc > tc) if rev else (sc < tc)
    upto = (sc >= tc) if rev else (sc <= tc)
    return tri, before, upto


def _scan_chunk_local(out, refs, sl, dir_masks, blk_masks, ka, last_row, bd):
    r_ref, k_ref, v_ref, kk_ref, lw_ref, a_ref = refs
    tri, before, upto = dir_masks
    eye, same16, off32, off64 = blk_masks
    c = CHUNK
    kk = kk_ref[0, sl, :].astype(F32)
    lw = lw_ref[0, sl, :]
    a = a_ref[0, sl, :]
    lw0, lw1, lw2 = _split3(lw)
    cum = _dot(tri, lw0) + (_dot(tri, lw1) + _dot(tri, lw2))
    yield
    e_in = jnp.exp(cum)
    e_inv = jnp.exp(-cum)
    al = kk * jnp.exp(cum - lw)
    rh = r_ref[0, sl, :].astype(F32) * e_in
    be = (kk * a) * e_inv
    kap = (k_ref[0, sl, :].astype(F32) * (1.0 + (a - 1.0) * ka)) * e_inv
    ptot = e_in[last_row:last_row + 1, :]
    ar = jnp.concatenate([al, rh], axis=0).astype(BF16)
    ab = _dot(ar, _block_diag_rhs(be, bd), _NT)
    ak = _dot(ar, _block_diag_rhs(kap, bd), _NT)
    yield
    lm = jnp.where(before, ab[:c], 0.0)
    aak = jnp.where(before, ak[:c], 0.0)
    kq = jnp.where(upto, ak[c:], 0.0)
    out["bq"] = jnp.where(upto, ab[c:], 0.0).astype(BF16)
    ld = jnp.where(same16, lm, 0.0)
    p = eye - ld
    ld_bd = _block_diag_rhs(ld, bd)
    l2 = _mmc(ld, ld_bd)
    yield
    pl2 = _mmc(jnp.concatenate([p, l2], axis=0), _block_diag_rhs(l2, bd))
    yield
    p = p + pl2[:c]
    l4 = pl2[c:]
    pl4 = _mmc(jnp.concatenate([p, l4], axis=0), _block_diag_rhs(l4, bd))
    yield
    p = p + pl4[:c]
    p8 = _mmc(p, _block_diag_rhs(pl4[c:], bd))
    v = v_ref[0, sl, :]
    akv = _mmc(jnp.concatenate([aak, kq], axis=0), _block_diag_rhs(v, bd))
    aak_v = akv[:c]
    out["kq_v"] = akv[c:]
    yield
    p = p + p8
    for off in (off32, off64):
        lo = jnp.where(off, lm, 0.0)
        t1 = _mmc(lo, _block_diag_rhs(p, bd))
        yield
        t2 = _mmc(p, _block_diag_rhs(t1, bd))
        yield
        p = p - t2
    wt = _mmc(p, _block_diag_rhs(al, bd))
    out["ut"] = _mmc(p, _block_diag_rhs(aak_v, bd))
    yield
    out["wr"] = jnp.concatenate([wt, rh], axis=0).astype(BF16)
    out["v"] = v.astype(BF16)
    out["bkT"] = jnp.transpose(jnp.concatenate([be, kap], axis=0)).astype(BF16)
    out["ptot"] = ptot


def _scan_chain(items, m_scr, y_ref, bd, bd_mask):
    c = CHUNK
    for loc, sl in items:
        m = m_scr[...]
        zs = _dot(loc["wr"], m.astype(BF16))
        yield
        u = -(zs[:c] + loc["ut"])
        uv = jnp.concatenate([u.astype(BF16), loc["v"]], axis=0)
        dm = _dot(loc["bkT"], uv)
        yu = _mmc(loc["bq"], _block_diag_rhs(u, bd))
        yield
        y_ref[0, sl, :] = (zs[c:] + loc["kq_v"] + yu).astype(BF16)
        pc = jnp.transpose(jnp.broadcast_to(loc["ptot"], (8, MXU_DIM)))
        m_scr[...] = jnp.broadcast_to(pc[:, 0:1], (MXU_DIM, MXU_DIM)) * (m + jnp.where(bd_mask, dm, 0.0))


def _run_lockstep(gens):
    gens = list(gens)
    while gens:
        alive = []
        for g in gens:
            try:
                next(g)
                alive.append(g)
            except StopIteration:
                pass
        gens = alive


def _scan_kernel(rf_ref, kf_ref, vf_ref, kkf_ref, lwf_ref, af_ref, rb_ref, kb_ref, vb_ref, kkb_ref, lwb_ref,
                 ab_ref, ka_ref, s0_ref, yf_ref, yb_ref, sfin_ref, mf_scr, mb_scr,
                 *, n_sub, n_blk):
    c = CHUNK
    blk = pl.program_id(2)

    @pl.when(blk == 0)
    def _():
        zero = jnp.zeros((HEAD, HEAD), F32)
        for d, scr in enumerate((mf_scr, mb_scr)):
            for h in range(HEADS_PER_GROUP):
                blocks = [zero] * HEADS_PER_GROUP
                blocks[h] = jnp.transpose(s0_ref[0, d, h])
                scr[h * HEAD:(h + 1) * HEAD, :] = jnp.concatenate(blocks, axis=1)

    tc = lax.broadcasted_iota(jnp.int32, (c, MXU_DIM), 0)
    sc = lax.broadcasted_iota(jnp.int32, (c, MXU_DIM), 1) % c
    same16 = (sc // 16) == (tc // 16)
    same32 = (sc // 32) == (tc // 32)
    blk_masks = (jnp.where(sc == tc, 1.0, 0.0), same16,
                 jnp.logical_and(same32, jnp.logical_not(same16)), jnp.logical_not(same32))
    bd_mask = (lax.broadcasted_iota(jnp.int32, (MXU_DIM, MXU_DIM), 0) // HEAD) == \
              (lax.broadcasted_iota(jnp.int32, (MXU_DIM, MXU_DIM), 1) // HEAD)
    lane_head = lax.broadcasted_iota(jnp.int32, (c, LANE_TILE), 1) // HEAD
    bd = tuple(jnp.where(lane_head == q, 1.0, 0.0).astype(BF16) for q in range(LANE_TILE // HEAD))
    ka = ka_ref[...]

    refs_f = (rf_ref, kf_ref, vf_ref, kkf_ref, lwf_ref, af_ref)
    refs_b = (rb_ref, kb_ref, vb_ref, kkb_ref, lwb_ref, ab_ref)
    masks_f = _scan_masks(False)
    masks_b = _scan_masks(True)
    slices = [slice(j * c, (j + 1) * c) for j in range(n_sub)]
    items_f = [({}, sl) for sl in slices]
    items_b = [({}, sl) for sl in reversed(slices)]
    chains = []
    for w0 in range(0, n_sub, SCAN_WAVE):
        wave_f = items_f[w0:w0 + SCAN_WAVE]
        wave_b = items_b[w0:w0 + SCAN_WAVE]
        gens = []
        for (of, sf), (ob, sb) in zip(wave_f, wave_b):
            gens.append(_scan_chunk_local(of, refs_f, sf, masks_f, blk_masks, ka, c - 1, bd))
            gens.append(_scan_chunk_local(ob, refs_b, sb, masks_b, blk_masks, ka, 0, bd))
        _run_lockstep(gens + chains)
        chains = [_scan_chain(wave_f, mf_scr, yf_ref, bd, bd_mask),
                  _scan_chain(wave_b, mb_scr, yb_ref, bd, bd_mask)]
    _run_lockstep(chains)

    @pl.when(blk == n_blk - 1)
    def _():
        for d, scr in enumerate((mf_scr, mb_scr)):
            for h in range(HEADS_PER_GROUP):
                rows = scr[h * HEAD:(h + 1) * HEAD, :]
                sfin_ref[0, d, h] = jnp.transpose(rows[:, h * HEAD:(h + 1) * HEAD])


def _scan(r, k, v, kk, lwf, af, lwb, ab, k_a, s0):
    b, t, d_rwkv = r.shape
    groups = d_rwkv // MXU_DIM
    tb = min(SCAN_BLOCK, t)
    n_blk = t // tb
    n_sub = tb // CHUNK
    tok_f = pl.BlockSpec((1, tb, MXU_DIM), lambda i, g, j: (i, j, g))
    tok_b = pl.BlockSpec((1, tb, MXU_DIM), lambda i, g, j: (i, n_blk - 1 - j, g))
    st = pl.BlockSpec((1, 2, HEADS_PER_GROUP, HEAD, HEAD), lambda i, g, j: (i, 0, g, 0, 0))
    y_sd = jax.ShapeDtypeStruct((b, t, d_rwkv), BF16)
    return pl.pallas_call(
        functools.partial(_scan_kernel, n_sub=n_sub, n_blk=n_blk),
        grid=(b, groups, n_blk),
        in_specs=[tok_f] * 6 + [tok_b] * 6 + [pl.BlockSpec((1, MXU_DIM), lambda i, g, j: (0, g)), st],
        out_specs=[tok_f, tok_b, st],
        out_shape=[y_sd, y_sd, jax.ShapeDtypeStruct(s0.shape, F32)],
        scratch_shapes=[pltpu.VMEM((MXU_DIM, MXU_DIM), F32), pltpu.VMEM((MXU_DIM, MXU_DIM), F32)],
        compiler_params=_cparams(("parallel", "parallel", "arbitrary")),
        name="scan",
    )(r, k, v, kk, lwf, af, r, k, v, kk, lwb, ab, k_a.reshape(1, d_rwkv), s0)


def _post_kernel(yf_ref, yb_ref, g_ref, bg_ref, lg_ref, lb_ref, ones_ref, o_ref):
    ones_bd = ones_ref[...]
    inv_n = 1.0 / HEAD
    y = yf_ref[0].astype(F32) + yb_ref[0].astype(F32)
    mu = _head_sum(y, ones_bd) * inv_n
    d = y - mu
    var = _head_sum(d * d, ones_bd) * inv_n
    yn = d * lax.rsqrt(var + GN_EPS) * lg_ref[...] + lb_ref[...]
    o_ref[0] = (yn * g_ref[0].astype(F32) + bg_ref[0].astype(F32)).astype(BF16)


def _post(yf, yb, g, bg, lnx_g, lnx_b, ones_bd):
    b, t, d_rwkv = yf.shape
    tp = min(512, t)
    tok = pl.BlockSpec((1, tp, MXU_DIM), lambda i, j, q: (i, j, q))
    par = pl.BlockSpec((1, MXU_DIM), lambda i, j, q: (0, q))
    return pl.pallas_call(
        _post_kernel,
        grid=(b, t // tp, d_rwkv // MXU_DIM),
        in_specs=[tok] * 4 + [par] * 2 + [pl.BlockSpec((MXU_DIM, MXU_DIM), lambda i, j, q: (0, 0))],
        out_specs=tok,
        out_shape=jax.ShapeDtypeStruct((b, t, d_rwkv), BF16),
        compiler_params=_cparams(("parallel", "parallel", "arbitrary")),
        name="rwkv_post",
    )(yf, yb, g, bg, lnx_g.reshape(1, d_rwkv), lnx_b.reshape(1, d_rwkv), ones_bd)


def _window_bounds(n, w):
    t = np.arange(n)
    lo = np.clip(t - w // 2, 0, n)
    hi = np.clip(t - w // 2 + w, 0, n)
    return lo, hi


def _window_matrix(n, w):
    lo, hi = _window_bounds(n, w)
    s = np.arange(n)[None, :]
    m = ((s >= lo[:, None]) & (s < hi[:, None])).astype(np.float32)
    return m / (hi - lo)[:, None].astype(np.float32)


def _pool_ctx_kernel(z_ref, band_ref, pw_ref, ps_ref, o_ref):
    z = z_ref[0]
    m = _dot(band_ref[0], z, precision=lax.Precision.HIGHEST)
    d = (m - z).astype(BF16)
    o_ref[0] = (_dot(d, pw_ref[0]) * ps_ref[...]).astype(BF16)


def _pool_ctx(proj3, col0, pool_w_bf16, pool_scale):
    b, t, _ = proj3.shape
    ng, pg, _ = pool_w_bf16.shape
    band = jnp.asarray(np.stack([_window_matrix(t, w) for w in POOL_WINDOWS]))
    cb = col0 // pg
    return pl.pallas_call(
        _pool_ctx_kernel,
        grid=(b, ng),
        in_specs=[pl.BlockSpec((1, t, pg), lambda i, q: (i, 0, cb + q)),
                  pl.BlockSpec((1, t, t), lambda i, q: (q, 0, 0)),
                  pl.BlockSpec((1, pg, pg), lambda i, q: (q, 0, 0)),
                  pl.BlockSpec((1, pg), lambda i, q: (0, q))],
        out_specs=pl.BlockSpec((1, t, pg), lambda i, q: (i, 0, q)),
        out_shape=jax.ShapeDtypeStruct((b, t, ng * pg), BF16),
        compiler_params=_cparams(("parallel", "arbitrary")),
        name="pool_ctx",
    )(proj3, band, pool_w_bf16, pool_scale.reshape(1, ng * pg))


def _pool_lat_kernel(z_ref, band_ref, pw_ref, ps_ref, o_ref, m1_ref, cs_ref, *, n_rows):
    tile = band_ref.shape[1]
    n_tok = n_rows * GRID_W
    band = band_ref[0]
    for i in range(n_tok // tile):
        sl = slice(i * tile, (i + 1) * tile)
        m1_ref[sl, :] = _dot(band, z_ref[0, sl, :], precision=lax.Precision.HIGHEST)
    cs_ref[0:GRID_W, :] = jnp.zeros((GRID_W, cs_ref.shape[1]), F32)
    for rr in range(n_rows):
        cs_ref[(rr + 1) * GRID_W:(rr + 2) * GRID_W, :] = (
            cs_ref[rr * GRID_W:(rr + 1) * GRID_W, :] + m1_ref[rr * GRID_W:(rr + 1) * GRID_W, :])
    group = pl.program_id(1)
    for gi, w in enumerate(POOL_WINDOWS):
        @pl.when(group == gi)
        def _(w=w):
            lo, hi = _window_bounds(n_rows, w)
            for rr in range(n_rows):
                l, h = int(lo[rr]), int(hi[rr])
                m2 = (cs_ref[h * GRID_W:(h + 1) * GRID_W, :]
                      - cs_ref[l * GRID_W:(l + 1) * GRID_W, :]) / float(h - l)
                sl = slice(rr * GRID_W, (rr + 1) * GRID_W)
                m1_ref[sl, :] = m2 - z_ref[0, sl, :]
    pw = pw_ref[0]
    ps = ps_ref[...]
    for i in range(n_tok // tile):
        sl = slice(i * tile, (i + 1) * tile)
        o_ref[0, sl, :] = (_dot(m1_ref[sl, :].astype(BF16), pw) * ps).astype(BF16)


def _pool_lat(proj3, col0, pool_w_bf16, pool_scale):
    b, t, _ = proj3.shape
    ng, pg, _ = pool_w_bf16.shape
    n_rows = t // GRID_W
    tile = MXU_DIM
    band = jnp.asarray(np.stack(
        [np.kron(np.eye(tile // GRID_W, dtype=np.float32), _window_matrix(GRID_W, w)) for w in POOL_WINDOWS]))
    cb = col0 // pg
    return pl.pallas_call(
        functools.partial(_pool_lat_kernel, n_rows=n_rows),
        grid=(b, ng),
        in_specs=[pl.BlockSpec((1, t, pg), lambda i, q: (i, 0, cb + q)),
                  pl.BlockSpec((1, tile, tile), lambda i, q: (q, 0, 0)),
                  pl.BlockSpec((1, pg, pg), lambda i, q: (q, 0, 0)),
                  pl.BlockSpec((1, pg), lambda i, q: (0, q))],
        out_specs=pl.BlockSpec((1, t, pg), lambda i, q: (i, 0, q)),
        out_shape=jax.ShapeDtypeStruct((b, t, ng * pg), BF16),
        scratch_shapes=[pltpu.VMEM((t, pg), F32), pltpu.VMEM((t + GRID_W, pg), F32)],
        compiler_params=_cparams(("parallel", "arbitrary")),
        name="pool_lat",
    )(proj3, band, pool_w_bf16, pool_scale.reshape(1, ng * pg))


ROW_SPLIT = 2


def _resident(shape):
    nd = len(shape)
    return pl.BlockSpec(shape, lambda i: (0,) * nd, pipeline_mode=pl.Buffered(1))


def _up_out_kernel(ya_ref, yb_ref, ga_ref, gb_ref, x_ref, mod_ref, g_ref, wa_ref, wb_ref, wo_ref,
                   x1_ref, h2_ref):
    rows = x_ref.shape[0] // ROW_SPLIT
    for part in range(ROW_SPLIT):
        sl = slice(part * rows, (part + 1) * rows)
        ua = _dot(ya_ref[sl, :], wa_ref[...])
        ub = _dot(yb_ref[sl, :], wb_ref[...])
        merged = (_sigmoid(ga_ref[sl, :]) * ua + _sigmoid(gb_ref[sl, :]) * ub).astype(BF16)
        x1 = x_ref[sl, :] + mod_ref[0, 2:3, :] * _dot(merged, wo_ref[...])
        x1_ref[sl, :] = x1
        y = x1 * lax.rsqrt(jnp.mean(x1 * x1, axis=-1, keepdims=True) + NORM_EPS)
        h2 = (y * g_ref[...]) * (1.0 + mod_ref[0, 4:5, :]) + mod_ref[0, 3:4, :]
        h2_ref[sl, :] = h2.astype(BF16)


def _up_out(ya2, yb2, gates, wa, wb, w_out, x2, mod3, tiles_per_mod_tokens, norm2_g, tm=256):
    m, d = x2.shape
    tm = min(tm, m)
    ka, kb = ya2.shape[1], yb2.shape[1]
    tpm = tiles_per_mod_tokens // tm
    row = lambda width, col: pl.BlockSpec((tm, width), lambda i: (i, col))
    return pl.pallas_call(
        _up_out_kernel,
        grid=(m // tm,),
        in_specs=[row(ka, 0), row(kb, 0), row(d, 0), row(d, 1), row(d, 0),
                  pl.BlockSpec((1, 6, d), lambda i: (i // tpm, 0, 0)),
                  _resident((1, d)), _resident((ka, d)), _resident((kb, d)), _resident((d, d))],
        out_specs=[row(d, 0), row(d, 0)],
        out_shape=[jax.ShapeDtypeStruct((m, d), F32), jax.ShapeDtypeStruct((m, d), BF16)],
        compiler_params=_cparams(("parallel",)),
        name="up_out",
    )(ya2, yb2, gates, gates, x2, mod3, norm2_g.reshape(1, d), wa, wb, w_out)


def _ffn1_kernel(h_ref, w1_ref, w3_ref, o_ref):
    h = h_ref[...]
    u1 = _dot(h, w1_ref[...])
    u3 = _dot(h, w3_ref[...])
    o_ref[...] = (u1 * _sigmoid(u1) * u3).astype(BF16)


def _ffn1(h2, w13, tm=1024, tn=PROJ_TN):
    m, d = h2.shape
    tm = min(tm, m)
    d_ff = w13.shape[1] // 2
    nj = d_ff // tn
    return pl.pallas_call(
        _ffn1_kernel,
        grid=(m // tm, nj),
        in_specs=[pl.BlockSpec((tm, d), lambda i, j: (i, 0)),
                  pl.BlockSpec((d, tn), lambda i, j: (0, j)),
                  pl.BlockSpec((d, tn), lambda i, j: (0, nj + j))],
        out_specs=pl.BlockSpec((tm, tn), lambda i, j: (i, j)),
        out_shape=jax.ShapeDtypeStruct((m, d_ff), BF16),
        compiler_params=_cparams(("parallel", "arbitrary")),
        name="ffn1",
    )(h2, w13, w13)


def _ffn2_kernel(a_ref, x_ref, mod_ref, g_ref, w_ref, o_ref):
    rows = x_ref.shape[0] // ROW_SPLIT
    for part in range(ROW_SPLIT):
        sl = slice(part * rows, (part + 1) * rows)
        x2 = x_ref[sl, :] + mod_ref[0, 5:6, :] * _dot(a_ref[sl, :], w_ref[...])
        y = x2 * lax.rsqrt(jnp.mean(x2 * x2, axis=-1, keepdims=True) + NORM_EPS)
        o_ref[sl, :] = y * g_ref[...]


def _ffn2(act, w2, x1, mod3, tiles_per_mod_tokens, final_g, tm=256):
    m, d = x1.shape
    tm = min(tm, m)
    d_ff = act.shape[1]
    tpm = tiles_per_mod_tokens // tm
    return pl.pallas_call(
        _ffn2_kernel,
        grid=(m // tm,),
        in_specs=[pl.BlockSpec((tm, d_ff), lambda i: (i, 0)),
                  pl.BlockSpec((tm, d), lambda i: (i, 0)),
                  pl.BlockSpec((1, 6, d), lambda i: (i // tpm, 0, 0)),
                  _resident((1, d)), _resident((d_ff, d))],
        out_specs=pl.BlockSpec((tm, d), lambda i: (i, 0)),
        out_shape=jax.ShapeDtypeStruct((m, d), F32),
        compiler_params=_cparams(("parallel",)),
        name="ffn2",
    )(act, x1, mod3, final_g.reshape(1, d), w2)


def _trunk(x, mod, s0, latent, p):
    b, t, d = x.shape
    m = b * t
    d_rwkv = p["d_rwkv"]
    d_pool = d - d_rwkv
    heads = d_rwkv // HEAD
    x2 = x.reshape(m, d)
    nb = mod.shape[0]
    mod3 = mod.reshape(nb, 6, d)
    tokens_per_mod = t if nb == b else m

    proj, gates = _in_proj(x2, p["norm1_g"], mod3, tokens_per_mod, p["w_in"], p["rw_cols_p"] + d_pool)
    proj3 = proj.reshape(b, t, proj.shape[1])

    r, k, v, kk, lwf, lwb, af, ab, g, bg = _prep(
        proj3, p["mu"], p["w0"], p["w2"], p["a0"], p["a2"], p["g2"], p["k_k"], p["k_a"], p["r_k"],
        p["ones_bd"], d_rwkv, p["rw_cols_p"])

    if s0 is None:
        s0 = jnp.zeros((b, 2, heads, HEAD, HEAD), F32)
    yf, yb, s_fin = _scan(r, k, v, kk, lwf, af, lwb, ab, p["k_a"], s0)
    ya = _post(yf, yb, g, bg, p["lnx_g"], p["lnx_b"], p["ones_bd"])

    pool_col0 = p["rw_cols_p"]
    if latent:
        yb_pool = _pool_lat(proj3, pool_col0, p["pool_w"], p["pool_scale"])
    else:
        yb_pool = _pool_ctx(proj3, pool_col0, p["pool_w"], p["pool_scale"])

    x1, h2 = _up_out(ya.reshape(m, d_rwkv), yb_pool.reshape(m, d_pool), gates, p["w_up_a"], p["w_up_b"],
                     p["w_out"], x2, mod3, tokens_per_mod, p["norm2_g"])
    act = _ffn1(h2, p["ffn_w13"])
    y = _ffn2(act, p["ffn_w2"], x1, mod3, tokens_per_mod, p["final_g"])
    return y.reshape(b, t, d), s_fin


def kernel(x_prompt, x_sample, c, state_rwkv, c_ctx, w_mod, b_mod, norm1_g, w_in, shift_mu, w0, w2, a0, a2, g2,
           k_k, k_a, r_k, lnx_g, lnx_b, w_up_a, pool_w, pool_scale, w_up_b, w_out, norm2_g, ffn_w13, ffn_w2,
           final_g):
    depth = w_in.shape[0]
    assert depth == 1, "final norm is fused into the last layer: single-layer trunk only"
    d = x_prompt.shape[-1]
    d_rwkv = w_up_a.shape[1]
    rw_cols = shift_mu.shape[1]
    rw_cols_p = -(-rw_cols // PROJ_TN) * PROJ_TN
    assert x_prompt.shape[1] == SEQ_TILE and x_sample.shape[1] % SEQ_TILE == 0

    l = 0
    w_in_l = w_in[l].astype(BF16)
    w_in_p = jnp.concatenate(
        [w_in_l[:, :rw_cols], jnp.zeros((d, rw_cols_p - rw_cols), BF16), w_in_l[:, rw_cols:]], axis=1)
    ones_bd = jnp.asarray(np.kron(np.eye(HEADS_PER_GROUP, dtype=np.float32),
                                  np.ones((HEAD, HEAD), np.float32))).astype(BF16)
    p = {
        "d_rwkv": d_rwkv, "rw_cols_p": rw_cols_p,
        "norm1_g": norm1_g[l], "w_in": w_in_p,
        "mu": jnp.pad(shift_mu[l], (0, rw_cols_p - rw_cols)).reshape(1, rw_cols_p),
        "w0": w0[l], "w2": w2[l], "a0": a0[l], "a2": a2[l], "g2": g2[l],
        "k_k": k_k[l], "k_a": k_a[l], "r_k": r_k[l].reshape(-1), "lnx_g": lnx_g[l], "lnx_b": lnx_b[l],
        "ones_bd": ones_bd,
        "w_up_a": w_up_a[l].astype(BF16), "w_up_b": w_up_b[l].astype(BF16),
        "pool_w": pool_w[l].astype(BF16), "pool_scale": pool_scale[l],
        "w_out": w_out[l].astype(BF16), "norm2_g": norm2_g[l],
        "ffn_w13": ffn_w13[l].astype(BF16), "ffn_w2": ffn_w2[l].astype(BF16), "final_g": final_g,
    }
    cond = jnp.concatenate([c_ctx[None, :], c], axis=0)
    mod = _mod(cond, w_mod[l], b_mod[l])
    y_prompt, s_ctx = _trunk(x_prompt, mod[:1], None, False, p)
    y_sample, _ = _trunk(x_sample, mod[1:], state_rwkv[:, l], True, p)
    new_state = s_ctx[:, None].astype(x_prompt.dtype)
    return (y_prompt, y_sample, new_state)
```

```python
import functools

import numpy as np
import jax
import jax.numpy as jnp
from jax import lax
from jax.experimental import pallas as pl
from jax.experimental.pallas import tpu as pltpu

F32 = jnp.float32
BF16 = jnp.bfloat16

HEAD = 64
POOL_WINDOWS = (2, 4, 8, 16)
GRID_W = 64
DECAY_LORA = 64
AAA_LORA = 64
GATE_LORA = 128
NORM_EPS = 1e-6
GN_EPS = 64e-5

LANE_TILE = 128
MXU_DIM = 256
HEADS_PER_GROUP = MXU_DIM // HEAD
VMEM_LIMIT = 56 * 1024 * 1024

CHUNK = 64
SEQ_TILE = 256
PREP_TILE = 512
PROJ_TN = 512
SCAN_BLOCK = 1024
SCAN_WAVE = 4


def _cparams(sem):
    return pltpu.CompilerParams(dimension_semantics=sem, vmem_limit_bytes=VMEM_LIMIT)


def _dot(a, b, dims=(((1,), (0,)), ((), ())), precision=None):
    return lax.dot_general(a, b, dims, precision=precision, preferred_element_type=F32)


_NN = (((1,), (0,)), ((), ()))
_NT = (((1,), (1,)), ((), ()))


def _split2(x):
    hi = x.astype(BF16)
    lo = (x - hi.astype(F32)).astype(BF16)
    return hi, lo


def _split3(x):
    hi = x.astype(BF16)
    r1 = x - hi.astype(F32)
    mid = r1.astype(BF16)
    lo = (r1 - mid.astype(F32)).astype(BF16)
    return hi, mid, lo


def _mm(a, b, passes):
    if passes == 1:
        return _dot(a.astype(BF16), b.astype(BF16))
    assert passes == 3
    ah, al = _split2(a)
    bh, bl = _split2(b)
    return _dot(ah, bh) + (_dot(ah, bl) + _dot(al, bh))


def _mm_exact_rhs(a, b_bf16):
    a0, a1 = _split2(a)
    return _dot(a0, b_bf16) + _dot(a1, b_bf16)


def _sigmoid(x):
    return 1.0 / (1.0 + jnp.exp(-x))


def _mod_kernel(c_ref, w_ref, b_ref, o_ref):
    c = c_ref[...]
    s = (c * _sigmoid(c)).astype(BF16)
    o_ref[...] = _dot(s, w_ref[...].astype(BF16)) + b_ref[...]


def _mod(cond, w_mod, b_mod):
    nb, d = cond.shape
    rows = -(-nb // 16) * 16
    cond_p = jnp.pad(cond, ((0, rows - nb), (0, 0)))
    n = w_mod.shape[1]
    tn = 1024
    out = pl.pallas_call(
        _mod_kernel,
        grid=(n // tn,),
        in_specs=[pl.BlockSpec((rows, d), lambda j: (0, 0)),
                  pl.BlockSpec((d, tn), lambda j: (0, j)),
                  pl.BlockSpec((1, tn), lambda j: (0, j))],
        out_specs=pl.BlockSpec((rows, tn), lambda j: (0, j)),
        out_shape=jax.ShapeDtypeStruct((rows, n), F32),
        compiler_params=_cparams(("arbitrary",)),
        name="mod",
    )(cond_p, w_mod, b_mod.reshape(1, n))
    return out[:nb]


NORM_ROWS = 16


def _in_kernel(x_ref, g_ref, mod_ref, w_ref, om_ref, og_ref, h_ref, *, nj_main):
    j = pl.program_id(1)

    @pl.when(j == 0)
    def _():
        gain = g_ref[...]
        scale = 1.0 + mod_ref[0, 1:2, :]
        shift = mod_ref[0, 0:1, :]

        def norm_rows(i, carry):
            rows = pl.ds(pl.multiple_of(i * NORM_ROWS, NORM_ROWS), NORM_ROWS)
            x = x_ref[rows, :]
            y = x * lax.rsqrt(jnp.mean(x * x, axis=-1, keepdims=True) + NORM_EPS)
            h_ref[rows, :] = ((y * gain) * scale + shift).astype(BF16)
            return carry

        lax.fori_loop(0, x_ref.shape[0] // NORM_ROWS, norm_rows, 0, unroll=8)

    @pl.when(j < nj_main)
    def _():
        om_ref[...] = _dot(h_ref[...], w_ref[...])

    @pl.when(j >= nj_main)
    def _():
        og_ref[...] = _dot(h_ref[...], w_ref[...])


def _in_proj(x2, norm_g, mod3, tiles_per_mod_tokens, w_bf16, n_main, tm=1024, tn=PROJ_TN):
    m, d = x2.shape
    tm = min(tm, m)
    n = w_bf16.shape[1]
    tpm = tiles_per_mod_tokens // tm
    nj_main = n_main // tn
    return pl.pallas_call(
        functools.partial(_in_kernel, nj_main=nj_main),
        grid=(m // tm, n // tn),
        in_specs=[pl.BlockSpec((tm, d), lambda i, j: (i, 0)),
                  pl.BlockSpec((1, d), lambda i, j: (0, 0)),
                  pl.BlockSpec((1, 6, d), lambda i, j: (i // tpm, 0, 0)),
                  pl.BlockSpec((d, tn), lambda i, j: (0, j))],
        out_specs=[pl.BlockSpec((tm, tn), lambda i, j: (i, jnp.minimum(j, nj_main - 1))),
                   pl.BlockSpec((tm, tn), lambda i, j: (i, jnp.maximum(j - nj_main, 0)))],
        out_shape=[jax.ShapeDtypeStruct((m, n_main), F32), jax.ShapeDtypeStruct((m, n - n_main), F32)],
        scratch_shapes=[pltpu.VMEM((tm, d), BF16)],
        compiler_params=_cparams(("parallel", "arbitrary")),
        name="in_proj",
    )(x2, norm_g.reshape(1, d), mod3, w_bf16)


P_LORA_DECAY = 3
P_LORA = 1
DECAY_GAIN = float(np.exp(-0.5))


def _head_sum(x, ones_bd):
    return _mm_exact_rhs(x, ones_bd)


def _prep_kernel(main_ref, prev_ref, next_ref, mu_ref, w0_ref, w2_ref, a0_ref, a2_ref, g2_ref,
                 kkw_ref, ka_ref, rk_ref, ones_ref, r_o, k_o, v_o, kk_o, lwf_o, lwb_o, af_o, ab_o, g_o, bg_o,
                 *, n_tiles, d_rwkv):
    t = pl.program_id(1)
    rows = main_ref.shape[1]
    row8 = lax.broadcasted_iota(jnp.int32, (8, 1), 0)
    has_prev = t > 0
    has_next = t < n_tiles - 1

    def shifted(c0, c1):
        z = main_ref[0, :, c0:c1]
        p = jnp.where(has_prev, prev_ref[0, 7:8, c0:c1], 0.0)
        n = jnp.where(has_next, next_ref[0, 0:1, c0:c1], 0.0)
        zp = pltpu.roll(z, 1, 0)
        zn = pltpu.roll(z, rows - 1, 0)
        zp = jnp.concatenate([jnp.where(row8 == 0, p, zp[:8]), zp[8:]], axis=0)
        zn = jnp.concatenate([zn[:rows - 8], jnp.where(row8 == 7, n, zn[rows - 8:])], axis=0)
        mu = mu_ref[:, c0:c1]
        return (1.0 - mu) * z + (0.5 * mu) * (zp + zn)

    dr = d_rwkv
    r = shifted(0, dr)
    r_o[0] = r.astype(BF16)
    k = shifted(dr, 2 * dr)
    k_o[0] = k.astype(BF16)
    v = shifted(2 * dr, 3 * dr)
    v_o[0] = v.astype(BF16)

    kkv = k * kkw_ref[...]
    ones_bd = ones_ref[...]
    for j in range(dr // MXU_DIM):
        sl = slice(j * MXU_DIM, (j + 1) * MXU_DIM)
        x = kkv[:, sl]
        ss = _head_sum(x * x, ones_bd)
        kk_o[0, :, sl] = (x * lax.rsqrt(jnp.maximum(ss, 1e-24))).astype(BF16)

    c = 3 * dr
    sm = shifted(c, c + 2 * DECAY_LORA + 2 * AAA_LORA + GATE_LORA)
    o = 0
    for d, lw_o in enumerate((lwf_o, lwb_o)):
        wd = jnp.tanh(sm[:, o:o + DECAY_LORA])
        o += DECAY_LORA
        xw = w0_ref[d:d + 1, :] + _mm(wd, w2_ref[d], passes=P_LORA_DECAY)
        lw_o[0] = -DECAY_GAIN * _sigmoid(xw)
    for d, a_o in enumerate((af_o, ab_o)):
        ad = sm[:, o:o + AAA_LORA]
        o += AAA_LORA
        a_o[0] = _sigmoid(a0_ref[d:d + 1, :] + _mm(ad, a2_ref[d], passes=P_LORA))
    gd = _sigmoid(sm[:, o:o + GATE_LORA])
    g = _mm(gd, g2_ref[...], passes=P_LORA)
    g_o[0] = g.astype(BF16)

    for j in range(dr // MXU_DIM):
        sl = slice(j * MXU_DIM, (j + 1) * MXU_DIM)
        ka = ka_ref[:, sl]
        kj = k[:, sl]
        kd_sum = kj * (1.0 + (af_o[0, :, sl] - 1.0) * ka) + kj * (1.0 + (ab_o[0, :, sl] - 1.0) * ka)
        bonus = _head_sum(r[:, sl] * kd_sum * rk_ref[:, sl], ones_bd) * v[:, sl]
        bg_o[0, :, sl] = (bonus * g[:, sl]).astype(BF16)


def _prep(proj3, mu_p, w0, w2, a0, a2, g2, k_k, k_a, r_k, ones_bd, d_rwkv, rw_cols_p):
    b, t, _ = proj3.shape
    tp = min(PREP_TILE, t)
    n_tiles = t // tp
    sub = tp // 8
    last8 = t // 8 - 1
    out_sd = jax.ShapeDtypeStruct((b, t, d_rwkv), F32)
    half_sd = jax.ShapeDtypeStruct((b, t, d_rwkv), BF16)
    o_spec = pl.BlockSpec((1, tp, d_rwkv), lambda i, j: (i, j, 0))

    def full(a):
        nd = a.ndim
        return pl.BlockSpec(a.shape, lambda i, j: (0,) * nd)

    params = (mu_p, w0, w2, a0, a2, g2, k_k.reshape(1, d_rwkv), k_a.reshape(1, d_rwkv),
              r_k.reshape(1, d_rwkv), ones_bd)
    return pl.pallas_call(
        functools.partial(_prep_kernel, n_tiles=n_tiles, d_rwkv=d_rwkv),
        grid=(b, n_tiles),
        in_specs=[pl.BlockSpec((1, tp, rw_cols_p), lambda i, j: (i, j, 0)),
                  pl.BlockSpec((1, 8, rw_cols_p), lambda i, j: (i, jnp.maximum(j * sub - 1, 0), 0)),
                  pl.BlockSpec((1, 8, rw_cols_p), lambda i, j: (i, jnp.minimum((j + 1) * sub, last8), 0)),
                  ] + [full(a) for a in params],
        out_specs=[o_spec] * 10,
        out_shape=[half_sd] * 4 + [out_sd] * 4 + [half_sd] * 2,
        compiler_params=_cparams(("parallel", "arbitrary")),
        name="rwkv_prep",
    )(proj3, proj3, proj3, *params)


def _block_diag_rhs(y, bd):
    yb = y.astype(BF16)
    per_tile = LANE_TILE // HEAD
    zero = jnp.zeros((y.shape[0], LANE_TILE), BF16)
    rows = []
    for h in range(HEADS_PER_GROUP):
        t = h // per_tile
        tiles = [zero] * (MXU_DIM // LANE_TILE)
        tiles[t] = yb[:, t * LANE_TILE:(t + 1) * LANE_TILE] * bd[h % per_tile]
        rows.append(jnp.concatenate(tiles, axis=1))
    return jnp.concatenate(rows, axis=0)


def _mmc(x, y_bd):
    return _dot(x.astype(BF16), y_bd)


def _scan_masks(rev):
    c = CHUNK
    ti = lax.broadcasted_iota(jnp.int32, (c, c), 0)
    si = lax.broadcasted_iota(jnp.int32, (c, c), 1)
    tri = jnp.where((si >= ti) if rev else (si <= ti), 1.0, 0.0).astype(BF16)
    tc = lax.broadcasted_iota(jnp.int32, (c, MXU_DIM), 0)
    sc = lax.broadcasted_iota(jnp.int32, (c, MXU_DIM), 1) % c
    before = (sc > tc) if rev else (sc < tc)
    upto = (sc >= tc) if rev else (sc <= tc)
    return tri, before, upto


def _scan_chunk_local(out, refs, sl, dir_masks, blk_masks, ka, last_row, bd):
    r_ref, k_ref, v_ref, kk_ref, lw_ref, a_ref = refs
    tri, before, upto = dir_masks
    eye, same16, off32, off64 = blk_masks
    c = CHUNK
    kk = kk_ref[0, sl, :].astype(F32)
    lw = lw_ref[0, sl, :]
    a = a_ref[0, sl, :]
    lw0, lw1, lw2 = _split3(lw)
    cum = _dot(tri, lw0) + (_dot(tri, lw1) + _dot(tri, lw2))
    yield
    e_in = jnp.exp(cum)
    e_inv = jnp.exp(-cum)
    al = kk * jnp.exp(cum - lw)
    rh = r_ref[0, sl, :].astype(F32) * e_in
    be = (kk * a) * e_inv
    kap = (k_ref[0, sl, :].astype(F32) * (1.0 + (a - 1.0) * ka)) * e_inv
    ptot = e_in[last_row:last_row + 1, :]
    ar = jnp.concatenate([al, rh], axis=0).astype(BF16)
    ab = _dot(ar, _block_diag_rhs(be, bd), _NT)
    ak = _dot(ar, _block_diag_rhs(kap, bd), _NT)
    yield
    lm = jnp.where(before, ab[:c], 0.0)
    aak = jnp.where(before, ak[:c], 0.0)
    kq = jnp.where(upto, ak[c:], 0.0)
    out["bq"] = jnp.where(upto, ab[c:], 0.0).astype(BF16)
    ld = jnp.where(same16, lm, 0.0)
    p = eye - ld
    ld_bd = _block_diag_rhs(ld, bd)
    l2 = _mmc(ld, ld_bd)
    yield
    pl2 = _mmc(jnp.concatenate([p, l2], axis=0), _block_diag_rhs(l2, bd))
    yield
    p = p + pl2[:c]
    l4 = pl2[c:]
    pl4 = _mmc(jnp.concatenate([p, l4], axis=0), _block_diag_rhs(l4, bd))
    yield
    p = p + pl4[:c]
    p8 = _mmc(p, _block_diag_rhs(pl4[c:], bd))
    v = v_ref[0, sl, :]
    akv = _mmc(jnp.concatenate([aak, kq], axis=0), _block_diag_rhs(v, bd))
    aak_v = akv[:c]
    out["kq_v"] = akv[c:]
    yield
    p = p + p8
    for off in (off32, off64):
        lo = jnp.where(off, lm, 0.0)
        t1 = _mmc(lo, _block_diag_rhs(p, bd))
        yield
        t2 = _mmc(p, _block_diag_rhs(t1, bd))
        yield
        p = p - t2
    wt = _mmc(p, _block_diag_rhs(al, bd))
    out["ut"] = _mmc(p, _block_diag_rhs(aak_v, bd))
    yield
    out["wr"] = jnp.concatenate([wt, rh], axis=0).astype(BF16)
    out["v"] = v.astype(BF16)
    out["bkT"] = jnp.transpose(jnp.concatenate([be, kap], axis=0)).astype(BF16)
    out["ptot"] = ptot


def _scan_chain(items, m_scr, y_ref, bd, bd_mask):
    c = CHUNK
    for loc, sl in items:
        m = m_scr[...]
        zs = _dot(loc["wr"], m.astype(BF16))
        yield
        u = -(zs[:c] + loc["ut"])
        uv = jnp.concatenate([u.astype(BF16), loc["v"]], axis=0)
        dm = _dot(loc["bkT"], uv)
        yu = _mmc(loc["bq"], _block_diag_rhs(u, bd))
        yield
        y_ref[0, sl, :] = (zs[c:] + loc["kq_v"] + yu).astype(BF16)
        pc = jnp.transpose(jnp.broadcast_to(loc["ptot"], (8, MXU_DIM)))
        m_scr[...] = jnp.broadcast_to(pc[:, 0:1], (MXU_DIM, MXU_DIM)) * (m + jnp.where(bd_mask, dm, 0.0))


def _run_lockstep(gens):
    gens = list(gens)
    while gens:
        alive = []
        for g in gens:
            try:
                next(g)
                alive.append(g)
            except StopIteration:
                pass
        gens = alive


def _scan_kernel(rf_ref, kf_ref, vf_ref, kkf_ref, lwf_ref, af_ref, rb_ref, kb_ref, vb_ref, kkb_ref, lwb_ref,
                 ab_ref, ka_ref, s0_ref, yf_ref, yb_ref, sfin_ref, mf_scr, mb_scr,
                 *, n_sub, n_blk):
    c = CHUNK
    blk = pl.program_id(2)

    @pl.when(blk == 0)
    def _():
        zero = jnp.zeros((HEAD, HEAD), F32)
        for d, scr in enumerate((mf_scr, mb_scr)):
            for h in range(HEADS_PER_GROUP):
                blocks = [zero] * HEADS_PER_GROUP
                blocks[h] = jnp.transpose(s0_ref[0, d, h])
                scr[h * HEAD:(h + 1) * HEAD, :] = jnp.concatenate(blocks, axis=1)

    tc = lax.broadcasted_iota(jnp.int32, (c, MXU_DIM), 0)
    sc = lax.broadcasted_iota(jnp.int32, (c, MXU_DIM), 1) % c
    same16 = (sc // 16) == (tc // 16)
    same32 = (sc // 32) == (tc // 32)
    blk_masks = (jnp.where(sc == tc, 1.0, 0.0), same16,
                 jnp.logical_and(same32, jnp.logical_not(same16)), jnp.logical_not(same32))
    bd_mask = (lax.broadcasted_iota(jnp.int32, (MXU_DIM, MXU_DIM), 0) // HEAD) == \
              (lax.broadcasted_iota(jnp.int32, (MXU_DIM, MXU_DIM), 1) // HEAD)
    lane_head = lax.broadcasted_iota(jnp.int32, (c, LANE_TILE), 1) // HEAD
    bd = tuple(jnp.where(lane_head == q, 1.0, 0.0).astype(BF16) for q in range(LANE_TILE // HEAD))
    ka = ka_ref[...]

    refs_f = (rf_ref, kf_ref, vf_ref, kkf_ref, lwf_ref, af_ref)
    refs_b = (rb_ref, kb_ref, vb_ref, kkb_ref, lwb_ref, ab_ref)
    masks_f = _scan_masks(False)
    masks_b = _scan_masks(True)
    slices = [slice(j * c, (j + 1) * c) for j in range(n_sub)]
    items_f = [({}, sl) for sl in slices]
    items_b = [({}, sl) for sl in reversed(slices)]
    chains = []
    for w0 in range(0, n_sub, SCAN_WAVE):
        wave_f = items_f[w0:w0 + SCAN_WAVE]
        wave_b = items_b[w0:w0 + SCAN_WAVE]
        gens = []
        for (of, sf), (ob, sb) in zip(wave_f, wave_b):
            gens.append(_scan_chunk_local(of, refs_f, sf, masks_f, blk_masks, ka, c - 1, bd))
            gens.append(_scan_chunk_local(ob, refs_b, sb, masks_b, blk_masks, ka, 0, bd))
        _run_lockstep(gens + chains)
        chains = [_scan_chain(wave_f, mf_scr, yf_ref, bd, bd_mask),
                  _scan_chain(wave_b, mb_scr, yb_ref, bd, bd_mask)]
    _run_lockstep(chains)

    @pl.when(blk == n_blk - 1)
    def _():
        for d, scr in enumerate((mf_scr, mb_scr)):
            for h in range(HEADS_PER_GROUP):
                rows = scr[h * HEAD:(h + 1) * HEAD, :]
                sfin_ref[0, d, h] = jnp.transpose(rows[:, h * HEAD:(h + 1) * HEAD])


def _scan(r, k, v, kk, lwf, af, lwb, ab, k_a, s0):
    b, t, d_rwkv = r.shape
    groups = d_rwkv // MXU_DIM
    tb = min(SCAN_BLOCK, t)
    n_blk = t // tb
    n_sub = tb // CHUNK
    tok_f = pl.BlockSpec((1, tb, MXU_DIM), lambda i, g, j: (i, j, g))
    tok_b = pl.BlockSpec((1, tb, MXU_DIM), lambda i, g, j: (i, n_blk - 1 - j, g))
    st = pl.BlockSpec((1, 2, HEADS_PER_GROUP, HEAD, HEAD), lambda i, g, j: (i, 0, g, 0, 0))
    y_sd = jax.ShapeDtypeStruct((b, t, d_rwkv), BF16)
    return pl.pallas_call(
        functools.partial(_scan_kernel, n_sub=n_sub, n_blk=n_blk),
        grid=(b, groups, n_blk),
        in_specs=[tok_f] * 6 + [tok_b] * 6 + [pl.BlockSpec((1, MXU_DIM), lambda i, g, j: (0, g)), st],
        out_specs=[tok_f, tok_b, st],
        out_shape=[y_sd, y_sd, jax.ShapeDtypeStruct(s0.shape, F32)],
        scratch_shapes=[pltpu.VMEM((MXU_DIM, MXU_DIM), F32), pltpu.VMEM((MXU_DIM, MXU_DIM), F32)],
        compiler_params=_cparams(("parallel", "parallel", "arbitrary")),
        name="scan",
    )(r, k, v, kk, lwf, af, r, k, v, kk, lwb, ab, k_a.reshape(1, d_rwkv), s0)


def _post_kernel(yf_ref, yb_ref, g_ref, bg_ref, lg_ref, lb_ref, ones_ref, o_ref):
    ones_bd = ones_ref[...]
    inv_n = 1.0 / HEAD
    for q in range(o_ref.shape[2] // MXU_DIM):
        sl = slice(q * MXU_DIM, (q + 1) * MXU_DIM)
        y = yf_ref[0, :, sl].astype(F32) + yb_ref[0, :, sl].astype(F32)
        mu = _head_sum(y, ones_bd) * inv_n
        d = y - mu
        var = _head_sum(d * d, ones_bd) * inv_n
        yn = d * lax.rsqrt(var + GN_EPS) * lg_ref[:, sl] + lb_ref[:, sl]
        o_ref[0, :, sl] = (yn * g_ref[0, :, sl].astype(F32) + bg_ref[0, :, sl].astype(F32)).astype(BF16)


def _post(yf, yb, g, bg, lnx_g, lnx_b, ones_bd):
    b, t, d_rwkv = yf.shape
    tp = min(512, t)
    tok = pl.BlockSpec((1, tp, d_rwkv), lambda i, j: (i, j, 0))
    par = pl.BlockSpec((1, d_rwkv), lambda i, j: (0, 0))
    return pl.pallas_call(
        _post_kernel,
        grid=(b, t // tp),
        in_specs=[tok] * 4 + [par] * 2 + [pl.BlockSpec((MXU_DIM, MXU_DIM), lambda i, j: (0, 0))],
        out_specs=tok,
        out_shape=jax.ShapeDtypeStruct((b, t, d_rwkv), BF16),
        compiler_params=_cparams(("parallel", "arbitrary")),
        name="rwkv_post",
    )(yf, yb, g, bg, lnx_g.reshape(1, d_rwkv), lnx_b.reshape(1, d_rwkv), ones_bd)


def _window_bounds(n, w):
    t = np.arange(n)
    lo = np.clip(t - w // 2, 0, n)
    hi = np.clip(t - w // 2 + w, 0, n)
    return lo, hi


def _window_matrix(n, w):
    lo, hi = _window_bounds(n, w)
    s = np.arange(n)[None, :]
    m = ((s >= lo[:, None]) & (s < hi[:, None])).astype(np.float32)
    return m / (hi - lo)[:, None].astype(np.float32)


def _pool_ctx_kernel(z_ref, band_ref, pw_ref, ps_ref, o_ref):
    z = z_ref[0]
    m = _dot(band_ref[0], z, precision=lax.Precision.HIGHEST)
    d = (m - z).astype(BF16)
    o_ref[0] = (_dot(d, pw_ref[0]) * ps_ref[...]).astype(BF16)


def _pool_ctx(proj3, col0, pool_w_bf16, pool_scale):
    b, t, _ = proj3.shape
    ng, pg, _ = pool_w_bf16.shape
    band = jnp.asarray(np.stack([_window_matrix(t, w) for w in POOL_WINDOWS]))
    cb = col0 // pg
    return pl.pallas_call(
        _pool_ctx_kernel,
        grid=(b, ng),
        in_specs=[pl.BlockSpec((1, t, pg), lambda i, q: (i, 0, cb + q)),
                  pl.BlockSpec((1, t, t), lambda i, q: (q, 0, 0)),
                  pl.BlockSpec((1, pg, pg), lambda i, q: (q, 0, 0)),
                  pl.BlockSpec((1, pg), lambda i, q: (0, q))],
        out_specs=pl.BlockSpec((1, t, pg), lambda i, q: (i, 0, q)),
        out_shape=jax.ShapeDtypeStruct((b, t, ng * pg), BF16),
        compiler_params=_cparams(("parallel", "arbitrary")),
        name="pool_ctx",
    )(proj3, band, pool_w_bf16, pool_scale.reshape(1, ng * pg))


def _pool_lat_kernel(z_ref, band_ref, pw_ref, ps_ref, o_ref, m1_ref, cs_ref, *, n_rows):
    tile = band_ref.shape[1]
    n_tok = n_rows * GRID_W
    band = band_ref[0]
    for i in range(n_tok // tile):
        sl = slice(i * tile, (i + 1) * tile)
        m1_ref[sl, :] = _dot(band, z_ref[0, sl, :], precision=lax.Precision.HIGHEST)
    cs_ref[0:GRID_W, :] = jnp.zeros((GRID_W, cs_ref.shape[1]), F32)
    for rr in range(n_rows):
        cs_ref[(rr + 1) * GRID_W:(rr + 2) * GRID_W, :] = (
            cs_ref[rr * GRID_W:(rr + 1) * GRID_W, :] + m1_ref[rr * GRID_W:(rr + 1) * GRID_W, :])
    group = pl.program_id(1)
    for gi, w in enumerate(POOL_WINDOWS):
        @pl.when(group == gi)
        def _(w=w):
            lo, hi = _window_bounds(n_rows, w)
            for rr in range(n_rows):
                l, h = int(lo[rr]), int(hi[rr])
                m2 = (cs_ref[h * GRID_W:(h + 1) * GRID_W, :]
                      - cs_ref[l * GRID_W:(l + 1) * GRID_W, :]) / float(h - l)
                sl = slice(rr * GRID_W, (rr + 1) * GRID_W)
                m1_ref[sl, :] = m2 - z_ref[0, sl, :]
    pw = pw_ref[0]
    ps = ps_ref[...]
    for i in range(n_tok // tile):
        sl = slice(i * tile, (i + 1) * tile)
        o_ref[0, sl, :] = (_dot(m1_ref[sl, :].astype(BF16), pw) * ps).astype(BF16)


def _pool_lat(proj3, col0, pool_w_bf16, pool_scale):
    b, t, _ = proj3.shape
    ng, pg, _ = pool_w_bf16.shape
    n_rows = t // GRID_W
    tile = MXU_DIM
    band = jnp.asarray(np.stack(
        [np.kron(np.eye(tile // GRID_W, dtype=np.float32), _window_matrix(GRID_W, w)) for w in POOL_WINDOWS]))
    cb = col0 // pg
    return pl.pallas_call(
        functools.partial(_pool_lat_kernel, n_rows=n_rows),
        grid=(b, ng),
        in_specs=[pl.BlockSpec((1, t, pg), lambda i, q: (i, 0, cb + q)),
                  pl.BlockSpec((1, tile, tile), lambda i, q: (q, 0, 0)),
                  pl.BlockSpec((1, pg, pg), lambda i, q: (q, 0, 0)),
                  pl.BlockSpec((1, pg), lambda i, q: (0, q))],
        out_specs=pl.BlockSpec((1, t, pg), lambda i, q: (i, 0, q)),
        out_shape=jax.ShapeDtypeStruct((b, t, ng * pg), BF16),
        scratch_shapes=[pltpu.VMEM((t, pg), F32), pltpu.VMEM((t + GRID_W, pg), F32)],
        compiler_params=_cparams(("parallel", "arbitrary")),
        name="pool_lat",
    )(proj3, band, pool_w_bf16, pool_scale.reshape(1, ng * pg))


ROW_SPLIT = 2


def _resident(shape):
    nd = len(shape)
    return pl.BlockSpec(shape, lambda i: (0,) * nd, pipeline_mode=pl.Buffered(1))


def _up_out_kernel(ya_ref, yb_ref, ga_ref, gb_ref, x_ref, mod_ref, g_ref, wa_ref, wb_ref, wo_ref,
                   x1_ref, h2_ref):
    rows = x_ref.shape[0] // ROW_SPLIT
    for part in range(ROW_SPLIT):
        sl = slice(part * rows, (part + 1) * rows)
        ua = _dot(ya_ref[sl, :], wa_ref[...])
        ub = _dot(yb_ref[sl, :], wb_ref[...])
        merged = (_sigmoid(ga_ref[sl, :]) * ua + _sigmoid(gb_ref[sl, :]) * ub).astype(BF16)
        x1 = x_ref[sl, :] + mod_ref[0, 2:3, :] * _dot(merged, wo_ref[...])
        x1_ref[sl, :] = x1
        y = x1 * lax.rsqrt(jnp.mean(x1 * x1, axis=-1, keepdims=True) + NORM_EPS)
        h2 = (y * g_ref[...]) * (1.0 + mod_ref[0, 4:5, :]) + mod_ref[0, 3:4, :]
        h2_ref[sl, :] = h2.astype(BF16)


def _up_out(ya2, yb2, gates, wa, wb, w_out, x2, mod3, tiles_per_mod_tokens, norm2_g, tm=256):
    m, d = x2.shape
    tm = min(tm, m)
    ka, kb = ya2.shape[1], yb2.shape[1]
    tpm = tiles_per_mod_tokens // tm
    row = lambda width, col: pl.BlockSpec((tm, width), lambda i: (i, col))
    return pl.pallas_call(
        _up_out_kernel,
        grid=(m // tm,),
        in_specs=[row(ka, 0), row(kb, 0), row(d, 0), row(d, 1), row(d, 0),
                  pl.BlockSpec((1, 6, d), lambda i: (i // tpm, 0, 0)),
                  _resident((1, d)), _resident((ka, d)), _resident((kb, d)), _resident((d, d))],
        out_specs=[row(d, 0), row(d, 0)],
        out_shape=[jax.ShapeDtypeStruct((m, d), F32), jax.ShapeDtypeStruct((m, d), BF16)],
        compiler_params=_cparams(("parallel",)),
        name="up_out",
    )(ya2, yb2, gates, gates, x2, mod3, norm2_g.reshape(1, d), wa, wb, w_out)


def _ffn1_kernel(h_ref, w1_ref, w3_ref, o_ref):
    h = h_ref[...]
    u1 = _dot(h, w1_ref[...])
    u3 = _dot(h, w3_ref[...])
    o_ref[...] = (u1 * _sigmoid(u1) * u3).astype(BF16)


def _ffn1(h2, w13, tm=2048, tn=PROJ_TN):
    m, d = h2.shape
    tm = min(tm, m)
    d_ff = w13.shape[1] // 2
    nj = d_ff // tn
    return pl.pallas_call(
        _ffn1_kernel,
        grid=(m // tm, nj),
        in_specs=[pl.BlockSpec((tm, d), lambda i, j: (i, 0)),
                  pl.BlockSpec((d, tn), lambda i, j: (0, j)),
                  pl.BlockSpec((d, tn), lambda i, j: (0, nj + j))],
        out_specs=pl.BlockSpec((tm, tn), lambda i, j: (i, j)),
        out_shape=jax.ShapeDtypeStruct((m, d_ff), BF16),
        compiler_params=_cparams(("parallel", "arbitrary")),
        name="ffn1",
    )(h2, w13, w13)


def _ffn2_kernel(a_ref, x_ref, mod_ref, g_ref, w_ref, o_ref):
    rows = x_ref.shape[0] // ROW_SPLIT
    for part in range(ROW_SPLIT):
        sl = slice(part * rows, (part + 1) * rows)
        x2 = x_ref[sl, :] + mod_ref[0, 5:6, :] * _dot(a_ref[sl, :], w_ref[...])
        y = x2 * lax.rsqrt(jnp.mean(x2 * x2, axis=-1, keepdims=True) + NORM_EPS)
        o_ref[sl, :] = y * g_ref[...]


def _ffn2(act, w2, x1, mod3, tiles_per_mod_tokens, final_g, tm=256):
    m, d = x1.shape
    tm = min(tm, m)
    d_ff = act.shape[1]
    tpm = tiles_per_mod_tokens // tm
    return pl.pallas_call(
        _ffn2_kernel,
        grid=(m // tm,),
        in_specs=[pl.BlockSpec((tm, d_ff), lambda i: (i, 0)),
                  pl.BlockSpec((tm, d), lambda i: (i, 0)),
                  pl.BlockSpec((1, 6, d), lambda i: (i // tpm, 0, 0)),
                  _resident((1, d)), _resident((d_ff, d))],
        out_specs=pl.BlockSpec((tm, d), lambda i: (i, 0)),
        out_shape=jax.ShapeDtypeStruct((m, d), F32),
        compiler_params=_cparams(("parallel",)),
        name="ffn2",
    )(act, x1, mod3, final_g.reshape(1, d), w2)


def _trunk(x, mod, s0, latent, p):
    b, t, d = x.shape
    m = b * t
    d_rwkv = p["d_rwkv"]
    d_pool = d - d_rwkv
    heads = d_rwkv // HEAD
    x2 = x.reshape(m, d)
    nb = mod.shape[0]
    mod3 = mod.reshape(nb, 6, d)
    tokens_per_mod = t if nb == b else m

    proj, gates = _in_proj(x2, p["norm1_g"], mod3, tokens_per_mod, p["w_in"], p["rw_cols_p"] + d_pool)
    proj3 = proj.reshape(b, t, proj.shape[1])

    r, k, v, kk, lwf, lwb, af, ab, g, bg = _prep(
        proj3, p["mu"], p["w0"], p["w2"], p["a0"], p["a2"], p["g2"], p["k_k"], p["k_a"], p["r_k"],
        p["ones_bd"], d_rwkv, p["rw_cols_p"])

    if s0 is None:
        s0 = jnp.zeros((b, 2, heads, HEAD, HEAD), F32)
    yf, yb, s_fin = _scan(r, k, v, kk, lwf, af, lwb, ab, p["k_a"], s0)
    ya = _post(yf, yb, g, bg, p["lnx_g"], p["lnx_b"], p["ones_bd"])

    pool_col0 = p["rw_cols_p"]
    if latent:
        yb_pool = _pool_lat(proj3, pool_col0, p["pool_w"], p["pool_scale"])
    else:
        yb_pool = _pool_ctx(proj3, pool_col0, p["pool_w"], p["pool_scale"])

    x1, h2 = _up_out(ya.reshape(m, d_rwkv), yb_pool.reshape(m, d_pool), gates, p["w_up_a"], p["w_up_b"],
                     p["w_out"], x2, mod3, tokens_per_mod, p["norm2_g"])
    act = _ffn1(h2, p["ffn_w13"])
    y = _ffn2(act, p["ffn_w2"], x1, mod3, tokens_per_mod, p["final_g"])
    return y.reshape(b, t, d), s_fin


def kernel(x_prompt, x_sample, c, state_rwkv, c_ctx, w_mod, b_mod, norm1_g, w_in, shift_mu, w0, w2, a0, a2, g2,
           k_k, k_a, r_k, lnx_g, lnx_b, w_up_a, pool_w, pool_scale, w_up_b, w_out, norm2_g, ffn_w13, ffn_w2,
           final_g):
    depth = w_in.shape[0]
    assert depth == 1, "final norm is fused into the last layer: single-layer trunk only"
    d = x_prompt.shape[-1]
    d_rwkv = w_up_a.shape[1]
    rw_cols = shift_mu.shape[1]
    rw_cols_p = -(-rw_cols // PROJ_TN) * PROJ_TN
    assert x_prompt.shape[1] == SEQ_TILE and x_sample.shape[1] % SEQ_TILE == 0

    l = 0
    w_in_l = w_in[l].astype(BF16)
    w_in_p = jnp.concatenate(
        [w_in_l[:, :rw_cols], jnp.zeros((d, rw_cols_p - rw_cols), BF16), w_in_l[:, rw_cols:]], axis=1)
    ones_bd = jnp.asarray(np.kron(np.eye(HEADS_PER_GROUP, dtype=np.float32),
                                  np.ones((HEAD, HEAD), np.float32))).astype(BF16)
    p = {
        "d_rwkv": d_rwkv, "rw_cols_p": rw_cols_p,
        "norm1_g": norm1_g[l], "w_in": w_in_p,
        "mu": jnp.pad(shift_mu[l], (0, rw_cols_p - rw_cols)).reshape(1, rw_cols_p),
        "w0": w0[l], "w2": w2[l], "a0": a0[l], "a2": a2[l], "g2": g2[l],
        "k_k": k_k[l], "k_a": k_a[l], "r_k": r_k[l].reshape(-1), "lnx_g": lnx_g[l], "lnx_b": lnx_b[l],
        "ones_bd": ones_bd,
        "w_up_a": w_up_a[l].astype(BF16), "w_up_b": w_up_b[l].astype(BF16),
        "pool_w": pool_w[l].astype(BF16), "pool_scale": pool_scale[l],
        "w_out": w_out[l].astype(BF16), "norm2_g": norm2_g[l],
        "ffn_w13": ffn_w13[l].astype(BF16), "ffn_w2": ffn_w2[l].astype(BF16), "final_g": final_g,
    }
    cond = jnp.concatenate([c_ctx[None, :], c], axis=0)
    mod = _mod(cond, w_mod[l], b_mod[l])
    y_prompt, s_ctx = _trunk(x_prompt, mod[:1], None, False, p)
    y_sample, _ = _trunk(x_sample, mod[1:], state_rwkv[:, l], True, p)
    new_state = s_ctx[:, None].astype(x_prompt.dtype)
    return (y_prompt, y_sample, new_state)
```

```python
import functools

import numpy as np
import jax
import jax.numpy as jnp
from jax import lax
from jax.experimental import pallas as pl
from jax.experimental.pallas import tpu as pltpu

F32 = jnp.float32
BF16 = jnp.bfloat16

HEAD = 64
POOL_WINDOWS = (2, 4, 8, 16)
GRID_W = 64
DECAY_LORA = 64
AAA_LORA = 64
GATE_LORA = 128
NORM_EPS = 1e-6
GN_EPS = 64e-5

LANE_TILE = 128
MXU_DIM = 256
HEADS_PER_GROUP = MXU_DIM // HEAD
VMEM_LIMIT = 56 * 1024 * 1024

CHUNK = 64
SEQ_TILE = 256
PREP_TILE = 512
PROJ_TN = 512
SCAN_BLOCK = 2048
SCAN_WAVE = 4


def _cparams(sem):
    return pltpu.CompilerParams(dimension_semantics=sem, vmem_limit_bytes=VMEM_LIMIT)


def _dot(a, b, dims=(((1,), (0,)), ((), ())), precision=None):
    return lax.dot_general(a, b, dims, precision=precision, preferred_element_type=F32)


_NN = (((1,), (0,)), ((), ()))
_NT = (((1,), (1,)), ((), ()))


def _split2(x):
    hi = x.astype(BF16)
    lo = (x - hi.astype(F32)).astype(BF16)
    return hi, lo


def _split3(x):
    hi = x.astype(BF16)
    r1 = x - hi.astype(F32)
    mid = r1.astype(BF16)
    lo = (r1 - mid.astype(F32)).astype(BF16)
    return hi, mid, lo


def _mm(a, b, passes):
    if passes == 1:
        return _dot(a.astype(BF16), b.astype(BF16))
    assert passes == 3
    ah, al = _split2(a)
    bh, bl = _split2(b)
    return _dot(ah, bh) + (_dot(ah, bl) + _dot(al, bh))


def _mm_exact_rhs(a, b_bf16):
    a0, a1 = _split2(a)
    return _dot(a0, b_bf16) + _dot(a1, b_bf16)


def _sigmoid(x):
    return 1.0 / (1.0 + jnp.exp(-x))


def _mod_kernel(c_ref, w_ref, b_ref, o_ref):
    c = c_ref[...]
    s = (c * _sigmoid(c)).astype(BF16)
    o_ref[...] = _dot(s, w_ref[...].astype(BF16)) + b_ref[...]


def _mod(cond, w_mod, b_mod):
    nb, d = cond.shape
    rows = -(-nb // 16) * 16
    cond_p = jnp.pad(cond, ((0, rows - nb), (0, 0)))
    n = w_mod.shape[1]
    tn = 1024
    out = pl.pallas_call(
        _mod_kernel,
        grid=(n // tn,),
        in_specs=[pl.BlockSpec((rows, d), lambda j: (0, 0)),
                  pl.BlockSpec((d, tn), lambda j: (0, j)),
                  pl.BlockSpec((1, tn), lambda j: (0, j))],
        out_specs=pl.BlockSpec((rows, tn), lambda j: (0, j)),
        out_shape=jax.ShapeDtypeStruct((rows, n), F32),
        compiler_params=_cparams(("arbitrary",)),
        name="mod",
    )(cond_p, w_mod, b_mod.reshape(1, n))
    return out[:nb]


NORM_ROWS = 16


def _in_kernel(x_ref, g_ref, mod_ref, w_ref, om_ref, og_ref, h_ref, *, nj_main):
    j = pl.program_id(1)

    @pl.when(j == 0)
    def _():
        gain = g_ref[...]
        scale = 1.0 + mod_ref[0, 1:2, :]
        shift = mod_ref[0, 0:1, :]

        def norm_rows(i, carry):
            rows = pl.ds(pl.multiple_of(i * NORM_ROWS, NORM_ROWS), NORM_ROWS)
            x = x_ref[rows, :]
            y = x * lax.rsqrt(jnp.mean(x * x, axis=-1, keepdims=True) + NORM_EPS)
            h_ref[rows, :] = ((y * gain) * scale + shift).astype(BF16)
            return carry

        lax.fori_loop(0, x_ref.shape[0] // NORM_ROWS, norm_rows, 0, unroll=8)

    @pl.when(j < nj_main)
    def _():
        om_ref[...] = _dot(h_ref[...], w_ref[...])

    @pl.when(j >= nj_main)
    def _():
        og_ref[...] = _dot(h_ref[...], w_ref[...])


def _in_proj(x2, norm_g, mod3, tiles_per_mod_tokens, w_bf16, n_main, tm=1024, tn=PROJ_TN):
    m, d = x2.shape
    tm = min(tm, m)
    n = w_bf16.shape[1]
    tpm = tiles_per_mod_tokens // tm
    nj_main = n_main // tn
    return pl.pallas_call(
        functools.partial(_in_kernel, nj_main=nj_main),
        grid=(m // tm, n // tn),
        in_specs=[pl.BlockSpec((tm, d), lambda i, j: (i, 0)),
                  pl.BlockSpec((1, d), lambda i, j: (0, 0)),
                  pl.BlockSpec((1, 6, d), lambda i, j: (i // tpm, 0, 0)),
                  pl.BlockSpec((d, tn), lambda i, j: (0, j))],
        out_specs=[pl.BlockSpec((tm, tn), lambda i, j: (i, jnp.minimum(j, nj_main - 1))),
                   pl.BlockSpec((tm, tn), lambda i, j: (i, jnp.maximum(j - nj_main, 0)))],
        out_shape=[jax.ShapeDtypeStruct((m, n_main), F32), jax.ShapeDtypeStruct((m, n - n_main), F32)],
        scratch_shapes=[pltpu.VMEM((tm, d), BF16)],
        compiler_params=_cparams(("parallel", "arbitrary")),
        name="in_proj",
    )(x2, norm_g.reshape(1, d), mod3, w_bf16)


P_LORA_DECAY = 3
P_LORA = 1
DECAY_GAIN = float(np.exp(-0.5))


def _head_sum(x, ones_bd):
    return _mm_exact_rhs(x, ones_bd)


def _prep_kernel(main_ref, prev_ref, next_ref, mu_ref, w0_ref, w2_ref, a0_ref, a2_ref, g2_ref,
                 kkw_ref, ka_ref, rk_ref, ones_ref, r_o, k_o, v_o, kk_o, lwf_o, lwb_o, af_o, ab_o, g_o, bg_o,
                 *, n_tiles, d_rwkv):
    t = pl.program_id(1)
    rows = main_ref.shape[1]
    row8 = lax.broadcasted_iota(jnp.int32, (8, 1), 0)
    has_prev = t > 0
    has_next = t < n_tiles - 1

    def shifted(c0, c1):
        z = main_ref[0, :, c0:c1]
        p = jnp.where(has_prev, prev_ref[0, 7:8, c0:c1], 0.0)
        n = jnp.where(has_next, next_ref[0, 0:1, c0:c1], 0.0)
        zp = pltpu.roll(z, 1, 0)
        zn = pltpu.roll(z, rows - 1, 0)
        zp = jnp.concatenate([jnp.where(row8 == 0, p, zp[:8]), zp[8:]], axis=0)
        zn = jnp.concatenate([zn[:rows - 8], jnp.where(row8 == 7, n, zn[rows - 8:])], axis=0)
        mu = mu_ref[:, c0:c1]
        return (1.0 - mu) * z + (0.5 * mu) * (zp + zn)

    dr = d_rwkv
    r = shifted(0, dr)
    r_o[0] = r.astype(BF16)
    k = shifted(dr, 2 * dr)
    k_o[0] = k.astype(BF16)
    v = shifted(2 * dr, 3 * dr)
    v_o[0] = v.astype(BF16)

    kkv = k * kkw_ref[...]
    ones_bd = ones_ref[...]
    for j in range(dr // MXU_DIM):
        sl = slice(j * MXU_DIM, (j + 1) * MXU_DIM)
        x = kkv[:, sl]
        ss = _head_sum(x * x, ones_bd)
        kk_o[0, :, sl] = (x * lax.rsqrt(jnp.maximum(ss, 1e-24))).astype(BF16)

    c = 3 * dr
    sm = shifted(c, c + 2 * DECAY_LORA + 2 * AAA_LORA + GATE_LORA)
    o = 0
    for d, lw_o in enumerate((lwf_o, lwb_o)):
        wd = jnp.tanh(sm[:, o:o + DECAY_LORA])
        o += DECAY_LORA
        xw = w0_ref[d:d + 1, :] + _mm(wd, w2_ref[d], passes=P_LORA_DECAY)
        lw_o[0] = -DECAY_GAIN * _sigmoid(xw)
    for d, a_o in enumerate((af_o, ab_o)):
        ad = sm[:, o:o + AAA_LORA]
        o += AAA_LORA
        a_o[0] = _sigmoid(a0_ref[d:d + 1, :] + _mm(ad, a2_ref[d], passes=P_LORA))
    gd = _sigmoid(sm[:, o:o + GATE_LORA])
    g = _mm(gd, g2_ref[...], passes=P_LORA)
    g_o[0] = g.astype(BF16)

    for j in range(dr // MXU_DIM):
        sl = slice(j * MXU_DIM, (j + 1) * MXU_DIM)
        ka = ka_ref[:, sl]
        kj = k[:, sl]
        kd_sum = kj * (1.0 + (af_o[0, :, sl] - 1.0) * ka) + kj * (1.0 + (ab_o[0, :, sl] - 1.0) * ka)
        bonus = _head_sum(r[:, sl] * kd_sum * rk_ref[:, sl], ones_bd) * v[:, sl]
        bg_o[0, :, sl] = (bonus * g[:, sl]).astype(BF16)


def _prep(proj3, mu_p, w0, w2, a0, a2, g2, k_k, k_a, r_k, ones_bd, d_rwkv, rw_cols_p):
    b, t, _ = proj3.shape
    tp = min(PREP_TILE, t)
    n_tiles = t // tp
    sub = tp // 8
    last8 = t // 8 - 1
    out_sd = jax.ShapeDtypeStruct((b, t, d_rwkv), F32)
    half_sd = jax.ShapeDtypeStruct((b, t, d_rwkv), BF16)
    o_spec = pl.BlockSpec((1, tp, d_rwkv), lambda i, j: (i, j, 0))

    def full(a):
        nd = a.ndim
        return pl.BlockSpec(a.shape, lambda i, j: (0,) * nd)

    params = (mu_p, w0, w2, a0, a2, g2, k_k.reshape(1, d_rwkv), k_a.reshape(1, d_rwkv),
              r_k.reshape(1, d_rwkv), ones_bd)
    return pl.pallas_call(
        functools.partial(_prep_kernel, n_tiles=n_tiles, d_rwkv=d_rwkv),
        grid=(b, n_tiles),
        in_specs=[pl.BlockSpec((1, tp, rw_cols_p), lambda i, j: (i, j, 0)),
                  pl.BlockSpec((1, 8, rw_cols_p), lambda i, j: (i, jnp.maximum(j * sub - 1, 0), 0)),
                  pl.BlockSpec((1, 8, rw_cols_p), lambda i, j: (i, jnp.minimum((j + 1) * sub, last8), 0)),
                  ] + [full(a) for a in params],
        out_specs=[o_spec] * 10,
        out_shape=[half_sd] * 4 + [out_sd] * 4 + [half_sd] * 2,
        compiler_params=_cparams(("parallel", "arbitrary")),
        name="rwkv_prep",
    )(proj3, proj3, proj3, *params)


def _block_diag_rhs(y, bd):
    yb = y.astype(BF16)
    per_tile = LANE_TILE // HEAD
    zero = jnp.zeros((y.shape[0], LANE_TILE), BF16)
    rows = []
    for h in range(HEADS_PER_GROUP):
        t = h // per_tile
        tiles = [zero] * (MXU_DIM // LANE_TILE)
        tiles[t] = yb[:, t * LANE_TILE:(t + 1) * LANE_TILE] * bd[h % per_tile]
        rows.append(jnp.concatenate(tiles, axis=1))
    return jnp.concatenate(rows, axis=0)


def _mmc(x, y_bd):
    return _dot(x.astype(BF16), y_bd)


def _scan_masks(rev):
    c = CHUNK
    ti = lax.broadcasted_iota(jnp.int32, (c, c), 0)
    si = lax.broadcasted_iota(jnp.int32, (c, c), 1)
    tri = jnp.where((si >= ti) if rev else (si <= ti), 1.0, 0.0).astype(BF16)
    tc = lax.broadcasted_iota(jnp.int32, (c, MXU_DIM), 0)
    sc = lax.broadcasted_iota(jnp.int32, (c, MXU_DIM), 1) % c
    before = (sc > tc) if rev else (sc < tc)
    upto = (sc >= tc) if rev else (sc <= tc)
    return tri, before, upto


def _scan_chunk_local(out, refs, sl, dir_masks, blk_masks, ka, last_row, bd):
    r_ref, k_ref, v_ref, kk_ref, lw_ref, a_ref = refs
    tri, before, upto = dir_masks
    eye, same16, off32, off64 = blk_masks
    c = CHUNK
    kk = kk_ref[0, sl, :].astype(F32)
    lw = lw_ref[0, sl, :]
    a = a_ref[0, sl, :]
    lw0, lw1, lw2 = _split3(lw)
    cum = _dot(tri, lw0) + (_dot(tri, lw1) + _dot(tri, lw2))
    yield
    e_in = jnp.exp(cum)
    e_inv = jnp.exp(-cum)
    al = kk * jnp.exp(cum - lw)
    rh = r_ref[0, sl, :].astype(F32) * e_in
    be = (kk * a) * e_inv
    kap = (k_ref[0, sl, :].astype(F32) * (1.0 + (a - 1.0) * ka)) * e_inv
    ptot = e_in[last_row:last_row + 1, :]
    ar = jnp.concatenate([al, rh], axis=0).astype(BF16)
    ab = _dot(ar, _block_diag_rhs(be, bd), _NT)
    ak = _dot(ar, _block_diag_rhs(kap, bd), _NT)
    yield
    lm = jnp.where(before, ab[:c], 0.0)
    aak = jnp.where(before, ak[:c], 0.0)
    kq = jnp.where(upto, ak[c:], 0.0)
    out["bq"] = jnp.where(upto, ab[c:], 0.0).astype(BF16)
    ld = jnp.where(same16, lm, 0.0)
    p = eye - ld
    ld_bd = _block_diag_rhs(ld, bd)
    l2 = _mmc(ld, ld_bd)
    yield
    pl2 = _mmc(jnp.concatenate([p, l2], axis=0), _block_diag_rhs(l2, bd))
    yield
    p = p + pl2[:c]
    l4 = pl2[c:]
    pl4 = _mmc(jnp.concatenate([p, l4], axis=0), _block_diag_rhs(l4, bd))
    yield
    p = p + pl4[:c]
    p8 = _mmc(p, _block_diag_rhs(pl4[c:], bd))
    v = v_ref[0, sl, :]
    akv = _mmc(jnp.concatenate([aak, kq], axis=0), _block_diag_rhs(v, bd))
    aak_v = akv[:c]
    out["kq_v"] = akv[c:]
    yield
    p = p + p8
    for off in (off32, off64):
        lo = jnp.where(off, lm, 0.0)
        t1 = _mmc(lo, _block_diag_rhs(p, bd))
        yield
        t2 = _mmc(p, _block_diag_rhs(t1, bd))
        yield
        p = p - t2
    wt = _mmc(p, _block_diag_rhs(al, bd))
    out["ut"] = _mmc(p, _block_diag_rhs(aak_v, bd))
    yield
    out["wr"] = jnp.concatenate([wt, rh], axis=0).astype(BF16)
    out["v"] = v.astype(BF16)
    out["bkT"] = jnp.transpose(jnp.concatenate([be, kap], axis=0)).astype(BF16)
    out["ptot"] = ptot


def _scan_chain(items, m_scr, y_ref, bd, bd_mask):
    c = CHUNK
    for loc, sl in items:
        m = m_scr[...]
        zs = _dot(loc["wr"], m.astype(BF16))
        yield
        u = -(zs[:c] + loc["ut"])
        uv = jnp.concatenate([u.astype(BF16), loc["v"]], axis=0)
        dm = _dot(loc["bkT"], uv)
        yu = _mmc(loc["bq"], _block_diag_rhs(u, bd))
        yield
        y_ref[0, sl, :] = (zs[c:] + loc["kq_v"] + yu).astype(BF16)
        pc = jnp.transpose(jnp.broadcast_to(loc["ptot"], (8, MXU_DIM)))
        m_scr[...] = jnp.broadcast_to(pc[:, 0:1], (MXU_DIM, MXU_DIM)) * (m + jnp.where(bd_mask, dm, 0.0))


def _run_lockstep(gens):
    gens = list(gens)
    while gens:
        alive = []
        for g in gens:
            try:
                next(g)
                alive.append(g)
            except StopIteration:
                pass
        gens = alive


def _scan_kernel(rf_ref, kf_ref, vf_ref, kkf_ref, lwf_ref, af_ref, rb_ref, kb_ref, vb_ref, kkb_ref, lwb_ref,
                 ab_ref, ka_ref, s0_ref, yf_ref, yb_ref, sfin_ref, mf_scr, mb_scr,
                 *, n_sub, n_blk):
    c = CHUNK
    blk = pl.program_id(2)

    @pl.when(blk == 0)
    def _():
        zero = jnp.zeros((HEAD, HEAD), F32)
        for d, scr in enumerate((mf_scr, mb_scr)):
            for h in range(HEADS_PER_GROUP):
                blocks = [zero] * HEADS_PER_GROUP
                blocks[h] = jnp.transpose(s0_ref[0, d, h])
                scr[h * HEAD:(h + 1) * HEAD, :] = jnp.concatenate(blocks, axis=1)

    tc = lax.broadcasted_iota(jnp.int32, (c, MXU_DIM), 0)
    sc = lax.broadcasted_iota(jnp.int32, (c, MXU_DIM), 1) % c
    same16 = (sc // 16) == (tc // 16)
    same32 = (sc // 32) == (tc // 32)
    blk_masks = (jnp.where(sc == tc, 1.0, 0.0), same16,
                 jnp.logical_and(same32, jnp.logical_not(same16)), jnp.logical_not(same32))
    bd_mask = (lax.broadcasted_iota(jnp.int32, (MXU_DIM, MXU_DIM), 0) // HEAD) == \
              (lax.broadcasted_iota(jnp.int32, (MXU_DIM, MXU_DIM), 1) // HEAD)
    lane_head = lax.broadcasted_iota(jnp.int32, (c, LANE_TILE), 1) // HEAD
    bd = tuple(jnp.where(lane_head == q, 1.0, 0.0).astype(BF16) for q in range(LANE_TILE // HEAD))
    ka = ka_ref[...]

    refs_f = (rf_ref, kf_ref, vf_ref, kkf_ref, lwf_ref, af_ref)
    refs_b = (rb_ref, kb_ref, vb_ref, kkb_ref, lwb_ref, ab_ref)
    masks_f = _scan_masks(False)
    masks_b = _scan_masks(True)
    slices = [slice(j * c, (j + 1) * c) for j in range(n_sub)]
    items_f = [({}, sl) for sl in slices]
    items_b = [({}, sl) for sl in reversed(slices)]
    chains = []
    for w0 in range(0, n_sub, SCAN_WAVE):
        wave_f = items_f[w0:w0 + SCAN_WAVE]
        wave_b = items_b[w0:w0 + SCAN_WAVE]
        gens = []
        for (of, sf), (ob, sb) in zip(wave_f, wave_b):
            gens.append(_scan_chunk_local(of, refs_f, sf, masks_f, blk_masks, ka, c - 1, bd))
            gens.append(_scan_chunk_local(ob, refs_b, sb, masks_b, blk_masks, ka, 0, bd))
        _run_lockstep(gens + chains)
        chains = [_scan_chain(wave_f, mf_scr, yf_ref, bd, bd_mask),
                  _scan_chain(wave_b, mb_scr, yb_ref, bd, bd_mask)]
    _run_lockstep(chains)

    @pl.when(blk == n_blk - 1)
    def _():
        for d, scr in enumerate((mf_scr, mb_scr)):
            for h in range(HEADS_PER_GROUP):
                rows = scr[h * HEAD:(h + 1) * HEAD, :]
                sfin_ref[0, d, h] = jnp.transpose(rows[:, h * HEAD:(h + 1) * HEAD])


def _scan(r, k, v, kk, lwf, af, lwb, ab, k_a, s0):
    b, t, d_rwkv = r.shape
    groups = d_rwkv // MXU_DIM
    tb = min(SCAN_BLOCK, t)
    n_blk = t // tb
    n_sub = tb // CHUNK
    tok_f = pl.BlockSpec((1, tb, MXU_DIM), lambda i, g, j: (i, j, g))
    tok_b = pl.BlockSpec((1, tb, MXU_DIM), lambda i, g, j: (i, n_blk - 1 - j, g))
    st = pl.BlockSpec((1, 2, HEADS_PER_GROUP, HEAD, HEAD), lambda i, g, j: (i, 0, g, 0, 0))
    y_sd = jax.ShapeDtypeStruct((b, t, d_rwkv), BF16)
    return pl.pallas_call(
        functools.partial(_scan_kernel, n_sub=n_sub, n_blk=n_blk),
        grid=(b, groups, n_blk),
        in_specs=[tok_f] * 6 + [tok_b] * 6 + [pl.BlockSpec((1, MXU_DIM), lambda i, g, j: (0, g)), st],
        out_specs=[tok_f, tok_b, st],
        out_shape=[y_sd, y_sd, jax.ShapeDtypeStruct(s0.shape, F32)],
        scratch_shapes=[pltpu.VMEM((MXU_DIM, MXU_DIM), F32), pltpu.VMEM((MXU_DIM, MXU_DIM), F32)],
        compiler_params=_cparams(("parallel", "parallel", "arbitrary")),
        name="scan",
    )(r, k, v, kk, lwf, af, r, k, v, kk, lwb, ab, k_a.reshape(1, d_rwkv), s0)


def _post_kernel(yf_ref, yb_ref, g_ref, bg_ref, lg_ref, lb_ref, ones_ref, o_ref):
    ones_bd = ones_ref[...]
    inv_n = 1.0 / HEAD
    for q in range(o_ref.shape[2] // MXU_DIM):
        sl = slice(q * MXU_DIM, (q + 1) * MXU_DIM)
        y = yf_ref[0, :, sl].astype(F32) + yb_ref[0, :, sl].astype(F32)
        mu = _head_sum(y, ones_bd) * inv_n
        d = y - mu
        var = _head_sum(d * d, ones_bd) * inv_n
        yn = d * lax.rsqrt(var + GN_EPS) * lg_ref[:, sl] + lb_ref[:, sl]
        o_ref[0, :, sl] = (yn * g_ref[0, :, sl].astype(F32) + bg_ref[0, :, sl].astype(F32)).astype(BF16)


def _post(yf, yb, g, bg, lnx_g, lnx_b, ones_bd):
    b, t, d_rwkv = yf.shape
    tp = min(512, t)
    tok = pl.BlockSpec((1, tp, d_rwkv), lambda i, j: (i, j, 0))
    par = pl.BlockSpec((1, d_rwkv), lambda i, j: (0, 0))
    return pl.pallas_call(
        _post_kernel,
        grid=(b, t // tp),
        in_specs=[tok] * 4 + [par] * 2 + [pl.BlockSpec((MXU_DIM, MXU_DIM), lambda i, j: (0, 0))],
        out_specs=tok,
        out_shape=jax.ShapeDtypeStruct((b, t, d_rwkv), BF16),
        compiler_params=_cparams(("parallel", "arbitrary")),
        name="rwkv_post",
    )(yf, yb, g, bg, lnx_g.reshape(1, d_rwkv), lnx_b.reshape(1, d_rwkv), ones_bd)


def _window_bounds(n, w):
    t = np.arange(n)
    lo = np.clip(t - w // 2, 0, n)
    hi = np.clip(t - w // 2 + w, 0, n)
    return lo, hi


def _window_matrix(n, w):
    lo, hi = _window_bounds(n, w)
    s = np.arange(n)[None, :]
    m = ((s >= lo[:, None]) & (s < hi[:, None])).astype(np.float32)
    return m / (hi - lo)[:, None].astype(np.float32)


def _pool_ctx_kernel(z_ref, band_ref, pw_ref, ps_ref, o_ref):
    z = z_ref[0]
    m = _dot(band_ref[0], z, precision=lax.Precision.HIGHEST)
    d = (m - z).astype(BF16)
    o_ref[0] = (_dot(d, pw_ref[0]) * ps_ref[...]).astype(BF16)


def _pool_ctx(proj3, col0, pool_w_bf16, pool_scale):
    b, t, _ = proj3.shape
    ng, pg, _ = pool_w_bf16.shape
    band = jnp.asarray(np.stack([_window_matrix(t, w) for w in POOL_WINDOWS]))
    cb = col0 // pg
    return pl.pallas_call(
        _pool_ctx_kernel,
        grid=(b, ng),
        in_specs=[pl.BlockSpec((1, t, pg), lambda i, q: (i, 0, cb + q)),
                  pl.BlockSpec((1, t, t), lambda i, q: (q, 0, 0)),
                  pl.BlockSpec((1, pg, pg), lambda i, q: (q, 0, 0)),
                  pl.BlockSpec((1, pg), lambda i, q: (0, q))],
        out_specs=pl.BlockSpec((1, t, pg), lambda i, q: (i, 0, q)),
        out_shape=jax.ShapeDtypeStruct((b, t, ng * pg), BF16),
        compiler_params=_cparams(("parallel", "arbitrary")),
        name="pool_ctx",
    )(proj3, band, pool_w_bf16, pool_scale.reshape(1, ng * pg))


def _pool_lat_kernel(z_ref, band_ref, pw_ref, ps_ref, o_ref, m1_ref, cs_ref, *, n_rows):
    tile = band_ref.shape[1]
    n_tok = n_rows * GRID_W
    band = band_ref[0]
    for i in range(n_tok // tile):
        sl = slice(i * tile, (i + 1) * tile)
        m1_ref[sl, :] = _dot(band, z_ref[0, sl, :], precision=lax.Precision.HIGHEST)
    cs_ref[0:GRID_W, :] = jnp.zeros((GRID_W, cs_ref.shape[1]), F32)
    for rr in range(n_rows):
        cs_ref[(rr + 1) * GRID_W:(rr + 2) * GRID_W, :] = (
            cs_ref[rr * GRID_W:(rr + 1) * GRID_W, :] + m1_ref[rr * GRID_W:(rr + 1) * GRID_W, :])
    group = pl.program_id(1)
    for gi, w in enumerate(POOL_WINDOWS):
        @pl.when(group == gi)
        def _(w=w):
            lo, hi = _window_bounds(n_rows, w)
            for rr in range(n_rows):
                l, h = int(lo[rr]), int(hi[rr])
                m2 = (cs_ref[h * GRID_W:(h + 1) * GRID_W, :]
                      - cs_ref[l * GRID_W:(l + 1) * GRID_W, :]) / float(h - l)
                sl = slice(rr * GRID_W, (rr + 1) * GRID_W)
                m1_ref[sl, :] = m2 - z_ref[0, sl, :]
    pw = pw_ref[0]
    ps = ps_ref[...]
    for i in range(n_tok // tile):
        sl = slice(i * tile, (i + 1) * tile)
        o_ref[0, sl, :] = (_dot(m1_ref[sl, :].astype(BF16), pw) * ps).astype(BF16)


def _pool_lat(proj3, col0, pool_w_bf16, pool_scale):
    b, t, _ = proj3.shape
    ng, pg, _ = pool_w_bf16.shape
    n_rows = t // GRID_W
    tile = MXU_DIM
    band = jnp.asarray(np.stack(
        [np.kron(np.eye(tile // GRID_W, dtype=np.float32), _window_matrix(GRID_W, w)) for w in POOL_WINDOWS]))
    cb = col0 // pg
    return pl.pallas_call(
        functools.partial(_pool_lat_kernel, n_rows=n_rows),
        grid=(b, ng),
        in_specs=[pl.BlockSpec((1, t, pg), lambda i, q: (i, 0, cb + q)),
                  pl.BlockSpec((1, tile, tile), lambda i, q: (q, 0, 0)),
                  pl.BlockSpec((1, pg, pg), lambda i, q: (q, 0, 0)),
                  pl.BlockSpec((1, pg), lambda i, q: (0, q))],
        out_specs=pl.BlockSpec((1, t, pg), lambda i, q: (i, 0, q)),
        out_shape=jax.ShapeDtypeStruct((b, t, ng * pg), BF16),
        scratch_shapes=[pltpu.VMEM((t, pg), F32), pltpu.VMEM((t + GRID_W, pg), F32)],
        compiler_params=_cparams(("parallel", "arbitrary")),
        name="pool_lat",
    )(proj3, band, pool_w_bf16, pool_scale.reshape(1, ng * pg))


ROW_SPLIT = 2


def _resident(shape):
    nd = len(shape)
    return pl.BlockSpec(shape, lambda i: (0,) * nd, pipeline_mode=pl.Buffered(1))


def _up_out_kernel(ya_ref, yb_ref, ga_ref, gb_ref, x_ref, mod_ref, g_ref, wa_ref, wb_ref, wo_ref,
                   x1_ref, h2_ref):
    rows = x_ref.shape[0] // ROW_SPLIT
    for part in range(ROW_SPLIT):
        sl = slice(part * rows, (part + 1) * rows)
        ua = _dot(ya_ref[sl, :], wa_ref[...])
        ub = _dot(yb_ref[sl, :], wb_ref[...])
        merged = (_sigmoid(ga_ref[sl, :]) * ua + _sigmoid(gb_ref[sl, :]) * ub).astype(BF16)
        x1 = x_ref[sl, :] + mod_ref[0, 2:3, :] * _dot(merged, wo_ref[...])
        x1_ref[sl, :] = x1
        y = x1 * lax.rsqrt(jnp.mean(x1 * x1, axis=-1, keepdims=True) + NORM_EPS)
        h2 = (y * g_ref[...]) * (1.0 + mod_ref[0, 4:5, :]) + mod_ref[0, 3:4, :]
        h2_ref[sl, :] = h2.astype(BF16)


def _up_out(ya2, yb2, gates, wa, wb, w_out, x2, mod3, tiles_per_mod_tokens, norm2_g, tm=256):
    m, d = x2.shape
    tm = min(tm, m)
    ka, kb = ya2.shape[1], yb2.shape[1]
    tpm = tiles_per_mod_tokens // tm
    row = lambda width, col: pl.BlockSpec((tm, width), lambda i: (i, col))
    return pl.pallas_call(
        _up_out_kernel,
        grid=(m // tm,),
        in_specs=[row(ka, 0), row(kb, 0), row(d, 0), row(d, 1), row(d, 0),
                  pl.BlockSpec((1, 6, d), lambda i: (i // tpm, 0, 0)),
                  _resident((1, d)), _resident((ka, d)), _resident((kb, d)), _resident((d, d))],
        out_specs=[row(d, 0), row(d, 0)],
        out_shape=[jax.ShapeDtypeStruct((m, d), F32), jax.ShapeDtypeStruct((m, d), BF16)],
        compiler_params=_cparams(("parallel",)),
        name="up_out",
    )(ya2, yb2, gates, gates, x2, mod3, norm2_g.reshape(1, d), wa, wb, w_out)


def _ffn1_kernel(h_ref, w1_ref, w3_ref, o_ref):
    h = h_ref[...]
    u1 = _dot(h, w1_ref[...])
    u3 = _dot(h, w3_ref[...])
    o_ref[...] = (u1 * _sigmoid(u1) * u3).astype(BF16)


def _ffn1(h2, w13, tm=1024, tn=PROJ_TN):
    m, d = h2.shape
    tm = min(tm, m)
    d_ff = w13.shape[1] // 2
    nj = d_ff // tn
    return pl.pallas_call(
        _ffn1_kernel,
        grid=(m // tm, nj),
        in_specs=[pl.BlockSpec((tm, d), lambda i, j: (i, 0)),
                  pl.BlockSpec((d, tn), lambda i, j: (0, j)),
                  pl.BlockSpec((d, tn), lambda i, j: (0, nj + j))],
        out_specs=pl.BlockSpec((tm, tn), lambda i, j: (i, j)),
        out_shape=jax.ShapeDtypeStruct((m, d_ff), BF16),
        compiler_params=_cparams(("parallel", "arbitrary")),
        name="ffn1",
    )(h2, w13, w13)


def _ffn2_kernel(a_ref, x_ref, mod_ref, g_ref, w_ref, o_ref):
    rows = x_ref.shape[0] // ROW_SPLIT
    for part in range(ROW_SPLIT):
        sl = slice(part * rows, (part + 1) * rows)
        x2 = x_ref[sl, :] + mod_ref[0, 5:6, :] * _dot(a_ref[sl, :], w_ref[...])
        y = x2 * lax.rsqrt(jnp.mean(x2 * x2, axis=-1, keepdims=True) + NORM_EPS)
        o_ref[sl, :] = y * g_ref[...]


def _ffn2(act, w2, x1, mod3, tiles_per_mod_tokens, final_g, tm=256):
    m, d = x1.shape
    tm = min(tm, m)
    d_ff = act.shape[1]
    tpm = tiles_per_mod_tokens // tm
    return pl.pallas_call(
        _ffn2_kernel,
        grid=(m // tm,),
        in_specs=[pl.BlockSpec((tm, d_ff), lambda i: (i, 0)),
                  pl.BlockSpec((tm, d), lambda i: (i, 0)),
                  pl.BlockSpec((1, 6, d), lambda i: (i // tpm, 0, 0)),
                  _resident((1, d)), _resident((d_ff, d))],
        out_specs=pl.BlockSpec((tm, d), lambda i: (i, 0)),
        out_shape=jax.ShapeDtypeStruct((m, d), F32),
        compiler_params=_cparams(("parallel",)),
        name="ffn2",
    )(act, x1, mod3, final_g.reshape(1, d), w2)


def _trunk(x, mod, s0, latent, p):
    b, t, d = x.shape
    m = b * t
    d_rwkv = p["d_rwkv"]
    d_pool = d - d_rwkv
    heads = d_rwkv // HEAD
    x2 = x.reshape(m, d)
    nb = mod.shape[0]
    mod3 = mod.reshape(nb, 6, d)
    tokens_per_mod = t if nb == b else m

    proj, gates = _in_proj(x2, p["norm1_g"], mod3, tokens_per_mod, p["w_in"], p["rw_cols_p"] + d_pool)
    proj3 = proj.reshape(b, t, proj.shape[1])

    r, k, v, kk, lwf, lwb, af, ab, g, bg = _prep(
        proj3, p["mu"], p["w0"], p["w2"], p["a0"], p["a2"], p["g2"], p["k_k"], p["k_a"], p["r_k"],
        p["ones_bd"], d_rwkv, p["rw_cols_p"])

    if s0 is None:
        s0 = jnp.zeros((b, 2, heads, HEAD, HEAD), F32)
    yf, yb, s_fin = _scan(r, k, v, kk, lwf, af, lwb, ab, p["k_a"], s0)
    ya = _post(yf, yb, g, bg, p["lnx_g"], p["lnx_b"], p["ones_bd"])

    pool_col0 = p["rw_cols_p"]
    if latent:
        yb_pool = _pool_lat(proj3, pool_col0, p["pool_w"], p["pool_scale"])
    else:
        yb_pool = _pool_ctx(proj3, pool_col0, p["pool_w"], p["pool_scale"])

    x1, h2 = _up_out(ya.reshape(m, d_rwkv), yb_pool.reshape(m, d_pool), gates, p["w_up_a"], p["w_up_b"],
                     p["w_out"], x2, mod3, tokens_per_mod, p["norm2_g"])
    act = _ffn1(h2, p["ffn_w13"])
    y = _ffn2(act, p["ffn_w2"], x1, mod3, tokens_per_mod, p["final_g"])
    return y.reshape(b, t, d), s_fin


def kernel(x_prompt, x_sample, c, state_rwkv, c_ctx, w_mod, b_mod, norm1_g, w_in, shift_mu, w0, w2, a0, a2, g2,
           k_k, k_a, r_k, lnx_g, lnx_b, w_up_a, pool_w, pool_scale, w_up_b, w_out, norm2_g, ffn_w13, ffn_w2,
           final_g):
    depth = w_in.shape[0]
    assert depth == 1, "final norm is fused into the last layer: single-layer trunk only"
    d = x_prompt.shape[-1]
    d_rwkv = w_up_a.shape[1]
    rw_cols = shift_mu.shape[1]
    rw_cols_p = -(-rw_cols // PROJ_TN) * PROJ_TN
    assert x_prompt.shape[1] == SEQ_TILE and x_sample.shape[1] % SEQ_TILE == 0

    l = 0
    w_in_l = w_in[l].astype(BF16)
    w_in_p = jnp.concatenate(
        [w_in_l[:, :rw_cols], jnp.zeros((d, rw_cols_p - rw_cols), BF16), w_in_l[:, rw_cols:]], axis=1)
    ones_bd = jnp.asarray(np.kron(np.eye(HEADS_PER_GROUP, dtype=np.float32),
                                  np.ones((HEAD, HEAD), np.float32))).astype(BF16)
    p = {
        "d_rwkv": d_rwkv, "rw_cols_p": rw_cols_p,
        "norm1_g": norm1_g[l], "w_in": w_in_p,
        "mu": jnp.pad(shift_mu[l], (0, rw_cols_p - rw_cols)).reshape(1, rw_cols_p),
        "w0": w0[l], "w2": w2[l], "a0": a0[l], "a2": a2[l], "g2": g2[l],
        "k_k": k_k[l], "k_a": k_a[l], "r_k": r_k[l].reshape(-1), "lnx_g": lnx_g[l], "lnx_b": lnx_b[l],
        "ones_bd": ones_bd,
        "w_up_a": w_up_a[l].astype(BF16), "w_up_b": w_up_b[l].astype(BF16),
        "pool_w": pool_w[l].astype(BF16), "pool_scale": pool_scale[l],
        "w_out": w_out[l].astype(BF16), "norm2_g": norm2_g[l],
        "ffn_w13": ffn_w13[l].astype(BF16), "ffn_w2": ffn_w2[l].astype(BF16), "final_g": final_g,
    }
    cond = jnp.concatenate([c_ctx[None, :], c], axis=0)
    mod = _mod(cond, w_mod[l], b_mod[l])
    y_prompt, s_ctx = _trunk(x_prompt, mod[:1], None, False, p)
    y_sample, _ = _trunk(x_sample, mod[1:], state_rwkv[:, l], True, p)
    new_state = s_ctx[:, None].astype(x_prompt.dtype)
    return (y_prompt, y_sample, new_state)
```

```python
import functools

import numpy as np
import jax
import jax.numpy as jnp
from jax import lax
from jax.experimental import pallas as pl
from jax.experimental.pallas import tpu as pltpu

F32 = jnp.float32
BF16 = jnp.bfloat16

HEAD = 64
POOL_WINDOWS = (2, 4, 8, 16)
GRID_W = 64
DECAY_LORA = 64
AAA_LORA = 64
GATE_LORA = 128
NORM_EPS = 1e-6
GN_EPS = 64e-5

LANE_TILE = 128
BF16_ROWS = 16
MXU_DIM = 256
HEADS_PER_GROUP = MXU_DIM // HEAD
VMEM_LIMIT = 56 * 1024 * 1024

CHUNK = 64
SEQ_TILE = 256
PREP_TILE = 512
POST_TILE = 512
MOD_TN = 1024
PROJ_TN = 512
SCAN_BLOCK = 2048
SCAN_WAVE = 4


def _cparams(sem):
    return pltpu.CompilerParams(dimension_semantics=sem, vmem_limit_bytes=VMEM_LIMIT)


def _dot(a, b, dims=(((1,), (0,)), ((), ()))):
    return lax.dot_general(a, b, dims, preferred_element_type=F32)


_NN = (((1,), (0,)), ((), ()))
_NT = (((1,), (1,)), ((), ()))


def _split2(x):
    hi = x.astype(BF16)
    lo = (x - hi.astype(F32)).astype(BF16)
    return hi, lo


def _split3(x):
    hi = x.astype(BF16)
    r1 = x - hi.astype(F32)
    mid = r1.astype(BF16)
    lo = (r1 - mid.astype(F32)).astype(BF16)
    return hi, mid, lo


def _mm(a, b, passes):
    if passes == 1:
        return _dot(a.astype(BF16), b.astype(BF16))
    assert passes == 3
    ah, al = _split2(a)
    bh, bl = _split2(b)
    return _dot(ah, bh) + (_dot(ah, bl) + _dot(al, bh))


def _mm_exact_rhs(a, b_bf16):
    a0, a1 = _split2(a)
    return _dot(a0, b_bf16) + _dot(a1, b_bf16)


def _sigmoid(x):
    return 1.0 / (1.0 + jnp.exp(-x))


def _mod_kernel(c_ref, w_ref, b_ref, o_ref):
    c = c_ref[...]
    s = (c * _sigmoid(c)).astype(BF16)
    o_ref[...] = _dot(s, w_ref[...].astype(BF16)) + b_ref[...]


def _mod(cond, w_mod, b_mod):
    nb, d = cond.shape
    rows = -(-nb // BF16_ROWS) * BF16_ROWS
    cond_p = jnp.pad(cond, ((0, rows - nb), (0, 0)))
    n = w_mod.shape[1]
    tn = MOD_TN
    out = pl.pallas_call(
        _mod_kernel,
        grid=(n // tn,),
        in_specs=[pl.BlockSpec((rows, d), lambda j: (0, 0)),
                  pl.BlockSpec((d, tn), lambda j: (0, j)),
                  pl.BlockSpec((1, tn), lambda j: (0, j))],
        out_specs=pl.BlockSpec((rows, tn), lambda j: (0, j)),
        out_shape=jax.ShapeDtypeStruct((rows, n), F32),
        compiler_params=_cparams(("arbitrary",)),
        name="mod",
    )(cond_p, w_mod, b_mod.reshape(1, n))
    return out[:nb]


NORM_ROWS = 16


def _in_kernel(x_ref, g_ref, mod_ref, w_ref, om_ref, og_ref, h_ref, *, nj_main):
    j = pl.program_id(1)

    @pl.when(j == 0)
    def _():
        gain = g_ref[...]
        scale = 1.0 + mod_ref[0, 1:2, :]
        shift = mod_ref[0, 0:1, :]

        def norm_rows(i, carry):
            rows = pl.ds(pl.multiple_of(i * NORM_ROWS, NORM_ROWS), NORM_ROWS)
            x = x_ref[rows, :]
            y = x * lax.rsqrt(jnp.mean(x * x, axis=-1, keepdims=True) + NORM_EPS)
            h_ref[rows, :] = ((y * gain) * scale + shift).astype(BF16)
            return carry

        lax.fori_loop(0, x_ref.shape[0] // NORM_ROWS, norm_rows, 0, unroll=8)

    @pl.when(j < nj_main)
    def _():
        om_ref[...] = _dot(h_ref[...], w_ref[...])

    @pl.when(j >= nj_main)
    def _():
        og_ref[...] = _dot(h_ref[...], w_ref[...])


def _in_proj(x2, norm_g, mod3, tiles_per_mod_tokens, w_bf16, n_main, tm=1024, tn=PROJ_TN):
    m, d = x2.shape
    tm = min(tm, m)
    n = w_bf16.shape[1]
    tpm = tiles_per_mod_tokens // tm
    nj_main = n_main // tn
    return pl.pallas_call(
        functools.partial(_in_kernel, nj_main=nj_main),
        grid=(m // tm, n // tn),
        in_specs=[pl.BlockSpec((tm, d), lambda i, j: (i, 0)),
                  pl.BlockSpec((1, d), lambda i, j: (0, 0)),
                  pl.BlockSpec((1, 6, d), lambda i, j: (i // tpm, 0, 0)),
                  pl.BlockSpec((d, tn), lambda i, j: (0, j))],
        out_specs=[pl.BlockSpec((tm, tn), lambda i, j: (i, jnp.minimum(j, nj_main - 1))),
                   pl.BlockSpec((tm, tn), lambda i, j: (i, jnp.maximum(j - nj_main, 0)))],
        out_shape=[jax.ShapeDtypeStruct((m, n_main), F32), jax.ShapeDtypeStruct((m, n - n_main), F32)],
        scratch_shapes=[pltpu.VMEM((tm, d), BF16)],
        compiler_params=_cparams(("parallel", "arbitrary")),
        name="in_proj",
    )(x2, norm_g.reshape(1, d), mod3, w_bf16)


P_LORA_DECAY = 3
P_LORA = 1
DECAY_GAIN = float(np.exp(-0.5))


def _head_sum(x, ones_bd):
    return _mm_exact_rhs(x, ones_bd)


def _prep_kernel(main_ref, prev_ref, next_ref, mu_ref, w0_ref, w2_ref, a0_ref, a2_ref, g2_ref,
                 kkw_ref, ka_ref, rk_ref, ones_ref, r_o, k_o, v_o, kk_o, lwf_o, lwb_o, af_o, ab_o, g_o, bg_o,
                 *, n_tiles, d_rwkv):
    t = pl.program_id(1)
    rows = main_ref.shape[1]
    row8 = lax.broadcasted_iota(jnp.int32, (8, 1), 0)
    has_prev = t > 0
    has_next = t < n_tiles - 1

    def shifted(c0, c1):
        z = main_ref[0, :, c0:c1]
        p = jnp.where(has_prev, prev_ref[0, 7:8, c0:c1], 0.0)
        n = jnp.where(has_next, next_ref[0, 0:1, c0:c1], 0.0)
        zp = pltpu.roll(z, 1, 0)
        zn = pltpu.roll(z, rows - 1, 0)
        zp = jnp.concatenate([jnp.where(row8 == 0, p, zp[:8]), zp[8:]], axis=0)
        zn = jnp.concatenate([zn[:rows - 8], jnp.where(row8 == 7, n, zn[rows - 8:])], axis=0)
        mu = mu_ref[:, c0:c1]
        return (1.0 - mu) * z + (0.5 * mu) * (zp + zn)

    dr = d_rwkv
    r = shifted(0, dr)
    r_o[0] = r.astype(BF16)
    k = shifted(dr, 2 * dr)
    k_o[0] = k.astype(BF16)
    v = shifted(2 * dr, 3 * dr)
    v_o[0] = v.astype(BF16)

    kkv = k * kkw_ref[...]
    ones_bd = ones_ref[...]
    for j in range(dr // MXU_DIM):
        sl = slice(j * MXU_DIM, (j + 1) * MXU_DIM)
        x = kkv[:, sl]
        ss = _head_sum(x * x, ones_bd)
        kk_o[0, :, sl] = (x * lax.rsqrt(jnp.maximum(ss, 1e-24))).astype(BF16)

    c = 3 * dr
    sm = shifted(c, c + 2 * DECAY_LORA + 2 * AAA_LORA + GATE_LORA)
    o = 0
    for d, lw_o in enumerate((lwf_o, lwb_o)):
        wd = jnp.tanh(sm[:, o:o + DECAY_LORA])
        o += DECAY_LORA
        xw = w0_ref[d:d + 1, :] + _mm(wd, w2_ref[d], passes=P_LORA_DECAY)
        lw_o[0] = -DECAY_GAIN * _sigmoid(xw)
    for d, a_o in enumerate((af_o, ab_o)):
        ad = sm[:, o:o + AAA_LORA]
        o += AAA_LORA
        a_o[0] = _sigmoid(a0_ref[d:d + 1, :] + _mm(ad, a2_ref[d], passes=P_LORA))
    gd = _sigmoid(sm[:, o:o + GATE_LORA])
    g = _mm(gd, g2_ref[...], passes=P_LORA)
    g_o[0] = g.astype(BF16)

    for j in range(dr // MXU_DIM):
        sl = slice(j * MXU_DIM, (j + 1) * MXU_DIM)
        ka = ka_ref[:, sl]
        kj = k[:, sl]
        kd_sum = kj * (1.0 + (af_o[0, :, sl] - 1.0) * ka) + kj * (1.0 + (ab_o[0, :, sl] - 1.0) * ka)
        bonus = _head_sum(r[:, sl] * kd_sum * rk_ref[:, sl], ones_bd) * v[:, sl]
        bg_o[0, :, sl] = (bonus * g[:, sl]).astype(BF16)


def _prep(proj3, mu_p, w0, w2, a0, a2, g2, k_k, k_a, r_k, ones_bd, d_rwkv, rw_cols_p):
    b, t, _ = proj3.shape
    tp = min(PREP_TILE, t)
    n_tiles = t // tp
    sub = tp // 8
    last8 = t // 8 - 1
    out_sd = jax.ShapeDtypeStruct((b, t, d_rwkv), F32)
    half_sd = jax.ShapeDtypeStruct((b, t, d_rwkv), BF16)
    o_spec = pl.BlockSpec((1, tp, d_rwkv), lambda i, j: (i, j, 0))

    def full(a):
        nd = a.ndim
        return pl.BlockSpec(a.shape, lambda i, j: (0,) * nd)

    params = (mu_p, w0, w2, a0, a2, g2, k_k.reshape(1, d_rwkv), k_a.reshape(1, d_rwkv),
              r_k.reshape(1, d_rwkv), ones_bd)
    return pl.pallas_call(
        functools.partial(_prep_kernel, n_tiles=n_tiles, d_rwkv=d_rwkv),
        grid=(b, n_tiles),
        in_specs=[pl.BlockSpec((1, tp, rw_cols_p), lambda i, j: (i, j, 0)),
                  pl.BlockSpec((1, 8, rw_cols_p), lambda i, j: (i, jnp.maximum(j * sub - 1, 0), 0)),
                  pl.BlockSpec((1, 8, rw_cols_p), lambda i, j: (i, jnp.minimum((j + 1) * sub, last8), 0)),
                  ] + [full(a) for a in params],
        out_specs=[o_spec] * 10,
        out_shape=[half_sd] * 4 + [out_sd] * 4 + [half_sd] * 2,
        compiler_params=_cparams(("parallel", "arbitrary")),
        name="rwkv_prep",
    )(proj3, proj3, proj3, *params)


def _block_diag_rhs(y, bd):
    yb = y.astype(BF16)
    per_tile = LANE_TILE // HEAD
    zero = jnp.zeros((y.shape[0], LANE_TILE), BF16)
    rows = []
    for h in range(HEADS_PER_GROUP):
        t = h // per_tile
        tiles = [zero] * (MXU_DIM // LANE_TILE)
        tiles[t] = yb[:, t * LANE_TILE:(t + 1) * LANE_TILE] * bd[h % per_tile]
        rows.append(jnp.concatenate(tiles, axis=1))
    return jnp.concatenate(rows, axis=0)


def _mmc(x, y_bd):
    return _dot(x.astype(BF16), y_bd)


def _scan_masks(rev):
    c = CHUNK
    ti = lax.broadcasted_iota(jnp.int32, (c, c), 0)
    si = lax.broadcasted_iota(jnp.int32, (c, c), 1)
    tri = jnp.where((si >= ti) if rev else (si <= ti), 1.0, 0.0).astype(BF16)
    tc = lax.broadcasted_iota(jnp.int32, (c, MXU_DIM), 0)
    sc = lax.broadcasted_iota(jnp.int32, (c, MXU_DIM), 1) % c
    before = (sc > tc) if rev else (sc < tc)
    upto = (sc >= tc) if rev else (sc <= tc)
    return tri, before, upto


def _scan_chunk_local(out, refs, sl, dir_masks, blk_masks, ka, last_row, bd):
    r_ref, k_ref, v_ref, kk_ref, lw_ref, a_ref = refs
    tri, before, upto = dir_masks
    eye, same16, off32, off64 = blk_masks
    c = CHUNK
    kk = kk_ref[0, sl, :].astype(F32)
    lw = lw_ref[0, sl, :]
    a = a_ref[0, sl, :]
    lw0, lw1, lw2 = _split3(lw)
    cum = _dot(tri, lw0) + (_dot(tri, lw1) + _dot(tri, lw2))
    yield
    e_in = jnp.exp(cum)
    e_inv = jnp.exp(-cum)
    al = kk * jnp.exp(cum - lw)
    rh = r_ref[0, sl, :].astype(F32) * e_in
    be = (kk * a) * e_inv
    kap = (k_ref[0, sl, :].astype(F32) * (1.0 + (a - 1.0) * ka)) * e_inv
    ptot = e_in[last_row:last_row + 1, :]
    ar = jnp.concatenate([al, rh], axis=0).astype(BF16)
    ab = _dot(ar, _block_diag_rhs(be, bd), _NT)
    ak = _dot(ar, _block_diag_rhs(kap, bd), _NT)
    yield
    lm = jnp.where(before, ab[:c], 0.0)
    aak = jnp.where(before, ak[:c], 0.0)
    kq = jnp.where(upto, ak[c:], 0.0)
    out["bq"] = jnp.where(upto, ab[c:], 0.0).astype(BF16)
    ld = jnp.where(same16, lm, 0.0)
    p = eye - ld
    ld_bd = _block_diag_rhs(ld, bd)
    l2 = _mmc(ld, ld_bd)
    yield
    pl2 = _mmc(jnp.concatenate([p, l2], axis=0), _block_diag_rhs(l2, bd))
    yield
    p = p + pl2[:c]
    l4 = pl2[c:]
    pl4 = _mmc(jnp.concatenate([p, l4], axis=0), _block_diag_rhs(l4, bd))
    yield
    p = p + pl4[:c]
    p8 = _mmc(p, _block_diag_rhs(pl4[c:], bd))
    v = v_ref[0, sl, :]
    akv = _mmc(jnp.concatenate([aak, kq], axis=0), _block_diag_rhs(v, bd))
    aak_v = akv[:c]
    out["kq_v"] = akv[c:]
    yield
    p = p + p8
    for off in (off32, off64):
        lo = jnp.where(off, lm, 0.0)
        t1 = _mmc(lo, _block_diag_rhs(p, bd))
        yield
        t2 = _mmc(p, _block_diag_rhs(t1, bd))
        yield
        p = p - t2
    wt = _mmc(p, _block_diag_rhs(al, bd))
    out["ut"] = _mmc(p, _block_diag_rhs(aak_v, bd))
    yield
    out["wr"] = jnp.concatenate([wt, rh], axis=0).astype(BF16)
    out["v"] = v.astype(BF16)
    out["bkT"] = jnp.transpose(jnp.concatenate([be, kap], axis=0)).astype(BF16)
    out["ptot"] = ptot


def _scan_chain(items, m_scr, y_ref, bd, bd_mask):
    c = CHUNK
    for loc, sl in items:
        m = m_scr[...]
        zs = _dot(loc["wr"], m.astype(BF16))
        yield
        u = -(zs[:c] + loc["ut"])
        uv = jnp.concatenate([u.astype(BF16), loc["v"]], axis=0)
        dm = _dot(loc["bkT"], uv)
        yu = _mmc(loc["bq"], _block_diag_rhs(u, bd))
        yield
        y_ref[0, sl, :] = (zs[c:] + loc["kq_v"] + yu).astype(BF16)
        pc = jnp.transpose(jnp.broadcast_to(loc["ptot"], (8, MXU_DIM)))
        m_scr[...] = jnp.broadcast_to(pc[:, 0:1], (MXU_DIM, MXU_DIM)) * (m + jnp.where(bd_mask, dm, 0.0))


def _run_lockstep(gens):
    gens = list(gens)
    while gens:
        alive = []
        for g in gens:
            try:
                next(g)
                alive.append(g)
            except StopIteration:
                pass
        gens = alive


def _scan_kernel(rf_ref, kf_ref, vf_ref, kkf_ref, lwf_ref, af_ref, rb_ref, kb_ref, vb_ref, kkb_ref, lwb_ref,
                 ab_ref, ka_ref, s0_ref, yf_ref, yb_ref, sfin_ref, mf_scr, mb_scr,
                 *, n_sub, n_blk):
    c = CHUNK
    blk = pl.program_id(2)

    @pl.when(blk == 0)
    def _():
        zero = jnp.zeros((HEAD, HEAD), F32)
        for d, scr in enumerate((mf_scr, mb_scr)):
            for h in range(HEADS_PER_GROUP):
                blocks = [zero] * HEADS_PER_GROUP
                blocks[h] = jnp.transpose(s0_ref[0, d, h])
                scr[h * HEAD:(h + 1) * HEAD, :] = jnp.concatenate(blocks, axis=1)

    tc = lax.broadcasted_iota(jnp.int32, (c, MXU_DIM), 0)
    sc = lax.broadcasted_iota(jnp.int32, (c, MXU_DIM), 1) % c
    same16 = (sc // 16) == (tc // 16)
    same32 = (sc // 32) == (tc // 32)
    blk_masks = (jnp.where(sc == tc, 1.0, 0.0), same16,
                 jnp.logical_and(same32, jnp.logical_not(same16)), jnp.logical_not(same32))
    bd_mask = (lax.broadcasted_iota(jnp.int32, (MXU_DIM, MXU_DIM), 0) // HEAD) == \
              (lax.broadcasted_iota(jnp.int32, (MXU_DIM, MXU_DIM), 1) // HEAD)
    lane_head = lax.broadcasted_iota(jnp.int32, (c, LANE_TILE), 1) // HEAD
    bd = tuple(jnp.where(lane_head == q, 1.0, 0.0).astype(BF16) for q in range(LANE_TILE // HEAD))
    ka = ka_ref[...]

    refs_f = (rf_ref, kf_ref, vf_ref, kkf_ref, lwf_ref, af_ref)
    refs_b = (rb_ref, kb_ref, vb_ref, kkb_ref, lwb_ref, ab_ref)
    masks_f = _scan_masks(False)
    masks_b = _scan_masks(True)
    slices = [slice(j * c, (j + 1) * c) for j in range(n_sub)]
    items_f = [({}, sl) for sl in slices]
    items_b = [({}, sl) for sl in reversed(slices)]
    chains = []
    for w0 in range(0, n_sub, SCAN_WAVE):
        wave_f = items_f[w0:w0 + SCAN_WAVE]
        wave_b = items_b[w0:w0 + SCAN_WAVE]
        gens = []
        for (of, sf), (ob, sb) in zip(wave_f, wave_b):
            gens.append(_scan_chunk_local(of, refs_f, sf, masks_f, blk_masks, ka, c - 1, bd))
            gens.append(_scan_chunk_local(ob, refs_b, sb, masks_b, blk_masks, ka, 0, bd))
        _run_lockstep(gens + chains)
        chains = [_scan_chain(wave_f, mf_scr, yf_ref, bd, bd_mask),
                  _scan_chain(wave_b, mb_scr, yb_ref, bd, bd_mask)]
    _run_lockstep(chains)

    @pl.when(blk == n_blk - 1)
    def _():
        for d, scr in enumerate((mf_scr, mb_scr)):
            for h in range(HEADS_PER_GROUP):
                rows = scr[h * HEAD:(h + 1) * HEAD, :]
                sfin_ref[0, d, h] = jnp.transpose(rows[:, h * HEAD:(h + 1) * HEAD])


def _scan(r, k, v, kk, lwf, af, lwb, ab, k_a, s0):
    b, t, d_rwkv = r.shape
    groups = d_rwkv // MXU_DIM
    tb = min(SCAN_BLOCK, t)
    assert t % tb == 0 and (tb // CHUNK) % SCAN_WAVE == 0
    n_blk = t // tb
    n_sub = tb // CHUNK
    tok_f = pl.BlockSpec((1, tb, MXU_DIM), lambda i, g, j: (i, j, g))
    tok_b = pl.BlockSpec((1, tb, MXU_DIM), lambda i, g, j: (i, n_blk - 1 - j, g))
    st = pl.BlockSpec((1, 2, HEADS_PER_GROUP, HEAD, HEAD), lambda i, g, j: (i, 0, g, 0, 0))
    y_sd = jax.ShapeDtypeStruct((b, t, d_rwkv), BF16)
    return pl.pallas_call(
        functools.partial(_scan_kernel, n_sub=n_sub, n_blk=n_blk),
        grid=(b, groups, n_blk),
        in_specs=[tok_f] * 6 + [tok_b] * 6 + [pl.BlockSpec((1, MXU_DIM), lambda i, g, j: (0, g)), st],
        out_specs=[tok_f, tok_b, st],
        out_shape=[y_sd, y_sd, jax.ShapeDtypeStruct(s0.shape, F32)],
        scratch_shapes=[pltpu.VMEM((MXU_DIM, MXU_DIM), F32), pltpu.VMEM((MXU_DIM, MXU_DIM), F32)],
        compiler_params=_cparams(("parallel", "parallel", "arbitrary")),
        name="scan",
    )(r, k, v, kk, lwf, af, r, k, v, kk, lwb, ab, k_a.reshape(1, d_rwkv), s0)


def _post_kernel(yf_ref, yb_ref, g_ref, bg_ref, lg_ref, lb_ref, ones_ref, o_ref):
    ones_bd = ones_ref[...]
    inv_n = 1.0 / HEAD
    for q in range(o_ref.shape[2] // MXU_DIM):
        sl = slice(q * MXU_DIM, (q + 1) * MXU_DIM)
        y = yf_ref[0, :, sl].astype(F32) + yb_ref[0, :, sl].astype(F32)
        mu = _head_sum(y, ones_bd) * inv_n
        d = y - mu
        var = _head_sum(d * d, ones_bd) * inv_n
        yn = d * lax.rsqrt(var + GN_EPS) * lg_ref[:, sl] + lb_ref[:, sl]
        o_ref[0, :, sl] = (yn * g_ref[0, :, sl].astype(F32) + bg_ref[0, :, sl].astype(F32)).astype(BF16)


def _post(yf, yb, g, bg, lnx_g, lnx_b, ones_bd):
    b, t, d_rwkv = yf.shape
    tp = min(POST_TILE, t)
    tok = pl.BlockSpec((1, tp, d_rwkv), lambda i, j: (i, j, 0))
    par = pl.BlockSpec((1, d_rwkv), lambda i, j: (0, 0))
    return pl.pallas_call(
        _post_kernel,
        grid=(b, t // tp),
        in_specs=[tok] * 4 + [par] * 2 + [pl.BlockSpec((MXU_DIM, MXU_DIM), lambda i, j: (0, 0))],
        out_specs=tok,
        out_shape=jax.ShapeDtypeStruct((b, t, d_rwkv), BF16),
        compiler_params=_cparams(("parallel", "arbitrary")),
        name="rwkv_post",
    )(yf, yb, g, bg, lnx_g.reshape(1, d_rwkv), lnx_b.reshape(1, d_rwkv), ones_bd)


def _window_bounds(n, w):
    t = np.arange(n)
    lo = np.clip(t - w // 2, 0, n)
    hi = np.clip(t - w // 2 + w, 0, n)
    return lo, hi


def _window_matrix(n, w):
    lo, hi = _window_bounds(n, w)
    s = np.arange(n)[None, :]
    member = ((s >= lo[:, None]) & (s < hi[:, None])).astype(np.float32)
    return member, 1.0 / (hi - lo).astype(np.float32)


def _window_tables(block_fn, width):
    bands, invs = [], []
    for w in POOL_WINDOWS:
        member, inv = block_fn(w)
        bands.append(member)
        invs.append(np.broadcast_to(inv[:, None], (inv.shape[0], width)))
    return jnp.asarray(np.stack(bands)).astype(BF16), jnp.asarray(np.stack(invs))


def _window_mean(member_bf16, inv, z):
    z_hi, z_lo = _split2(z)
    return (_dot(member_bf16, z_hi) + _dot(member_bf16, z_lo)) * inv


def _pool_ctx_kernel(z_ref, band_ref, inv_ref, pw_ref, ps_ref, o_ref):
    z = z_ref[0]
    m = _window_mean(band_ref[0], inv_ref[0], z)
    d = (m - z).astype(BF16)
    o_ref[0] = (_dot(d, pw_ref[0]) * ps_ref[...]).astype(BF16)


def _pool_ctx(proj3, col0, pool_w_bf16, pool_scale):
    b, t, _ = proj3.shape
    ng, pg, _ = pool_w_bf16.shape
    band, inv = _window_tables(lambda w: _window_matrix(t, w), pg)
    cb = col0 // pg
    return pl.pallas_call(
        _pool_ctx_kernel,
        grid=(b, ng),
        in_specs=[pl.BlockSpec((1, t, pg), lambda i, q: (i, 0, cb + q)),
                  pl.BlockSpec((1, t, t), lambda i, q: (q, 0, 0)),
                  pl.BlockSpec((1, t, pg), lambda i, q: (q, 0, 0)),
                  pl.BlockSpec((1, pg, pg), lambda i, q: (q, 0, 0)),
                  pl.BlockSpec((1, pg), lambda i, q: (0, q))],
        out_specs=pl.BlockSpec((1, t, pg), lambda i, q: (i, 0, q)),
        out_shape=jax.ShapeDtypeStruct((b, t, ng * pg), BF16),
        compiler_params=_cparams(("parallel", "arbitrary")),
        name="pool_ctx",
    )(proj3, band, inv, pool_w_bf16, pool_scale.reshape(1, ng * pg))


def _pool_lat_kernel(z_ref, band_ref, inv_ref, pw_ref, ps_ref, o_ref, m1_ref, cs_ref, *, n_rows):
    tile = band_ref.shape[1]
    n_tok = n_rows * GRID_W
    band = band_ref[0]
    inv = inv_ref[0]
    for i in range(n_tok // tile):
        sl = slice(i * tile, (i + 1) * tile)
        m1_ref[sl, :] = _window_mean(band, inv, z_ref[0, sl, :])
    cs_ref[0:GRID_W, :] = jnp.zeros((GRID_W, cs_ref.shape[1]), F32)
    for rr in range(n_rows):
        cs_ref[(rr + 1) * GRID_W:(rr + 2) * GRID_W, :] = (
            cs_ref[rr * GRID_W:(rr + 1) * GRID_W, :] + m1_ref[rr * GRID_W:(rr + 1) * GRID_W, :])
    group = pl.program_id(1)
    for gi, w in enumerate(POOL_WINDOWS):
        @pl.when(group == gi)
        def _(w=w):
            lo, hi = _window_bounds(n_rows, w)
            for rr in range(n_rows):
                l, h = int(lo[rr]), int(hi[rr])
                m2 = (cs_ref[h * GRID_W:(h + 1) * GRID_W, :]
                      - cs_ref[l * GRID_W:(l + 1) * GRID_W, :]) / float(h - l)
                sl = slice(rr * GRID_W, (rr + 1) * GRID_W)
                m1_ref[sl, :] = m2 - z_ref[0, sl, :]
    pw = pw_ref[0]
    ps = ps_ref[...]
    for i in range(n_tok // tile):
        sl = slice(i * tile, (i + 1) * tile)
        o_ref[0, sl, :] = (_dot(m1_ref[sl, :].astype(BF16), pw) * ps).astype(BF16)


def _pool_lat(proj3, col0, pool_w_bf16, pool_scale):
    b, t, _ = proj3.shape
    ng, pg, _ = pool_w_bf16.shape
    n_rows = t // GRID_W
    tile = MXU_DIM
    reps = tile // GRID_W

    def grid_row_blocks(w):
        member, inv = _window_matrix(GRID_W, w)
        return np.kron(np.eye(reps, dtype=np.float32), member), np.tile(inv, reps)

    band, inv = _window_tables(grid_row_blocks, pg)
    cb = col0 // pg
    return pl.pallas_call(
        functools.partial(_pool_lat_kernel, n_rows=n_rows),
        grid=(b, ng),
        in_specs=[pl.BlockSpec((1, t, pg), lambda i, q: (i, 0, cb + q)),
                  pl.BlockSpec((1, tile, tile), lambda i, q: (q, 0, 0)),
                  pl.BlockSpec((1, tile, pg), lambda i, q: (q, 0, 0)),
                  pl.BlockSpec((1, pg, pg), lambda i, q: (q, 0, 0)),
                  pl.BlockSpec((1, pg), lambda i, q: (0, q))],
        out_specs=pl.BlockSpec((1, t, pg), lambda i, q: (i, 0, q)),
        out_shape=jax.ShapeDtypeStruct((b, t, ng * pg), BF16),
        scratch_shapes=[pltpu.VMEM((t, pg), F32), pltpu.VMEM((t + GRID_W, pg), F32)],
        compiler_params=_cparams(("parallel", "arbitrary")),
        name="pool_lat",
    )(proj3, band, inv, pool_w_bf16, pool_scale.reshape(1, ng * pg))


ROW_SPLIT = 2


def _resident(shape):
    nd = len(shape)
    return pl.BlockSpec(shape, lambda i: (0,) * nd, pipeline_mode=pl.Buffered(1))


def _up_out_kernel(ya_ref, yb_ref, ga_ref, gb_ref, x_ref, mod_ref, g_ref, wa_ref, wb_ref, wo_ref,
                   x1_ref, h2_ref):
    rows = x_ref.shape[0] // ROW_SPLIT
    for part in range(ROW_SPLIT):
        sl = slice(part * rows, (part + 1) * rows)
        ua = _dot(ya_ref[sl, :], wa_ref[...])
        ub = _dot(yb_ref[sl, :], wb_ref[...])
        merged = (_sigmoid(ga_ref[sl, :]) * ua + _sigmoid(gb_ref[sl, :]) * ub).astype(BF16)
        x1 = x_ref[sl, :] + mod_ref[0, 2:3, :] * _dot(merged, wo_ref[...])
        x1_ref[sl, :] = x1
        y = x1 * lax.rsqrt(jnp.mean(x1 * x1, axis=-1, keepdims=True) + NORM_EPS)
        h2 = (y * g_ref[...]) * (1.0 + mod_ref[0, 4:5, :]) + mod_ref[0, 3:4, :]
        h2_ref[sl, :] = h2.astype(BF16)


def _up_out(ya2, yb2, gates, wa, wb, w_out, x2, mod3, tiles_per_mod_tokens, norm2_g, tm=256):
    m, d = x2.shape
    tm = min(tm, m)
    ka, kb = ya2.shape[1], yb2.shape[1]
    tpm = tiles_per_mod_tokens // tm
    row = lambda width, col: pl.BlockSpec((tm, width), lambda i: (i, col))
    return pl.pallas_call(
        _up_out_kernel,
        grid=(m // tm,),
        in_specs=[row(ka, 0), row(kb, 0), row(d, 0), row(d, 1), row(d, 0),
                  pl.BlockSpec((1, 6, d), lambda i: (i // tpm, 0, 0)),
                  _resident((1, d)), _resident((ka, d)), _resident((kb, d)), _resident((d, d))],
        out_specs=[row(d, 0), row(d, 0)],
        out_shape=[jax.ShapeDtypeStruct((m, d), F32), jax.ShapeDtypeStruct((m, d), BF16)],
        compiler_params=_cparams(("parallel",)),
        name="up_out",
    )(ya2, yb2, gates, gates, x2, mod3, norm2_g.reshape(1, d), wa, wb, w_out)


def _ffn1_kernel(h_ref, w1_ref, w3_ref, o_ref):
    h = h_ref[...]
    u1 = _dot(h, w1_ref[...])
    u3 = _dot(h, w3_ref[...])
    o_ref[...] = (u1 * _sigmoid(u1) * u3).astype(BF16)


def _ffn1(h2, w13, tm=1024, tn=PROJ_TN):
    m, d = h2.shape
    tm = min(tm, m)
    d_ff = w13.shape[1] // 2
    nj = d_ff // tn
    return pl.pallas_call(
        _ffn1_kernel,
        grid=(m // tm, nj),
        in_specs=[pl.BlockSpec((tm, d), lambda i, j: (i, 0)),
                  pl.BlockSpec((d, tn), lambda i, j: (0, j)),
                  pl.BlockSpec((d, tn), lambda i, j: (0, nj + j))],
        out_specs=pl.BlockSpec((tm, tn), lambda i, j: (i, j)),
        out_shape=jax.ShapeDtypeStruct((m, d_ff), BF16),
        compiler_params=_cparams(("parallel", "arbitrary")),
        name="ffn1",
    )(h2, w13, w13)


def _ffn2_kernel(a_ref, x_ref, mod_ref, g_ref, w_ref, o_ref):
    rows = x_ref.shape[0] // ROW_SPLIT
    for part in range(ROW_SPLIT):
        sl = slice(part * rows, (part + 1) * rows)
        x2 = x_ref[sl, :] + mod_ref[0, 5:6, :] * _dot(a_ref[sl, :], w_ref[...])
        y = x2 * lax.rsqrt(jnp.mean(x2 * x2, axis=-1, keepdims=True) + NORM_EPS)
        o_ref[sl, :] = y * g_ref[...]


def _ffn2(act, w2, x1, mod3, tiles_per_mod_tokens, final_g, tm=256):
    m, d = x1.shape
    tm = min(tm, m)
    d_ff = act.shape[1]
    tpm = tiles_per_mod_tokens // tm
    return pl.pallas_call(
        _ffn2_kernel,
        grid=(m // tm,),
        in_specs=[pl.BlockSpec((tm, d_ff), lambda i: (i, 0)),
                  pl.BlockSpec((tm, d), lambda i: (i, 0)),
                  pl.BlockSpec((1, 6, d), lambda i: (i // tpm, 0, 0)),
                  _resident((1, d)), _resident((d_ff, d))],
        out_specs=pl.BlockSpec((tm, d), lambda i: (i, 0)),
        out_shape=jax.ShapeDtypeStruct((m, d), F32),
        compiler_params=_cparams(("parallel",)),
        name="ffn2",
    )(act, x1, mod3, final_g.reshape(1, d), w2)


def _trunk(x, mod, s0, latent, p):
    b, t, d = x.shape
    m = b * t
    d_rwkv = p["d_rwkv"]
    d_pool = d - d_rwkv
    heads = d_rwkv // HEAD
    x2 = x.reshape(m, d)
    nb = mod.shape[0]
    mod3 = mod.reshape(nb, 6, d)
    tokens_per_mod = t if nb == b else m

    proj, gates = _in_proj(x2, p["norm1_g"], mod3, tokens_per_mod, p["w_in"], p["rw_cols_p"] + d_pool)
    proj3 = proj.reshape(b, t, proj.shape[1])

    r, k, v, kk, lwf, lwb, af, ab, g, bg = _prep(
        proj3, p["mu"], p["w0"], p["w2"], p["a0"], p["a2"], p["g2"], p["k_k"], p["k_a"], p["r_k"],
        p["ones_bd"], d_rwkv, p["rw_cols_p"])

    if s0 is None:
        s0 = jnp.zeros((b, 2, heads, HEAD, HEAD), F32)
    yf, yb, s_fin = _scan(r, k, v, kk, lwf, af, lwb, ab, p["k_a"], s0)
    ya = _post(yf, yb, g, bg, p["lnx_g"], p["lnx_b"], p["ones_bd"])

    pool_col0 = p["rw_cols_p"]
    if latent:
        yb_pool = _pool_lat(proj3, pool_col0, p["pool_w"], p["pool_scale"])
    else:
        yb_pool = _pool_ctx(proj3, pool_col0, p["pool_w"], p["pool_scale"])

    x1, h2 = _up_out(ya.reshape(m, d_rwkv), yb_pool.reshape(m, d_pool), gates, p["w_up_a"], p["w_up_b"],
                     p["w_out"], x2, mod3, tokens_per_mod, p["norm2_g"])
    act = _ffn1(h2, p["ffn_w13"])
    y = _ffn2(act, p["ffn_w2"], x1, mod3, tokens_per_mod, p["final_g"])
    return y.reshape(b, t, d), s_fin


def kernel(x_prompt, x_sample, c, state_rwkv, c_ctx, w_mod, b_mod, norm1_g, w_in, shift_mu, w0, w2, a0, a2, g2,
           k_k, k_a, r_k, lnx_g, lnx_b, w_up_a, pool_w, pool_scale, w_up_b, w_out, norm2_g, ffn_w13, ffn_w2,
           final_g):
    depth = w_in.shape[0]
    assert depth == 1, "final norm is fused into the last layer: single-layer trunk only"
    d = x_prompt.shape[-1]
    d_rwkv = w_up_a.shape[1]
    rw_cols = shift_mu.shape[1]
    rw_cols_p = -(-rw_cols // PROJ_TN) * PROJ_TN
    assert x_prompt.shape[1] == SEQ_TILE and x_sample.shape[1] % SEQ_TILE == 0

    l = 0
    w_in_l = w_in[l].astype(BF16)
    w_in_p = jnp.concatenate(
        [w_in_l[:, :rw_cols], jnp.zeros((d, rw_cols_p - rw_cols), BF16), w_in_l[:, rw_cols:]], axis=1)
    ones_bd = jnp.asarray(np.kron(np.eye(HEADS_PER_GROUP, dtype=np.float32),
                                  np.ones((HEAD, HEAD), np.float32))).astype(BF16)
    p = {
        "d_rwkv": d_rwkv, "rw_cols_p": rw_cols_p,
        "norm1_g": norm1_g[l], "w_in": w_in_p,
        "mu": jnp.pad(shift_mu[l], (0, rw_cols_p - rw_cols)).reshape(1, rw_cols_p),
        "w0": w0[l], "w2": w2[l], "a0": a0[l], "a2": a2[l], "g2": g2[l],
        "k_k": k_k[l], "k_a": k_a[l], "r_k": r_k[l].reshape(-1), "lnx_g": lnx_g[l], "lnx_b": lnx_b[l],
        "ones_bd": ones_bd,
        "w_up_a": w_up_a[l].astype(BF16), "w_up_b": w_up_b[l].astype(BF16),
        "pool_w": pool_w[l].astype(BF16), "pool_scale": pool_scale[l],
        "w_out": w_out[l].astype(BF16), "norm2_g": norm2_g[l],
        "ffn_w13": ffn_w13[l].astype(BF16), "ffn_w2": ffn_w2[l].astype(BF16), "final_g": final_g,
    }
    cond = jnp.concatenate([c_ctx[None, :], c], axis=0)
    mod = _mod(cond, w_mod[l], b_mod[l])
    y_prompt, s_ctx = _trunk(x_prompt, mod[:1], None, False, p)
    y_sample, _ = _trunk(x_sample, mod[1:], state_rwkv[:, l], True, p)
    new_state = s_ctx[:, None].astype(x_prompt.dtype)
    return (y_prompt, y_sample, new_state)
```

```python
import functools

import numpy as np
import jax
import jax.numpy as jnp
from jax import lax
from jax.experimental import pallas as pl
from jax.experimental.pallas import tpu as pltpu

F32 = jnp.float32
BF16 = jnp.bfloat16

HEAD = 64
POOL_WINDOWS = (2, 4, 8, 16)
GRID_W = 64
DECAY_LORA = 64
AAA_LORA = 64
GATE_LORA = 128
NORM_EPS = 1e-6
GN_EPS = 64e-5

LANE_TILE = 128
BF16_ROWS = 16
MXU_DIM = 256
HEADS_PER_GROUP = MXU_DIM // HEAD
VMEM_LIMIT = 56 * 1024 * 1024

CHUNK = 64
SEQ_TILE = 256
PREP_TILE = 512
POST_TILE = 512
MOD_TN = 1024
PROJ_TN = 512
SCAN_BLOCK = 2048
SCAN_WAVE = 4


def _cparams(sem):
    return pltpu.CompilerParams(dimension_semantics=sem, vmem_limit_bytes=VMEM_LIMIT)


def _dot(a, b, dims=(((1,), (0,)), ((), ()))):
    return lax.dot_general(a, b, dims, preferred_element_type=F32)


_NN = (((1,), (0,)), ((), ()))
_NT = (((1,), (1,)), ((), ()))


def _split2(x):
    hi = x.astype(BF16)
    lo = (x - hi.astype(F32)).astype(BF16)
    return hi, lo


def _split3(x):
    hi = x.astype(BF16)
    r1 = x - hi.astype(F32)
    mid = r1.astype(BF16)
    lo = (r1 - mid.astype(F32)).astype(BF16)
    return hi, mid, lo


def _mm(a, b, passes):
    if passes == 1:
        return _dot(a.astype(BF16), b.astype(BF16))
    assert passes == 3
    ah, al = _split2(a)
    bh, bl = _split2(b)
    return _dot(ah, bh) + (_dot(ah, bl) + _dot(al, bh))


def _mm_exact_rhs(a, b_bf16):
    a0, a1 = _split2(a)
    return _dot(a0, b_bf16) + _dot(a1, b_bf16)


def _sigmoid(x):
    return 1.0 / (1.0 + jnp.exp(-x))


def _mod_kernel(c_ref, w_ref, b_ref, o_ref):
    c = c_ref[...]
    s = (c * _sigmoid(c)).astype(BF16)
    o_ref[...] = _dot(s, w_ref[...].astype(BF16)) + b_ref[...]


def _mod(cond, w_mod, b_mod):
    nb, d = cond.shape
    rows = -(-nb // BF16_ROWS) * BF16_ROWS
    cond_p = jnp.pad(cond, ((0, rows - nb), (0, 0)))
    n = w_mod.shape[1]
    tn = MOD_TN
    out = pl.pallas_call(
        _mod_kernel,
        grid=(n // tn,),
        in_specs=[pl.BlockSpec((rows, d), lambda j: (0, 0)),
                  pl.BlockSpec((d, tn), lambda j: (0, j)),
                  pl.BlockSpec((1, tn), lambda j: (0, j))],
        out_specs=pl.BlockSpec((rows, tn), lambda j: (0, j)),
        out_shape=jax.ShapeDtypeStruct((rows, n), F32),
        compiler_params=_cparams(("arbitrary",)),
        name="mod",
    )(cond_p, w_mod, b_mod.reshape(1, n))
    return out[:nb]


NORM_ROWS = 16


def _in_kernel(x_ref, g_ref, mod_ref, w_ref, om_ref, og_ref, h_ref, *, nj_main):
    j = pl.program_id(1)

    @pl.when(j == 0)
    def _():
        gain = g_ref[...]
        scale = 1.0 + mod_ref[0, 1:2, :]
        shift = mod_ref[0, 0:1, :]

        def norm_rows(i, carry):
            rows = pl.ds(pl.multiple_of(i * NORM_ROWS, NORM_ROWS), NORM_ROWS)
            x = x_ref[rows, :]
            y = x * lax.rsqrt(jnp.mean(x * x, axis=-1, keepdims=True) + NORM_EPS)
            h_ref[rows, :] = ((y * gain) * scale + shift).astype(BF16)
            return carry

        lax.fori_loop(0, x_ref.shape[0] // NORM_ROWS, norm_rows, 0, unroll=8)

    @pl.when(j < nj_main)
    def _():
        om_ref[...] = _dot(h_ref[...], w_ref[...])

    @pl.when(j >= nj_main)
    def _():
        og_ref[...] = _dot(h_ref[...], w_ref[...])


def _in_proj(x2, norm_g, mod3, tiles_per_mod_tokens, w_bf16, n_main, tm=1024, tn=PROJ_TN):
    m, d = x2.shape
    tm = min(tm, m)
    n = w_bf16.shape[1]
    tpm = tiles_per_mod_tokens // tm
    nj_main = n_main // tn
    return pl.pallas_call(
        functools.partial(_in_kernel, nj_main=nj_main),
        grid=(m // tm, n // tn),
        in_specs=[pl.BlockSpec((tm, d), lambda i, j: (i, 0)),
                  pl.BlockSpec((1, d), lambda i, j: (0, 0)),
                  pl.BlockSpec((1, 6, d), lambda i, j: (i // tpm, 0, 0)),
                  pl.BlockSpec((d, tn), lambda i, j: (0, j))],
        out_specs=[pl.BlockSpec((tm, tn), lambda i, j: (i, jnp.minimum(j, nj_main - 1))),
                   pl.BlockSpec((tm, tn), lambda i, j: (i, jnp.maximum(j - nj_main, 0)))],
        out_shape=[jax.ShapeDtypeStruct((m, n_main), F32), jax.ShapeDtypeStruct((m, n - n_main), F32)],
        scratch_shapes=[pltpu.VMEM((tm, d), BF16)],
        compiler_params=_cparams(("parallel", "arbitrary")),
        name="in_proj",
    )(x2, norm_g.reshape(1, d), mod3, w_bf16)


P_LORA_DECAY = 3
P_LORA = 1
DECAY_GAIN = float(np.exp(-0.5))


def _head_sum(x, ones_bd):
    return _mm_exact_rhs(x, ones_bd)


def _prep_kernel(main_ref, prev_ref, next_ref, mu_ref, w0_ref, w2_ref, a0_ref, a2_ref, g2_ref,
                 kkw_ref, ka_ref, rk_ref, ones_ref, r_o, k_o, v_o, kk_o, lwf_o, lwb_o, af_o, ab_o, g_o, bg_o,
                 *, n_tiles, d_rwkv):
    t = pl.program_id(1)
    rows = main_ref.shape[1]
    row8 = lax.broadcasted_iota(jnp.int32, (8, 1), 0)
    has_prev = t > 0
    has_next = t < n_tiles - 1

    def shifted(c0, c1):
        z = main_ref[0, :, c0:c1]
        p = jnp.where(has_prev, prev_ref[0, 7:8, c0:c1], 0.0)
        n = jnp.where(has_next, next_ref[0, 0:1, c0:c1], 0.0)
        zp = pltpu.roll(z, 1, 0)
        zn = pltpu.roll(z, rows - 1, 0)
        zp = jnp.concatenate([jnp.where(row8 == 0, p, zp[:8]), zp[8:]], axis=0)
        zn = jnp.concatenate([zn[:rows - 8], jnp.where(row8 == 7, n, zn[rows - 8:])], axis=0)
        mu = mu_ref[:, c0:c1]
        return (1.0 - mu) * z + (0.5 * mu) * (zp + zn)

    dr = d_rwkv
    r = shifted(0, dr)
    r_o[0] = r.astype(BF16)
    k = shifted(dr, 2 * dr)
    k_o[0] = k.astype(BF16)
    v = shifted(2 * dr, 3 * dr)
    v_o[0] = v.astype(BF16)

    kkv = k * kkw_ref[...]
    ones_bd = ones_ref[...]
    for j in range(dr // MXU_DIM):
        sl = slice(j * MXU_DIM, (j + 1) * MXU_DIM)
        x = kkv[:, sl]
        ss = _head_sum(x * x, ones_bd)
        kk_o[0, :, sl] = (x * lax.rsqrt(jnp.maximum(ss, 1e-24))).astype(BF16)

    c = 3 * dr
    sm = shifted(c, c + 2 * DECAY_LORA + 2 * AAA_LORA + GATE_LORA)
    o = 0
    for d, lw_o in enumerate((lwf_o, lwb_o)):
        wd = jnp.tanh(sm[:, o:o + DECAY_LORA])
        o += DECAY_LORA
        xw = w0_ref[d:d + 1, :] + _mm(wd, w2_ref[d], passes=P_LORA_DECAY)
        lw_o[0] = -DECAY_GAIN * _sigmoid(xw)
    for d, a_o in enumerate((af_o, ab_o)):
        ad = sm[:, o:o + AAA_LORA]
        o += AAA_LORA
        a_o[0] = _sigmoid(a0_ref[d:d + 1, :] + _mm(ad, a2_ref[d], passes=P_LORA))
    gd = _sigmoid(sm[:, o:o + GATE_LORA])
    g = _mm(gd, g2_ref[...], passes=P_LORA)
    g_o[0] = g.astype(BF16)

    for j in range(dr // MXU_DIM):
        sl = slice(j * MXU_DIM, (j + 1) * MXU_DIM)
        ka = ka_ref[:, sl]
        kj = k[:, sl]
        kd_sum = kj * (1.0 + (af_o[0, :, sl] - 1.0) * ka) + kj * (1.0 + (ab_o[0, :, sl] - 1.0) * ka)
        bonus = _head_sum(r[:, sl] * kd_sum * rk_ref[:, sl], ones_bd) * v[:, sl]
        bg_o[0, :, sl] = (bonus * g[:, sl]).astype(BF16)


def _prep(proj3, mu_p, w0, w2, a0, a2, g2, k_k, k_a, r_k, ones_bd, d_rwkv, rw_cols_p):
    b, t, _ = proj3.shape
    tp = min(PREP_TILE, t)
    n_tiles = t // tp
    sub = tp // 8
    last8 = t // 8 - 1
    out_sd = jax.ShapeDtypeStruct((b, t, d_rwkv), F32)
    half_sd = jax.ShapeDtypeStruct((b, t, d_rwkv), BF16)
    o_spec = pl.BlockSpec((1, tp, d_rwkv), lambda i, j: (i, j, 0))

    def full(a):
        nd = a.ndim
        return pl.BlockSpec(a.shape, lambda i, j: (0,) * nd)

    params = (mu_p, w0, w2, a0, a2, g2, k_k.reshape(1, d_rwkv), k_a.reshape(1, d_rwkv),
              r_k.reshape(1, d_rwkv), ones_bd)
    return pl.pallas_call(
        functools.partial(_prep_kernel, n_tiles=n_tiles, d_rwkv=d_rwkv),
        grid=(b, n_tiles),
        in_specs=[pl.BlockSpec((1, tp, rw_cols_p), lambda i, j: (i, j, 0)),
                  pl.BlockSpec((1, 8, rw_cols_p), lambda i, j: (i, jnp.maximum(j * sub - 1, 0), 0)),
                  pl.BlockSpec((1, 8, rw_cols_p), lambda i, j: (i, jnp.minimum((j + 1) * sub, last8), 0)),
                  ] + [full(a) for a in params],
        out_specs=[o_spec] * 10,
        out_shape=[half_sd] * 4 + [out_sd] * 4 + [half_sd] * 2,
        compiler_params=_cparams(("parallel", "arbitrary")),
        name="rwkv_prep",
    )(proj3, proj3, proj3, *params)


def _block_diag_rhs(y, bd):
    yb = y.astype(BF16)
    per_tile = LANE_TILE // HEAD
    zero = jnp.zeros((y.shape[0], LANE_TILE), BF16)
    rows = []
    for h in range(HEADS_PER_GROUP):
        t = h // per_tile
        tiles = [zero] * (MXU_DIM // LANE_TILE)
        tiles[t] = yb[:, t * LANE_TILE:(t + 1) * LANE_TILE] * bd[h % per_tile]
        rows.append(jnp.concatenate(tiles, axis=1))
    return jnp.concatenate(rows, axis=0)


def _mmc(x, y_bd):
    return _dot(x.astype(BF16), y_bd)


def _scan_masks(rev):
    c = CHUNK
    ti = lax.broadcasted_iota(jnp.int32, (c, c), 0)
    si = lax.broadcasted_iota(jnp.int32, (c, c), 1)
    tri = jnp.where((si >= ti) if rev else (si <= ti), 1.0, 0.0).astype(BF16)
    tc = lax.broadcasted_iota(jnp.int32, (c, MXU_DIM), 0)
    sc = lax.broadcasted_iota(jnp.int32, (c, MXU_DIM), 1) % c
    before = (sc > tc) if rev else (sc < tc)
    upto = (sc >= tc) if rev else (sc <= tc)
    return tri, before, upto


def _scan_chunk_local(out, refs, idx, dir_masks, blk_masks, ka, last_row, bd):
    r_ref, k_ref, v_ref, kk_ref, lw_ref, a_ref = refs
    tri, before, upto = dir_masks
    eye, same16, off32, off64 = blk_masks
    c = CHUNK
    sl, ls = idx
    kk = kk_ref[0, sl, ls].astype(F32)
    lw = lw_ref[0, sl, ls]
    a = a_ref[0, sl, ls]
    lw0, lw1, lw2 = _split3(lw)
    cum = _dot(tri, lw0) + (_dot(tri, lw1) + _dot(tri, lw2))
    yield
    e_in = jnp.exp(cum)
    e_inv = jnp.exp(-cum)
    al = kk * jnp.exp(cum - lw)
    rh = r_ref[0, sl, ls].astype(F32) * e_in
    be = (kk * a) * e_inv
    kap = (k_ref[0, sl, ls].astype(F32) * (1.0 + (a - 1.0) * ka)) * e_inv
    ptot = e_in[last_row:last_row + 1, :]
    ar = jnp.concatenate([al, rh], axis=0).astype(BF16)
    ab = _dot(ar, _block_diag_rhs(be, bd), _NT)
    ak = _dot(ar, _block_diag_rhs(kap, bd), _NT)
    yield
    lm = jnp.where(before, ab[:c], 0.0)
    aak = jnp.where(before, ak[:c], 0.0)
    kq = jnp.where(upto, ak[c:], 0.0)
    out["bq"] = jnp.where(upto, ab[c:], 0.0).astype(BF16)
    ld = jnp.where(same16, lm, 0.0)
    p = eye - ld
    ld_bd = _block_diag_rhs(ld, bd)
    l2 = _mmc(ld, ld_bd)
    yield
    pl2 = _mmc(jnp.concatenate([p, l2], axis=0), _block_diag_rhs(l2, bd))
    yield
    p = p + pl2[:c]
    l4 = pl2[c:]
    pl4 = _mmc(jnp.concatenate([p, l4], axis=0), _block_diag_rhs(l4, bd))
    yield
    p = p + pl4[:c]
    p8 = _mmc(p, _block_diag_rhs(pl4[c:], bd))
    v = v_ref[0, sl, ls]
    akv = _mmc(jnp.concatenate([aak, kq], axis=0), _block_diag_rhs(v, bd))
    aak_v = akv[:c]
    out["kq_v"] = akv[c:]
    yield
    p = p + p8
    for off in (off32, off64):
        lo = jnp.where(off, lm, 0.0)
        t1 = _mmc(lo, _block_diag_rhs(p, bd))
        yield
        t2 = _mmc(p, _block_diag_rhs(t1, bd))
        yield
        p = p - t2
    wt = _mmc(p, _block_diag_rhs(al, bd))
    out["ut"] = _mmc(p, _block_diag_rhs(aak_v, bd))
    yield
    out["wr"] = jnp.concatenate([wt, rh], axis=0).astype(BF16)
    out["v"] = v.astype(BF16)
    out["bkT"] = jnp.transpose(jnp.concatenate([be, kap], axis=0)).astype(BF16)
    out["ptot"] = ptot


def _scan_chain(items, m_scr, y_ref, bd, bd_mask):
    c = CHUNK
    for loc, (sl, ls) in items:
        m = m_scr[...]
        zs = _dot(loc["wr"], m.astype(BF16))
        yield
        u = -(zs[:c] + loc["ut"])
        uv = jnp.concatenate([u.astype(BF16), loc["v"]], axis=0)
        dm = _dot(loc["bkT"], uv)
        yu = _mmc(loc["bq"], _block_diag_rhs(u, bd))
        yield
        y_ref[0, sl, ls] = (zs[c:] + loc["kq_v"] + yu).astype(BF16)
        pc = jnp.transpose(jnp.broadcast_to(loc["ptot"], (8, MXU_DIM)))
        m_scr[...] = jnp.broadcast_to(pc[:, 0:1], (MXU_DIM, MXU_DIM)) * (m + jnp.where(bd_mask, dm, 0.0))


def _run_lockstep(gens):
    gens = list(gens)
    while gens:
        alive = []
        for g in gens:
            try:
                next(g)
                alive.append(g)
            except StopIteration:
                pass
        gens = alive


def _scan_kernel(rf_ref, kf_ref, vf_ref, kkf_ref, lwf_ref, af_ref, rb_ref, kb_ref, vb_ref, kkb_ref, lwb_ref,
                 ab_ref, ka_ref, s0_ref, yf_ref, yb_ref, sfin_ref, *m_scr, n_sub, n_blk):
    c = CHUNK
    blk = pl.program_id(2)
    n_grp = len(m_scr) // 2

    @pl.when(blk == 0)
    def _():
        zero = jnp.zeros((HEAD, HEAD), F32)
        for gq in range(n_grp):
            for d in range(2):
                for h in range(HEADS_PER_GROUP):
                    blocks = [zero] * HEADS_PER_GROUP
                    blocks[h] = jnp.transpose(s0_ref[0, d, gq * HEADS_PER_GROUP + h])
                    m_scr[2 * gq + d][h * HEAD:(h + 1) * HEAD, :] = jnp.concatenate(blocks, axis=1)

    tc = lax.broadcasted_iota(jnp.int32, (c, MXU_DIM), 0)
    sc = lax.broadcasted_iota(jnp.int32, (c, MXU_DIM), 1) % c
    same16 = (sc // 16) == (tc // 16)
    same32 = (sc // 32) == (tc // 32)
    blk_masks = (jnp.where(sc == tc, 1.0, 0.0), same16,
                 jnp.logical_and(same32, jnp.logical_not(same16)), jnp.logical_not(same32))
    bd_mask = (lax.broadcasted_iota(jnp.int32, (MXU_DIM, MXU_DIM), 0) // HEAD) == \
              (lax.broadcasted_iota(jnp.int32, (MXU_DIM, MXU_DIM), 1) // HEAD)
    lane_head = lax.broadcasted_iota(jnp.int32, (c, LANE_TILE), 1) // HEAD
    bd = tuple(jnp.where(lane_head == q, 1.0, 0.0).astype(BF16) for q in range(LANE_TILE // HEAD))
    refs_f = (rf_ref, kf_ref, vf_ref, kkf_ref, lwf_ref, af_ref)
    refs_b = (rb_ref, kb_ref, vb_ref, kkb_ref, lwb_ref, ab_ref)
    masks_f = _scan_masks(False)
    masks_b = _scan_masks(True)
    slices = [slice(j * c, (j + 1) * c) for j in range(n_sub)]
    chains = []
    for gq in range(n_grp):
        ls = slice(gq * MXU_DIM, (gq + 1) * MXU_DIM)
        ka = ka_ref[:, ls]
        items_f = [({}, (sl, ls)) for sl in slices]
        items_b = [({}, (sl, ls)) for sl in reversed(slices)]
        for w0 in range(0, n_sub, SCAN_WAVE):
            wave_f = items_f[w0:w0 + SCAN_WAVE]
            wave_b = items_b[w0:w0 + SCAN_WAVE]
            gens = []
            for (of, sf), (ob, sb) in zip(wave_f, wave_b):
                gens.append(_scan_chunk_local(of, refs_f, sf, masks_f, blk_masks, ka, c - 1, bd))
                gens.append(_scan_chunk_local(ob, refs_b, sb, masks_b, blk_masks, ka, 0, bd))
            _run_lockstep(gens + chains)
            chains = [_scan_chain(wave_f, m_scr[2 * gq], yf_ref, bd, bd_mask),
                      _scan_chain(wave_b, m_scr[2 * gq + 1], yb_ref, bd, bd_mask)]
    _run_lockstep(chains)

    @pl.when(blk == n_blk - 1)
    def _():
        for gq in range(n_grp):
            for d in range(2):
                for h in range(HEADS_PER_GROUP):
                    rows = m_scr[2 * gq + d][h * HEAD:(h + 1) * HEAD, :]
                    sfin_ref[0, d, gq * HEADS_PER_GROUP + h] = jnp.transpose(rows[:, h * HEAD:(h + 1) * HEAD])


def _scan(r, k, v, kk, lwf, af, lwb, ab, k_a, s0):
    b, t, d_rwkv = r.shape
    groups = d_rwkv // MXU_DIM
    tb = min(SCAN_BLOCK, t)
    assert t % tb == 0 and (tb // CHUNK) % SCAN_WAVE == 0
    n_blk = t // tb
    n_sub = tb // CHUNK
    n_grp = 2 if (n_blk == 1 and groups % 2 == 0) else 1
    lanes = n_grp * MXU_DIM
    tok_f = pl.BlockSpec((1, tb, lanes), lambda i, g, j: (i, j, g))
    tok_b = pl.BlockSpec((1, tb, lanes), lambda i, g, j: (i, n_blk - 1 - j, g))
    st = pl.BlockSpec((1, 2, n_grp * HEADS_PER_GROUP, HEAD, HEAD), lambda i, g, j: (i, 0, g, 0, 0))
    y_sd = jax.ShapeDtypeStruct((b, t, d_rwkv), BF16)
    return pl.pallas_call(
        functools.partial(_scan_kernel, n_sub=n_sub, n_blk=n_blk),
        grid=(b, groups // n_grp, n_blk),
        in_specs=[tok_f] * 6 + [tok_b] * 6 + [pl.BlockSpec((1, lanes), lambda i, g, j: (0, g)), st],
        out_specs=[tok_f, tok_b, st],
        out_shape=[y_sd, y_sd, jax.ShapeDtypeStruct(s0.shape, F32)],
        scratch_shapes=[pltpu.VMEM((MXU_DIM, MXU_DIM), F32)] * (2 * n_grp),
        compiler_params=_cparams(("parallel", "parallel", "arbitrary")),
        name="scan",
    )(r, k, v, kk, lwf, af, r, k, v, kk, lwb, ab, k_a.reshape(1, d_rwkv), s0)


def _post_kernel(yf_ref, yb_ref, g_ref, bg_ref, lg_ref, lb_ref, ones_ref, o_ref):
    ones_bd = ones_ref[...]
    inv_n = 1.0 / HEAD
    for q in range(o_ref.shape[2] // MXU_DIM):
        sl = slice(q * MXU_DIM, (q + 1) * MXU_DIM)
        y = yf_ref[0, :, sl].astype(F32) + yb_ref[0, :, sl].astype(F32)
        mu = _head_sum(y, ones_bd) * inv_n
        d = y - mu
        var = _head_sum(d * d, ones_bd) * inv_n
        yn = d * lax.rsqrt(var + GN_EPS) * lg_ref[:, sl] + lb_ref[:, sl]
        o_ref[0, :, sl] = (yn * g_ref[0, :, sl].astype(F32) + bg_ref[0, :, sl].astype(F32)).astype(BF16)


def _post(yf, yb, g, bg, lnx_g, lnx_b, ones_bd):
    b, t, d_rwkv = yf.shape
    tp = min(POST_TILE, t)
    tok = pl.BlockSpec((1, tp, d_rwkv), lambda i, j: (i, j, 0))
    par = pl.BlockSpec((1, d_rwkv), lambda i, j: (0, 0))
    return pl.pallas_call(
        _post_kernel,
        grid=(b, t // tp),
        in_specs=[tok] * 4 + [par] * 2 + [pl.BlockSpec((MXU_DIM, MXU_DIM), lambda i, j: (0, 0))],
        out_specs=tok,
        out_shape=jax.ShapeDtypeStruct((b, t, d_rwkv), BF16),
        compiler_params=_cparams(("parallel", "arbitrary")),
        name="rwkv_post",
    )(yf, yb, g, bg, lnx_g.reshape(1, d_rwkv), lnx_b.reshape(1, d_rwkv), ones_bd)


def _window_bounds(n, w):
    t = np.arange(n)
    lo = np.clip(t - w // 2, 0, n)
    hi = np.clip(t - w // 2 + w, 0, n)
    return lo, hi


def _window_matrix(n, w):
    lo, hi = _window_bounds(n, w)
    s = np.arange(n)[None, :]
    member = ((s >= lo[:, None]) & (s < hi[:, None])).astype(np.float32)
    return member, 1.0 / (hi - lo).astype(np.float32)


def _window_tables(block_fn, width):
    bands, invs = [], []
    for w in POOL_WINDOWS:
        member, inv = block_fn(w)
        bands.append(member)
        invs.append(np.broadcast_to(inv[:, None], (inv.shape[0], width)))
    return jnp.asarray(np.stack(bands)).astype(BF16), jnp.asarray(np.stack(invs))


def _window_mean(member_bf16, inv, z):
    z_hi, z_lo = _split2(z)
    return (_dot(member_bf16, z_hi) + _dot(member_bf16, z_lo)) * inv


def _pool_ctx_kernel(z_ref, band_ref, inv_ref, pw_ref, ps_ref, o_ref):
    z = z_ref[0]
    m = _window_mean(band_ref[0], inv_ref[0], z)
    d = (m - z).astype(BF16)
    o_ref[0] = (_dot(d, pw_ref[0]) * ps_ref[...]).astype(BF16)


def _pool_ctx(proj3, col0, pool_w_bf16, pool_scale):
    b, t, _ = proj3.shape
    ng, pg, _ = pool_w_bf16.shape
    band, inv = _window_tables(lambda w: _window_matrix(t, w), pg)
    cb = col0 // pg
    return pl.pallas_call(
        _pool_ctx_kernel,
        grid=(b, ng),
        in_specs=[pl.BlockSpec((1, t, pg), lambda i, q: (i, 0, cb + q)),
                  pl.BlockSpec((1, t, t), lambda i, q: (q, 0, 0)),
                  pl.BlockSpec((1, t, pg), lambda i, q: (q, 0, 0)),
                  pl.BlockSpec((1, pg, pg), lambda i, q: (q, 0, 0)),
                  pl.BlockSpec((1, pg), lambda i, q: (0, q))],
        out_specs=pl.BlockSpec((1, t, pg), lambda i, q: (i, 0, q)),
        out_shape=jax.ShapeDtypeStruct((b, t, ng * pg), BF16),
        compiler_params=_cparams(("parallel", "arbitrary")),
        name="pool_ctx",
    )(proj3, band, inv, pool_w_bf16, pool_scale.reshape(1, ng * pg))


def _pool_lat_kernel(z_ref, band_ref, inv_ref, pw_ref, ps_ref, o_ref, m1_ref, cs_ref, *, n_rows):
    tile = band_ref.shape[1]
    n_tok = n_rows * GRID_W
    band = band_ref[0]
    inv = inv_ref[0]
    for i in range(n_tok // tile):
        sl = slice(i * tile, (i + 1) * tile)
        m1_ref[sl, :] = _window_mean(band, inv, z_ref[0, sl, :])
    cs_ref[0:GRID_W, :] = jnp.zeros((GRID_W, cs_ref.shape[1]), F32)
    for rr in range(n_rows):
        cs_ref[(rr + 1) * GRID_W:(rr + 2) * GRID_W, :] = (
            cs_ref[rr * GRID_W:(rr + 1) * GRID_W, :] + m1_ref[rr * GRID_W:(rr + 1) * GRID_W, :])
    group = pl.program_id(1)
    for gi, w in enumerate(POOL_WINDOWS):
        @pl.when(group == gi)
        def _(w=w):
            lo, hi = _window_bounds(n_rows, w)
            for rr in range(n_rows):
                l, h = int(lo[rr]), int(hi[rr])
                m2 = (cs_ref[h * GRID_W:(h + 1) * GRID_W, :]
                      - cs_ref[l * GRID_W:(l + 1) * GRID_W, :]) / float(h - l)
                sl = slice(rr * GRID_W, (rr + 1) * GRID_W)
                m1_ref[sl, :] = m2 - z_ref[0, sl, :]
    pw = pw_ref[0]
    ps = ps_ref[...]
    for i in range(n_tok // tile):
        sl = slice(i * tile, (i + 1) * tile)
        o_ref[0, sl, :] = (_dot(m1_ref[sl, :].astype(BF16), pw) * ps).astype(BF16)


def _pool_lat(proj3, col0, pool_w_bf16, pool_scale):
    b, t, _ = proj3.shape
    ng, pg, _ = pool_w_bf16.shape
    n_rows = t // GRID_W
    tile = MXU_DIM
    reps = tile // GRID_W

    def grid_row_blocks(w):
        member, inv = _window_matrix(GRID_W, w)
        return np.kron(np.eye(reps, dtype=np.float32), member), np.tile(inv, reps)

    band, inv = _window_tables(grid_row_blocks, pg)
    cb = col0 // pg
    return pl.pallas_call(
        functools.partial(_pool_lat_kernel, n_rows=n_rows),
        grid=(b, ng),
        in_specs=[pl.BlockSpec((1, t, pg), lambda i, q: (i, 0, cb + q)),
                  pl.BlockSpec((1, tile, tile), lambda i, q: (q, 0, 0)),
                  pl.BlockSpec((1, tile, pg), lambda i, q: (q, 0, 0)),
                  pl.BlockSpec((1, pg, pg), lambda i, q: (q, 0, 0)),
                  pl.BlockSpec((1, pg), lambda i, q: (0, q))],
        out_specs=pl.BlockSpec((1, t, pg), lambda i, q: (i, 0, q)),
        out_shape=jax.ShapeDtypeStruct((b, t, ng * pg), BF16),
        scratch_shapes=[pltpu.VMEM((t, pg), F32), pltpu.VMEM((t + GRID_W, pg), F32)],
        compiler_params=_cparams(("parallel", "arbitrary")),
        name="pool_lat",
    )(proj3, band, inv, pool_w_bf16, pool_scale.reshape(1, ng * pg))


ROW_SPLIT = 2


def _resident(shape):
    nd = len(shape)
    return pl.BlockSpec(shape, lambda i: (0,) * nd, pipeline_mode=pl.Buffered(1))


def _up_out_kernel(ya_ref, yb_ref, ga_ref, gb_ref, x_ref, mod_ref, g_ref, wa_ref, wb_ref, wo_ref,
                   x1_ref, h2_ref):
    rows = x_ref.shape[0] // ROW_SPLIT
    for part in range(ROW_SPLIT):
        sl = slice(part * rows, (part + 1) * rows)
        ua = _dot(ya_ref[sl, :], wa_ref[...])
        ub = _dot(yb_ref[sl, :], wb_ref[...])
        merged = (_sigmoid(ga_ref[sl, :]) * ua + _sigmoid(gb_ref[sl, :]) * ub).astype(BF16)
        x1 = x_ref[sl, :] + mod_ref[0, 2:3, :] * _dot(merged, wo_ref[...])
        x1_ref[sl, :] = x1
        y = x1 * lax.rsqrt(jnp.mean(x1 * x1, axis=-1, keepdims=True) + NORM_EPS)
        h2 = (y * g_ref[...]) * (1.0 + mod_ref[0, 4:5, :]) + mod_ref[0, 3:4, :]
        h2_ref[sl, :] = h2.astype(BF16)


def _up_out(ya2, yb2, gates, wa, wb, w_out, x2, mod3, tiles_per_mod_tokens, norm2_g, tm=256):
    m, d = x2.shape
    tm = min(tm, m)
    ka, kb = ya2.shape[1], yb2.shape[1]
    tpm = tiles_per_mod_tokens // tm
    row = lambda width, col: pl.BlockSpec((tm, width), lambda i: (i, col))
    return pl.pallas_call(
        _up_out_kernel,
        grid=(m // tm,),
        in_specs=[row(ka, 0), row(kb, 0), row(d, 0), row(d, 1), row(d, 0),
                  pl.BlockSpec((1, 6, d), lambda i: (i // tpm, 0, 0)),
                  _resident((1, d)), _resident((ka, d)), _resident((kb, d)), _resident((d, d))],
        out_specs=[row(d, 0), row(d, 0)],
        out_shape=[jax.ShapeDtypeStruct((m, d), F32), jax.ShapeDtypeStruct((m, d), BF16)],
        compiler_params=_cparams(("parallel",)),
        name="up_out",
    )(ya2, yb2, gates, gates, x2, mod3, norm2_g.reshape(1, d), wa, wb, w_out)


def _ffn1_kernel(h_ref, w1_ref, w3_ref, o_ref):
    h = h_ref[...]
    u1 = _dot(h, w1_ref[...])
    u3 = _dot(h, w3_ref[...])
    o_ref[...] = (u1 * _sigmoid(u1) * u3).astype(BF16)


def _ffn1(h2, w13, tm=1024, tn=PROJ_TN):
    m, d = h2.shape
    tm = min(tm, m)
    d_ff = w13.shape[1] // 2
    nj = d_ff // tn
    return pl.pallas_call(
        _ffn1_kernel,
        grid=(m // tm, nj),
        in_specs=[pl.BlockSpec((tm, d), lambda i, j: (i, 0)),
                  pl.BlockSpec((d, tn), lambda i, j: (0, j)),
                  pl.BlockSpec((d, tn), lambda i, j: (0, nj + j))],
        out_specs=pl.BlockSpec((tm, tn), lambda i, j: (i, j)),
        out_shape=jax.ShapeDtypeStruct((m, d_ff), BF16),
        compiler_params=_cparams(("parallel", "arbitrary")),
        name="ffn1",
    )(h2, w13, w13)


def _ffn2_kernel(a_ref, x_ref, mod_ref, g_ref, w_ref, o_ref):
    rows = x_ref.shape[0] // ROW_SPLIT
    for part in range(ROW_SPLIT):
        sl = slice(part * rows, (part + 1) * rows)
        x2 = x_ref[sl, :] + mod_ref[0, 5:6, :] * _dot(a_ref[sl, :], w_ref[...])
        y = x2 * lax.rsqrt(jnp.mean(x2 * x2, axis=-1, keepdims=True) + NORM_EPS)
        o_ref[sl, :] = y * g_ref[...]


def _ffn2(act, w2, x1, mod3, tiles_per_mod_tokens, final_g, tm=256):
    m, d = x1.shape
    tm = min(tm, m)
    d_ff = act.shape[1]
    tpm = tiles_per_mod_tokens // tm
    return pl.pallas_call(
        _ffn2_kernel,
        grid=(m // tm,),
        in_specs=[pl.BlockSpec((tm, d_ff), lambda i: (i, 0)),
                  pl.BlockSpec((tm, d), lambda i: (i, 0)),
                  pl.BlockSpec((1, 6, d), lambda i: (i // tpm, 0, 0)),
                  _resident((1, d)), _resident((d_ff, d))],
        out_specs=pl.BlockSpec((tm, d), lambda i: (i, 0)),
        out_shape=jax.ShapeDtypeStruct((m, d), F32),
        compiler_params=_cparams(("parallel",)),
        name="ffn2",
    )(act, x1, mod3, final_g.reshape(1, d), w2)


def _trunk(x, mod, s0, latent, p):
    b, t, d = x.shape
    m = b * t
    d_rwkv = p["d_rwkv"]
    d_pool = d - d_rwkv
    heads = d_rwkv // HEAD
    x2 = x.reshape(m, d)
    nb = mod.shape[0]
    mod3 = mod.reshape(nb, 6, d)
    tokens_per_mod = t if nb == b else m

    proj, gates = _in_proj(x2, p["norm1_g"], mod3, tokens_per_mod, p["w_in"], p["rw_cols_p"] + d_pool)
    proj3 = proj.reshape(b, t, proj.shape[1])

    r, k, v, kk, lwf, lwb, af, ab, g, bg = _prep(
        proj3, p["mu"], p["w0"], p["w2"], p["a0"], p["a2"], p["g2"], p["k_k"], p["k_a"], p["r_k"],
        p["ones_bd"], d_rwkv, p["rw_cols_p"])

    if s0 is None:
        s0 = jnp.zeros((b, 2, heads, HEAD, HEAD), F32)
    yf, yb, s_fin = _scan(r, k, v, kk, lwf, af, lwb, ab, p["k_a"], s0)
    ya = _post(yf, yb, g, bg, p["lnx_g"], p["lnx_b"], p["ones_bd"])

    pool_col0 = p["rw_cols_p"]
    if latent:
        yb_pool = _pool_lat(proj3, pool_col0, p["pool_w"], p["pool_scale"])
    else:
        yb_pool = _pool_ctx(proj3, pool_col0, p["pool_w"], p["pool_scale"])

    x1, h2 = _up_out(ya.reshape(m, d_rwkv), yb_pool.reshape(m, d_pool), gates, p["w_up_a"], p["w_up_b"],
                     p["w_out"], x2, mod3, tokens_per_mod, p["norm2_g"])
    act = _ffn1(h2, p["ffn_w13"])
    y = _ffn2(act, p["ffn_w2"], x1, mod3, tokens_per_mod, p["final_g"])
    return y.reshape(b, t, d), s_fin


def kernel(x_prompt, x_sample, c, state_rwkv, c_ctx, w_mod, b_mod, norm1_g, w_in, shift_mu, w0, w2, a0, a2, g2,
           k_k, k_a, r_k, lnx_g, lnx_b, w_up_a, pool_w, pool_scale, w_up_b, w_out, norm2_g, ffn_w13, ffn_w2,
           final_g):
    depth = w_in.shape[0]
    assert depth == 1, "final norm is fused into the last layer: single-layer trunk only"
    d = x_prompt.shape[-1]
    d_rwkv = w_up_a.shape[1]
    rw_cols = shift_mu.shape[1]
    rw_cols_p = -(-rw_cols // PROJ_TN) * PROJ_TN
    assert x_prompt.shape[1] == SEQ_TILE and x_sample.shape[1] % SEQ_TILE == 0

    l = 0
    w_in_l = w_in[l].astype(BF16)
    w_in_p = jnp.concatenate(
        [w_in_l[:, :rw_cols], jnp.zeros((d, rw_cols_p - rw_cols), BF16), w_in_l[:, rw_cols:]], axis=1)
    ones_bd = jnp.asarray(np.kron(np.eye(HEADS_PER_GROUP, dtype=np.float32),
                                  np.ones((HEAD, HEAD), np.float32))).astype(BF16)
    p = {
        "d_rwkv": d_rwkv, "rw_cols_p": rw_cols_p,
        "norm1_g": norm1_g[l], "w_in": w_in_p,
        "mu": jnp.pad(shift_mu[l], (0, rw_cols_p - rw_cols)).reshape(1, rw_cols_p),
        "w0": w0[l], "w2": w2[l], "a0": a0[l], "a2": a2[l], "g2": g2[l],
        "k_k": k_k[l], "k_a": k_a[l], "r_k": r_k[l].reshape(-1), "lnx_g": lnx_g[l], "lnx_b": lnx_b[l],
        "ones_bd": ones_bd,
        "w_up_a": w_up_a[l].astype(BF16), "w_up_b": w_up_b[l].astype(BF16),
        "pool_w": pool_w[l].astype(BF16), "pool_scale": pool_scale[l],
        "w_out": w_out[l].astype(BF16), "norm2_g": norm2_g[l],
        "ffn_w13": ffn_w13[l].astype(BF16), "ffn_w2": ffn_w2[l].astype(BF16), "final_g": final_g,
    }
    cond = jnp.concatenate([c_ctx[None, :], c], axis=0)
    mod = _mod(cond, w_mod[l], b_mod[l])
    y_prompt, s_ctx = _trunk(x_prompt, mod[:1], None, False, p)
    y_sample, _ = _trunk(x_sample, mod[1:], state_rwkv[:, l], True, p)
    new_state = s_ctx[:, None].astype(x_prompt.dtype)
    return (y_prompt, y_sample, new_state)
```
